```python
import math
import jax, jax.numpy as jnp
from jax import lax
import numpy as np

D_MODEL = 2048
BATCH = 4
SEQ = 2048
DEPTH = 2
DEC_BATCH = 8
DEC_SEQ = 4
PAST_LEN = 16384
PAGE_SIZE = 128

MIX_WIDTH = D_MODEL
N_MIXERS = 4
GROUP_WIDTH = MIX_WIDTH // N_MIXERS
SSD_HEADDIM = 64
SSD_HEADS = GROUP_WIDTH // SSD_HEADDIM
SSD_GROUPS = 2
SSD_STATE = 128
SSD_CONV = 4
SSD_CHUNK = 128
SSD_CONV_DIM = GROUP_WIDTH + 2 * SSD_GROUPS * SSD_STATE
SSD_COLS = GROUP_WIDTH + SSD_CONV_DIM + SSD_HEADS
MOBA_HEAD_DIM = 128
MOBA_HEADS = GROUP_WIDTH // MOBA_HEAD_DIM
MOBA_BLOCK = 256
MOBA_TOPK = 3
MOBA_Q_CHUNK = 16
MOBA_COLS = 3 * GROUP_WIDTH
ROPE_THETA = 10000.0
GMLP_CHUNK = 128
GMLP_HEADS = 4
GMLP_HEAD_DIM = GROUP_WIDTH // GMLP_HEADS
GMLP_COLS = 2 * GROUP_WIDTH
RWKV_HEAD = 64
RWKV_HEADS = GROUP_WIDTH // RWKV_HEAD
RWKV_DECAY_RANK = 64
RWKV_ICLR_RANK = 64
RWKV_GATE_RANK = 128
RWKV_COLS = 3 * GROUP_WIDTH + RWKV_DECAY_RANK + RWKV_ICLR_RANK + RWKV_GATE_RANK
RWKV_DECAY_SCALE = 0.606531
RWKV_LN_EPS = 64e-5
IN_COLS = SSD_COLS + MOBA_COLS + GMLP_COLS + RWKV_COLS
D_FF = 5632
N_MOD = 9
FFN_RES = 0.5
EPS = 1e-6
NEG = -1e30
F32 = jnp.float32

kernel_name = 'hymba_ssd_moba_gmlp_rwkv7_macaron_step'


def rms_norm(x, w, eps=EPS):
    xf = x.astype(F32)
    y = xf * lax.rsqrt(jnp.mean(xf * xf, -1, keepdims=True) + eps)
    return (y * w.astype(F32)).astype(x.dtype)


def modulate(x, w, shift, scale):
    return rms_norm(x, w) * (1 + scale) + shift


def swiglu(h, w1, w3, w2):
    return (jax.nn.silu(h @ w1) * (h @ w3)) @ w2


def rope(x, pos):
    half = x.shape[-1] // 2
    freq = ROPE_THETA ** (-jnp.arange(half, dtype=F32) / half)
    ang = pos.astype(F32)[:, None] * freq[None, :]
    cos = jnp.cos(ang)[None, :, None, :]
    sin = jnp.sin(ang)[None, :, None, :]
    xf = x.astype(F32)
    x1, x2 = xf[..., :half], xf[..., half:]
    return jnp.concatenate([x1 * cos - x2 * sin, x1 * sin + x2 * cos], -1).astype(x.dtype)


def causal_conv(x, prev, w, b):
    xp = jnp.concatenate([prev.astype(x.dtype), x], 1)
    L = x.shape[1]
    out = b + xp[:, 0:L] * w[:, 0]
    for i in range(1, SSD_CONV):
        out = out + xp[:, i:i + L] * w[:, i]
    return out, xp[:, -(SSD_CONV - 1):]


def segsum(a):
    T = a.shape[-1]
    cs = jnp.cumsum(a, -1)
    d = cs[..., :, None] - cs[..., None, :]
    return jnp.where(jnp.tril(jnp.ones((T, T), bool)), d, -jnp.inf)


def ssd_scan(x, dt, A, B, C, h0):
    b, l, h, p = x.shape
    q = SSD_CHUNK if l % SSD_CHUNK == 0 else l
    c = l // q
    rep = h // B.shape[2]
    B = jnp.repeat(B, rep, axis=2).reshape(b, c, q, h, -1)
    C = jnp.repeat(C, rep, axis=2).reshape(b, c, q, h, -1)
    xdt = (x * dt[..., None]).reshape(b, c, q, h, p)
    a = (dt * A).reshape(b, c, q, h).transpose(0, 3, 1, 2)
    a_cs = jnp.cumsum(a, -1)
    Lmat = jnp.exp(segsum(a))
    scores = jnp.einsum('bclhn,bcshn->bhcls', C, B) * Lmat
    y_diag = jnp.einsum('bhcls,bcshp->bclhp', scores, xdt)
    decay_states = jnp.exp(a_cs[..., -1:] - a_cs)
    states = jnp.einsum('bclhn,bhcl,bclhp->bchpn', B, decay_states, xdt)
    states = jnp.concatenate([h0[:, None], states], 1)
    chunk_decay = jnp.exp(segsum(jnp.pad(a_cs[..., -1], ((0, 0), (0, 0), (1, 0)))))
    new_states = jnp.einsum('bhzc,bchpn->bzhpn', chunk_decay, states)
    states, h_final = new_states[:, :-1], new_states[:, -1]
    y_off = jnp.einsum('bclhn,bchpn,bhcl->bclhp', C, states, jnp.exp(a_cs))
    return (y_diag + y_off).reshape(b, l, h, p), h_final


def ssd_mixer(cols, conv_prev, h0, conv_w, conv_b, dt_bias, a_log, d_skip, norm_w):
    b_, l = cols.shape[:2]
    G = GROUP_WIDTH
    GN = SSD_GROUPS * SSD_STATE
    z = cols[..., :G]
    xbc, conv_new = causal_conv(cols[..., G:G + SSD_CONV_DIM], conv_prev, conv_w, conv_b)
    xbc = jax.nn.silu(xbc)
    dt = jax.nn.softplus(cols[..., G + SSD_CONV_DIM:].astype(F32) + dt_bias)
    xs = xbc[..., :G].reshape(b_, l, SSD_HEADS, SSD_HEADDIM).astype(F32)
    Bm = xbc[..., G:G + GN].reshape(b_, l, SSD_GROUPS, SSD_STATE).astype(F32)
    Cm = xbc[..., G + GN:].reshape(b_, l, SSD_GROUPS, SSD_STATE).astype(F32)
    A = -jnp.exp(a_log.astype(F32))
    y, h_new = ssd_scan(xs, dt, A, Bm, Cm, h0.astype(F32))
    y = (y + xs * d_skip[:, None]).reshape(b_, l, G)
    y = rms_norm(y * jax.nn.silu(z.astype(F32)), norm_w)
    return y.astype(cols.dtype), conv_new, h_new


def moba_attention(q, k, v, q_pos):
    b, nq, h, d = q.shape
    T = k.shape[1]
    nb = -(-T // MOBA_BLOCK)
    pad = nb * MOBA_BLOCK - T
    kb = jnp.pad(k, ((0, 0), (0, pad), (0, 0), (0, 0))).reshape(b, nb, MOBA_BLOCK, h, d).transpose(0, 3, 1, 2, 4)
    vb = jnp.pad(v, ((0, 0), (0, pad), (0, 0), (0, 0))).reshape(b, nb, MOBA_BLOCK, h, d).transpose(0, 3, 1, 2, 4)
    k_mean = jnp.mean(kb.astype(F32), axis=3)
    qh = q.transpose(0, 2, 1, 3)
    q_blk = q_pos // MOBA_BLOCK
    gate = jnp.einsum('bhqd,bhnd->bhqn', qh.astype(F32), k_mean)
    past = jnp.arange(nb)[None, :] < q_blk[:, None]
    gate = jnp.where(past, gate, NEG)
    _, top_i = lax.top_k(gate, min(MOBA_TOPK, nb))
    top_ok = top_i < q_blk[None, None, :, None]
    own = jnp.broadcast_to(q_blk[None, None, :, None], (b, h, nq, 1)).astype(top_i.dtype)
    blk_idx = jnp.concatenate([top_i, own], -1)
    blk_ok = jnp.concatenate([top_ok, jnp.ones(own.shape, bool)], -1)
    qc = MOBA_Q_CHUNK if nq % MOBA_Q_CHUNK == 0 else nq
    nc = nq // qc

    def split(a):
        return jnp.moveaxis(a.reshape(a.shape[:2] + (nc, qc) + a.shape[3:]), 2, 0)

    gather = jax.vmap(jax.vmap(lambda blocks, idx: blocks[idx]))
    scale = d ** -0.5

    def attend(args):
        qq, idx, ok, pp = args
        kg = gather(kb, idx)
        vg = gather(vb, idx)
        s = jnp.einsum('bhqd,bhqkjd->bhqkj', qq.astype(kg.dtype), kg, preferred_element_type=F32) * scale
        kpos = idx[..., None] * MOBA_BLOCK + jnp.arange(MOBA_BLOCK)
        mask = ok[..., None] & (kpos <= pp[None, None, :, None, None])
        s = jnp.where(mask, s, NEG)
        p = jax.nn.softmax(s.reshape(s.shape[:3] + (-1,)), -1).reshape(s.shape)
        return jnp.einsum('bhqkj,bhqkjd->bhqd', p.astype(vg.dtype), vg, preferred_element_type=F32)

    out = lax.map(attend, (split(qh), split(blk_idx), split(blk_ok), q_pos.reshape(nc, qc)))
    return jnp.moveaxis(out, 0, 2).reshape(b, h, nq, d).transpose(0, 2, 1, 3)


def moba_mixer(cols, pos, kv_past, qn_w, kn_w, out_norm_w):
    b_, l = cols.shape[:2]
    G = GROUP_WIDTH
    shp = (b_, l, MOBA_HEADS, MOBA_HEAD_DIM)
    q = rope(rms_norm(cols[..., :G].reshape(shp), qn_w), pos)
    k = rope(rms_norm(cols[..., G:2 * G].reshape(shp), kn_w), pos)
    v = cols[..., 2 * G:].reshape(shp)
    if kv_past is None:
        kf, vf = k, v
    else:
        kf = jnp.concatenate([kv_past[0].astype(k.dtype), k], 1)
        vf = jnp.concatenate([kv_past[1].astype(v.dtype), v], 1)
    o = moba_attention(q, kf, vf, pos).astype(cols.dtype).reshape(b_, l, G)
    return rms_norm(o, out_norm_w), k, v


def gmlp_mixer(cols, v_norm_w, w_s, b_s, out_norm_w):
    b_, l = cols.shape[:2]
    G = GROUP_WIDTH
    u = jax.nn.gelu(cols[..., :G])
    v = rms_norm(jax.nn.gelu(cols[..., G:]), v_norm_w)
    q = min(l, GMLP_CHUNK)
    c = l // q
    ws = jnp.where(jnp.tril(jnp.ones((q, q), bool)), w_s[:, :q, :q], 0.0)
    vh = v.reshape(b_, c, q, GMLP_HEADS, GMLP_HEAD_DIM)
    mixed = jnp.einsum('hts,bcshe->bcthe', ws, vh) + b_s[:, :q].T[None, None, :, :, None]
    y = u * mixed.reshape(b_, l, G)
    return rms_norm(y, out_norm_w), v


def rwkv_mixer(cols, shift_prev, s0, mu, w0, w_up, a0, a_up, g_up, k_k, k_a, r_k, ln_w, ln_b):
    b_, l = cols.shape[:2]
    G = GROUP_WIDTH
    prev = jnp.concatenate([shift_prev[:, None].astype(cols.dtype), cols[:, :-1]], 1)
    xs = cols + (prev - cols) * mu
    o = 3 * G
    r, k, v = xs[..., :G], xs[..., G:2 * G], xs[..., 2 * G:o]
    wl = xs[..., o:o + RWKV_DECAY_RANK]
    al = xs[..., o + RWKV_DECAY_RANK:o + RWKV_DECAY_RANK + RWKV_ICLR_RANK]
    gl = xs[..., o + RWKV_DECAY_RANK + RWKV_ICLR_RANK:]
    w_log = -RWKV_DECAY_SCALE * jax.nn.sigmoid((w0 + jnp.tanh(wl) @ w_up).astype(F32))
    a = jax.nn.sigmoid((a0 + al @ a_up).astype(F32))
    g = jax.nn.sigmoid(gl) @ g_up
    hd = lambda t: t.astype(F32).reshape(b_, l, RWKV_HEADS, RWKV_HEAD)
    r, k, v, a, decay = hd(r), hd(k), hd(v), hd(a), jnp.exp(hd(w_log))
    kk = k * k_k.reshape(RWKV_HEADS, RWKV_HEAD)
    kk = kk * lax.rsqrt(jnp.maximum(jnp.sum(kk * kk, -1, keepdims=True), 1e-12))
    k = k * (1 + (a - 1) * k_a.reshape(RWKV_HEADS, RWKV_HEAD))

    def step(S, inp):
        r_t, k_t, v_t, w_t, kk_t, a_t = inp
        sa = jnp.einsum('bhij,bhj->bhi', S, -kk_t)
        S = S * w_t[:, :, None, :] + sa[..., None] * (kk_t * a_t)[:, :, None, :] + v_t[..., None] * k_t[:, :, None, :]
        return S, jnp.einsum('bhij,bhj->bhi', S, r_t)

    tm = lambda t: jnp.moveaxis(t, 1, 0)
    s_new, y = lax.scan(step, s0.astype(F32), (tm(r), tm(k), tm(v), tm(decay), tm(kk), tm(a)))
    y = jnp.moveaxis(y, 0, 1)
    mean = jnp.mean(y, -1, keepdims=True)
    var = jnp.mean(jnp.square(y - mean), -1, keepdims=True)
    y = ((y - mean) * lax.rsqrt(var + RWKV_LN_EPS)).reshape(b_, l, G) * ln_w + ln_b
    bonus = jnp.sum(r * k * r_k, -1, keepdims=True) * v
    y = (y + bonus.reshape(b_, l, G)) * g
    return y.astype(cols.dtype), s_new, cols[:, -1]


def token_mixers(h, pos, kv_past, conv_prev, ssd_prev, rwkv_prev, shift_prev, P, i):
    cols = h @ P['w_in'][i]
    o1 = SSD_COLS
    o2 = o1 + MOBA_COLS
    o3 = o2 + GMLP_COLS
    y_ssd, conv_new, ssd_new = ssd_mixer(cols[..., :o1], conv_prev, ssd_prev, P['ssd_conv_w'][i], P['ssd_conv_b'][i],
                                         P['ssd_dt_bias'][i], P['ssd_a_log'][i], P['ssd_d'][i], P['ssd_norm_w'][i])
    y_att, k_new, v_new = moba_mixer(cols[..., o1:o2], pos, kv_past, P['moba_q_norm_w'][i], P['moba_k_norm_w'][i],
                                     P['moba_out_norm_w'][i])
    y_gm, v_gm = gmlp_mixer(cols[..., o2:o3], P['gmlp_v_norm_w'][i], P['gmlp_w_s'][i], P['gmlp_b_s'][i],
                            P['gmlp_out_norm_w'][i])
    y_rw, rwkv_new, shift_new = rwkv_mixer(cols[..., o3:], shift_prev, rwkv_prev, P['rwkv_mu'][i], P['rwkv_w0'][i],
                                           P['rwkv_w_up'][i], P['rwkv_a0'][i], P['rwkv_a_up'][i], P['rwkv_g_up'][i],
                                           P['rwkv_k_k'][i], P['rwkv_k_a'][i], P['rwkv_r_k'][i], P['rwkv_ln_w'][i],
                                           P['rwkv_ln_b'][i])
    y = jnp.concatenate([y_ssd, y_att, y_gm, y_rw], -1) @ P['w_out'][i]
    return y, (k_new, v_new, ssd_new, conv_new, rwkv_new, shift_new, v_gm)


def run_trunk(x, c, pos, paged, conv0, ssd0, rwkv0, shift0, P):
    b = x.shape[0]
    per_layer = []
    for i in range(DEPTH):
        mod = (jax.nn.silu(c) @ P['w_ada'][i] + P['b_ada'][i]).reshape(b, N_MOD, 1, D_MODEL)
        h = modulate(x, P['norm_w'][i, 0], mod[:, 0], mod[:, 1])
        x = x + FFN_RES * mod[:, 2] * swiglu(h, P['ffn_w1'][i, 0], P['ffn_w3'][i, 0], P['ffn_w2'][i, 0])
        h = modulate(x, P['norm_w'][i, 1], mod[:, 3], mod[:, 4])
        if paged is None:
            kv_past = None
        else:
            ck, cv, pt = paged
            kv_past = (ck[i][pt].reshape(b, -1, MOBA_HEADS, MOBA_HEAD_DIM),
                       cv[i][pt].reshape(b, -1, MOBA_HEADS, MOBA_HEAD_DIM))
        y, st = token_mixers(h, pos, kv_past, conv0[i], ssd0[i], rwkv0[i], shift0[i], P, i)
        x = x + mod[:, 5] * y
        h = modulate(x, P['norm_w'][i, 2], mod[:, 6], mod[:, 7])
        x = x + FFN_RES * mod[:, 8] * swiglu(h, P['ffn_w1'][i, 1], P['ffn_w3'][i, 1], P['ffn_w2'][i, 1])
        per_layer.append(st)
    stacked = tuple(jnp.stack(s) for s in zip(*per_layer))
    return x, stacked


def setup_inputs(seed: int = 0) -> dict:
    key = jax.random.key(seed)
    ks = list(jax.random.split(key, 64))
    nk = lambda: ks.pop()
    normal = lambda shape, s: jax.random.normal(nk(), shape, F32) * s
    gain = lambda shape: 1.0 + 0.02 * jax.random.normal(nk(), shape, F32)
    n_pages = PAST_LEN // PAGE_SIZE
    n_used = DEC_BATCH * n_pages
    n_pool = (5 * n_used + 3) // 4
    x_prompt = normal((BATCH, SEQ, D_MODEL), 1.0)
    x_sample = normal((DEC_BATCH, DEC_SEQ, D_MODEL), 1.0)
    c_prompt = normal((BATCH, D_MODEL), 1.0)
    c_sample = normal((DEC_BATCH, D_MODEL), 1.0)
    cache_k = normal((DEPTH, n_pool, PAGE_SIZE, MOBA_HEADS, MOBA_HEAD_DIM), 1.0)
    cache_v = normal((DEPTH, n_pool, PAGE_SIZE, MOBA_HEADS, MOBA_HEAD_DIM), 1.0)
    page_table = jax.random.permutation(nk(), n_pool)[:n_used].reshape(DEC_BATCH, n_pages).astype(jnp.int32)
    state_ssd = normal((DEPTH, DEC_BATCH, SSD_HEADS, SSD_HEADDIM, SSD_STATE), 0.1)
    state_ssd_conv = normal((DEPTH, DEC_BATCH, SSD_CONV - 1, SSD_CONV_DIM), 1.0)
    state_rwkv = normal((DEPTH, DEC_BATCH, RWKV_HEADS, RWKV_HEAD, RWKV_HEAD), 0.1)
    state_rwkv_shift = normal((DEPTH, DEC_BATCH, RWKV_COLS), 1.0)
    dt = jnp.exp(jax.random.uniform(nk(), (DEPTH, SSD_HEADS), F32) * (math.log(0.1) - math.log(0.001)) + math.log(0.001))
    return {
        'x_prompt': x_prompt, 'x_sample': x_sample, 'c_prompt': c_prompt, 'c_sample': c_sample,
        'cache_k': cache_k, 'cache_v': cache_v, 'page_table': page_table,
        'state_ssd': state_ssd, 'state_ssd_conv': state_ssd_conv,
        'state_rwkv': state_rwkv, 'state_rwkv_shift': state_rwkv_shift,
        'norm_w': gain((DEPTH, 3, D_MODEL)),
        'w_ada': normal((DEPTH, D_MODEL, N_MOD * D_MODEL), 0.5 * D_MODEL ** -0.5),
        'b_ada': normal((DEPTH, N_MOD * D_MODEL), 0.02),
        'ffn_w1': normal((DEPTH, 2, D_MODEL, D_FF), D_MODEL ** -0.5),
        'ffn_w3': normal((DEPTH, 2, D_MODEL, D_FF), D_MODEL ** -0.5),
        'ffn_w2': normal((DEPTH, 2, D_FF, D_MODEL), D_FF ** -0.5),
        'w_in': normal((DEPTH, D_MODEL, IN_COLS), D_MODEL ** -0.5),
        'w_out': normal((DEPTH, MIX_WIDTH, D_MODEL), MIX_WIDTH ** -0.5),
        'ssd_conv_w': normal((DEPTH, SSD_CONV_DIM, SSD_CONV), SSD_CONV ** -0.5),
        'ssd_conv_b': normal((DEPTH, SSD_CONV_DIM), 0.02),
        'ssd_dt_bias': dt + jnp.log(-jnp.expm1(-dt)),
        'ssd_a_log': jnp.log(jax.random.uniform(nk(), (DEPTH, SSD_HEADS), F32, minval=1.0, maxval=16.0)),
        'ssd_d': gain((DEPTH, SSD_HEADS)),
        'ssd_norm_w': gain((DEPTH, GROUP_WIDTH)),
        'moba_q_norm_w': gain((DEPTH, MOBA_HEAD_DIM)),
        'moba_k_norm_w': gain((DEPTH, MOBA_HEAD_DIM)),
        'moba_out_norm_w': gain((DEPTH, GROUP_WIDTH)),
        'gmlp_v_norm_w': gain((DEPTH, GROUP_WIDTH)),
        'gmlp_w_s': normal((DEPTH, GMLP_HEADS, GMLP_CHUNK, GMLP_CHUNK), GMLP_CHUNK ** -0.5),
        'gmlp_b_s': gain((DEPTH, GMLP_HEADS, GMLP_CHUNK)),
        'gmlp_out_norm_w': gain((DEPTH, GROUP_WIDTH)),
        'rwkv_mu': jax.random.uniform(nk(), (DEPTH, RWKV_COLS), F32),
        'rwkv_w0': normal((DEPTH, GROUP_WIDTH), 0.5),
        'rwkv_w_up': normal((DEPTH, RWKV_DECAY_RANK, GROUP_WIDTH), 0.1),
        'rwkv_a0': normal((DEPTH, GROUP_WIDTH), 0.1),
        'rwkv_a_up': normal((DEPTH, RWKV_ICLR_RANK, GROUP_WIDTH), 0.5 * RWKV_ICLR_RANK ** -0.5),
        'rwkv_g_up': normal((DEPTH, RWKV_GATE_RANK, GROUP_WIDTH), RWKV_GATE_RANK ** -0.5),
        'rwkv_k_k': 0.85 + 0.02 * jax.random.normal(nk(), (DEPTH, GROUP_WIDTH), F32),
        'rwkv_k_a': gain((DEPTH, GROUP_WIDTH)),
        'rwkv_r_k': normal((DEPTH, RWKV_HEADS, RWKV_HEAD), 0.1),
        'rwkv_ln_w': gain((DEPTH, GROUP_WIDTH)),
        'rwkv_ln_b': normal((DEPTH, GROUP_WIDTH), 0.02),
    }


def reference(x_prompt, x_sample, c_prompt, c_sample, cache_k, cache_v, page_table, state_ssd, state_ssd_conv,
              state_rwkv, state_rwkv_shift, norm_w, w_ada, b_ada, ffn_w1, ffn_w3, ffn_w2, w_in, w_out,
              ssd_conv_w, ssd_conv_b, ssd_dt_bias, ssd_a_log, ssd_d, ssd_norm_w, moba_q_norm_w, moba_k_norm_w,
              moba_out_norm_w, gmlp_v_norm_w, gmlp_w_s, gmlp_b_s, gmlp_out_norm_w, rwkv_mu, rwkv_w0, rwkv_w_up,
              rwkv_a0, rwkv_a_up, rwkv_g_up, rwkv_k_k, rwkv_k_a, rwkv_r_k, rwkv_ln_w, rwkv_ln_b):
    P = dict(norm_w=norm_w, w_ada=w_ada, b_ada=b_ada, ffn_w1=ffn_w1, ffn_w3=ffn_w3, ffn_w2=ffn_w2, w_in=w_in,
             w_out=w_out, ssd_conv_w=ssd_conv_w, ssd_conv_b=ssd_conv_b, ssd_dt_bias=ssd_dt_bias, ssd_a_log=ssd_a_log,
             ssd_d=ssd_d, ssd_norm_w=ssd_norm_w, moba_q_norm_w=moba_q_norm_w, moba_k_norm_w=moba_k_norm_w,
             moba_out_norm_w=moba_out_norm_w, gmlp_v_norm_w=gmlp_v_norm_w, gmlp_w_s=gmlp_w_s, gmlp_b_s=gmlp_b_s,
             gmlp_out_norm_w=gmlp_out_norm_w, rwkv_mu=rwkv_mu, rwkv_w0=rwkv_w0, rwkv_w_up=rwkv_w_up,
             rwkv_a0=rwkv_a0, rwkv_a_up=rwkv_a_up, rwkv_g_up=rwkv_g_up, rwkv_k_k=rwkv_k_k, rwkv_k_a=rwkv_k_a,
             rwkv_r_k=rwkv_r_k, rwkv_ln_w=rwkv_ln_w, rwkv_ln_b=rwkv_ln_b)
    n_p = x_prompt.shape[0]
    pos_p = jnp.arange(x_prompt.shape[1], dtype=jnp.int32)
    pos_s = PAST_LEN + jnp.arange(x_sample.shape[1], dtype=jnp.int32)
    conv0 = jnp.zeros((DEPTH, n_p, SSD_CONV - 1, SSD_CONV_DIM), x_prompt.dtype)
    ssd0 = jnp.zeros((DEPTH, n_p, SSD_HEADS, SSD_HEADDIM, SSD_STATE), F32)
    rwkv0 = jnp.zeros((DEPTH, n_p, RWKV_HEADS, RWKV_HEAD, RWKV_HEAD), F32)
    shift0 = jnp.zeros((DEPTH, n_p, RWKV_COLS), x_prompt.dtype)
    y_prompt, (k_p, v_p, ssd_p, conv_p, rwkv_p, shift_p, _) = run_trunk(
        x_prompt, c_prompt, pos_p, None, conv0, ssd0, rwkv0, shift0, P)
    y_sample, (k_s, v_s, ssd_s, conv_s, rwkv_s, shift_s, gmlp_v_s) = run_trunk(
        x_sample, c_sample, pos_s, (cache_k, cache_v, page_table), state_ssd_conv, state_ssd, state_rwkv,
        state_rwkv_shift, P)
    return (y_prompt, y_sample, k_p, v_p, k_s, v_s, ssd_p, ssd_s, conv_p, conv_s, rwkv_p, rwkv_s, shift_p, shift_s, gmlp_v_s)
```

```python
import functools
import math

import jax
import jax.numpy as jnp
from jax import lax
from jax.experimental import pallas as pl
from jax.experimental.pallas import tpu as pltpu

F32 = jnp.float32
BF16 = jnp.bfloat16
HI = lax.Precision.HIGHEST

D_MODEL = 2048
DEPTH = 2
PAST_LEN = 16384
PAGE_SIZE = 128
GROUP_WIDTH = 512
SSD_HEADS = 8
SSD_HEADDIM = 64
SSD_STATE = 128
SSD_CONV = 4
SSD_CHUNK = 128
SSD_CONV_DIM = 1024
MOBA_HEADS = 4
MOBA_HEAD_DIM = 128
MOBA_BLOCK = 256
MOBA_TOPK = 3
ROPE_THETA = 10000.0
GMLP_CHUNK = 128
GMLP_HEADS = 4
RWKV_HEADS = 8
RWKV_HEAD = 64
RWKV_COLS = 1792
RWKV_CHUNK = 64
RWKV_SUB = 16
RWKV_DECAY_SCALE = 0.606531
RWKV_LN_EPS = 64e-5
D_FF = 5632
N_MOD = 9
FFN_RES = 0.5
EPS = 1e-6
NEG = -1e30

COLS = 6144
C_Z, C_X, C_BC = 0, 512, 1024
C_MQ, C_MK, C_MV = 1536, 2048, 2560
C_GU, C_GV = 3072, 3584
C_RR, C_RK, C_RV, C_RL = 4096, 4608, 5120, 5632
C_DT = 5888

LANE = 128
SAMPLE_PAD = 128
VMEM_LIMIT = 56 * 1024 * 1024


def _params(n_axes, vmem=VMEM_LIMIT):
    return pltpu.CompilerParams(dimension_semantics=("arbitrary",) * n_axes, vmem_limit_bytes=vmem)


def _dot(a, b, precision=None):
    return jnp.dot(a, b, preferred_element_type=F32, precision=precision)


def _dot_t(a, b, precision=None):
    return lax.dot_general(a, b, (((1,), (1,)), ((), ())), preferred_element_type=F32, precision=precision)


def _dot_0(a, b, precision=None):
    return lax.dot_general(a, b, (((0,), (0,)), ((), ())), preferred_element_type=F32, precision=precision)


def _iota(shape, dim):
    return lax.broadcasted_iota(jnp.int32, shape, dim)


def _rms(x, w):
    return x * lax.rsqrt(jnp.mean(x * x, -1, keepdims=True) + EPS) * w


def _silu(x):
    return x * jax.nn.sigmoid(x)


def _ada_kernel(c_ref, w_ref, b_ref, o_ref):
    s = _silu(c_ref[...]).astype(BF16)
    o_ref[...] = _dot(s, w_ref[...].astype(BF16)) + b_ref[...]


def _ada(c_all, w_ada, b_ada3, layer):
    rows = c_all.shape[0]
    n_out = w_ada.shape[-1]
    tn = 1024
    return pl.pallas_call(
        _ada_kernel,
        grid=(n_out // tn,),
        in_specs=[
            pl.BlockSpec((rows, D_MODEL), lambda n: (0, 0)),
            pl.BlockSpec((None, D_MODEL, tn), lambda n: (layer, 0, n)),
            pl.BlockSpec((None, 1, tn), lambda n: (layer, 0, n)),
        ],
        out_specs=pl.BlockSpec((rows, tn), lambda n: (0, n)),
        out_shape=jax.ShapeDtypeStruct((rows, n_out), F32),
        compiler_params=_params(1),
        name="ada",
    )(c_all, w_ada, b_ada3)


class _Mod:
    def __init__(self, arr, per_row, rows_per_batch, tm):
        self.arr = arr
        self.per_row = per_row
        self.tiles_per_batch = None if per_row else rows_per_batch // tm
        self.tm = tm

    def spec(self, j):
        if self.per_row:
            return pl.BlockSpec((None, self.tm, D_MODEL), lambda r, *_: (j, r, 0))
        tpb = self.tiles_per_batch
        return pl.BlockSpec((None, None, 1, D_MODEL), lambda r, *_: (r // tpb, j, 0, 0))


def _ffn_kernel(x_ref, nw_ref, sh_ref, sc_ref, g_ref, w1_ref, w3_ref, w2_ref, o_ref, h_ref, acc_ref, *, nf):
    f = pl.program_id(1)

    @pl.when(f == 0)
    def _():
        xn = _rms(x_ref[...], nw_ref[...])
        h_ref[...] = (xn * (1 + sc_ref[...]) + sh_ref[...]).astype(BF16)
        acc_ref[...] = jnp.zeros_like(acc_ref)

    h = h_ref[...]
    a = _dot(h, w1_ref[...])
    b = _dot(h, w3_ref[...])
    acc_ref[...] += _dot((_silu(a) * b).astype(BF16), w2_ref[...])

    @pl.when(f == nf - 1)
    def _():
        o_ref[...] = x_ref[...] + FFN_RES * g_ref[...] * acc_ref[...]


def _ffn(x, mod, j0, nw4, w1, w3, w2, layer, slot, tm, tf):
    rows = x.shape[0]
    nf = D_FF // tf
    return pl.pallas_call(
        functools.partial(_ffn_kernel, nf=nf),
        grid=(rows // tm, nf),
        in_specs=[
            pl.BlockSpec((tm, D_MODEL), lambda r, f: (r, 0)),
            pl.BlockSpec((None, None, 1, D_MODEL), lambda r, f: (layer, 2 * slot, 0, 0)),
            mod.spec(j0), mod.spec(j0 + 1), mod.spec(j0 + 2),
            pl.BlockSpec((None, None, D_MODEL, tf), lambda r, f: (layer, slot, 0, f)),
            pl.BlockSpec((None, None, D_MODEL, tf), lambda r, f: (layer, slot, 0, f)),
            pl.BlockSpec((None, None, tf, D_MODEL), lambda r, f: (layer, slot, f, 0)),
        ],
        out_specs=pl.BlockSpec((tm, D_MODEL), lambda r, f: (r, 0)),
        out_shape=jax.ShapeDtypeStruct((rows, D_MODEL), F32),
        scratch_shapes=[pltpu.VMEM((tm, D_MODEL), BF16), pltpu.VMEM((tm, D_MODEL), F32)],
        compiler_params=_params(2),
        name="ffn",
    )(x, nw4, mod.arr, mod.arr, mod.arr, w1, w3, w2)


def _inproj_kernel(x_ref, nw_ref, sh_ref, sc_ref, w_ref, o_ref, h_ref):
    @pl.when(pl.program_id(1) == 0)
    def _():
        xn = _rms(x_ref[...], nw_ref[...])
        h_ref[...] = (xn * (1 + sc_ref[...]) + sh_ref[...]).astype(BF16)

    o_ref[...] = _dot(h_ref[...], w_ref[...])


def _inproj(x, mod, nw4, w_in, layer, tm, tn):
    rows = x.shape[0]
    return pl.pallas_call(
        _inproj_kernel,
        grid=(rows // tm, COLS // tn),
        in_specs=[
            pl.BlockSpec((tm, D_MODEL), lambda r, n: (r, 0)),
            pl.BlockSpec((None, None, 1, D_MODEL), lambda r, n: (layer, 1, 0, 0)),
            mod.spec(3), mod.spec(4),
            pl.BlockSpec((None, D_MODEL, tn), lambda r, n: (layer, 0, n)),
        ],
        out_specs=pl.BlockSpec((tm, tn), lambda r, n: (r, n)),
        out_shape=jax.ShapeDtypeStruct((rows, COLS), F32),
        scratch_shapes=[pltpu.VMEM((tm, D_MODEL), BF16)],
        compiler_params=_params(2),
        name="inproj",
    )(x, nw4, mod.arr, mod.arr, w_in)


def _outproj_kernel(x_ref, g_ref, y0_ref, y1_ref, y2_ref, y3_ref, w_ref, o_ref):
    G = GROUP_WIDTH
    acc = _dot(y0_ref[...], w_ref[0:G, :])
    acc += _dot(y1_ref[...], w_ref[G:2 * G, :])
    acc += _dot(y2_ref[...], w_ref[2 * G:3 * G, :])
    acc += _dot(y3_ref[...], w_ref[3 * G:4 * G, :])
    o_ref[...] = x_ref[...] + g_ref[...] * acc


def _outproj(x, mod, ys, w_out, layer, tm):
    rows = x.shape[0]
    yspec = pl.BlockSpec((tm, GROUP_WIDTH), lambda r: (r, 0))
    return pl.pallas_call(
        _outproj_kernel,
        grid=(rows // tm,),
        in_specs=[
            pl.BlockSpec((tm, D_MODEL), lambda r: (r, 0)),
            mod.spec(5), yspec, yspec, yspec, yspec,
            pl.BlockSpec((None, D_MODEL, D_MODEL), lambda r: (layer, 0, 0)),
        ],
        out_specs=pl.BlockSpec((tm, D_MODEL), lambda r: (r, 0)),
        out_shape=jax.ShapeDtypeStruct((rows, D_MODEL), F32),
        compiler_params=_params(1),
        name="outproj",
    )(x, mod.arr, *ys, w_out)


def _softplus(x):
    return jnp.maximum(x, 0.0) + jnp.log1p(jnp.exp(-jnp.abs(x)))


def _ssd_kernel(z_ref, x_ref, bc_ref, dt_ref, prev_ref, cw_ref, cb_ref, dtb_ref, alog_ref, dskip_ref, nw_ref,
                h0_ref, y_ref, hout_ref, ext_ref, st_ref, *, C, nc, t_valid):
    c = pl.program_id(1)
    G = GROUP_WIDTH

    @pl.when(c == 0)
    def _():
        ext_ref[0:8, :] = prev_ref[...]
        st_ref[...] = h0_ref[...]

    ext_ref[8:8 + C, 0:G] = x_ref[...]
    ext_ref[8:8 + C, G:2 * G] = bc_ref[...]
    conv = cb_ref[...] + ext_ref[5:5 + C, :] * cw_ref[0:1, :]
    for i in range(1, SSD_CONV):
        conv = conv + ext_ref[5 + i:5 + i + C, :] * cw_ref[i:i + 1, :]
    ext_ref[0:8, :] = ext_ref[C:C + 8, :]
    xbc = _silu(conv)
    xs = xbc[:, 0:G]

    dt = _softplus(dt_ref[...] + dtb_ref[...])
    if t_valid is not None:
        dt = jnp.where(c * C + _iota((C, LANE), 0) < t_valid, dt, 0.0)
    a = dt * (-jnp.exp(alog_ref[...]))
    tri = _iota((C, C), 0) >= _iota((C, C), 1)
    a_cs = _dot(tri.astype(F32), a, HI)
    a_cs_t = a_cs.T
    lane_lo = _iota((C, LANE), 1) < SSD_HEADDIM
    row_lo = _iota((LANE, 1), 0) < SSD_HEADDIM

    ys = []
    for p in range(SSD_HEADS // 2):
        g = p // 2
        h0, h1 = 2 * p, 2 * p + 1
        bm = xbc[:, G + g * SSD_STATE:G + (g + 1) * SSD_STATE]
        cm = xbc[:, G + 2 * SSD_STATE + g * SSD_STATE:G + 2 * SSD_STATE + (g + 1) * SSD_STATE]
        col0, col1 = a_cs[:, h0:h0 + 1], a_cs[:, h1:h1 + 1]
        row0, row1 = a_cs_t[h0:h0 + 1, :], a_cs_t[h1:h1 + 1, :]
        scores = _dot_t(cm, bm, HI)
        m0 = scores * jnp.exp(jnp.where(tri, col0 - row0, -jnp.inf))
        m1 = scores * jnp.exp(jnp.where(tri, col1 - row1, -jnp.inf))
        xs_p = xs[:, p * LANE:(p + 1) * LANE]
        xdt = xs_p * jnp.where(lane_lo, dt[:, h0:h0 + 1], dt[:, h1:h1 + 1])
        y_diag = jnp.where(lane_lo, _dot(m0, xdt, HI), _dot(m1, xdt, HI))
        st = st_ref[p * LANE:(p + 1) * LANE, :]
        y_off = _dot_t(cm, st, HI) * jnp.where(lane_lo, jnp.exp(col0), jnp.exp(col1))
        last0, last1 = a_cs[C - 1:C, h0:h0 + 1], a_cs[C - 1:C, h1:h1 + 1]
        decay = jnp.where(lane_lo, jnp.exp(last0 - col0), jnp.exp(last1 - col1))
        new = _dot_0(xdt * decay, bm, HI)
        st_ref[p * LANE:(p + 1) * LANE, :] = st * jnp.where(row_lo, jnp.exp(last0), jnp.exp(last1)) + new
        ys.append(y_diag + y_off + xs_p * dskip_ref[:, p * LANE:(p + 1) * LANE])

    y = jnp.concatenate(ys, axis=1) * _silu(z_ref[...])
    y_ref[...] = _rms(y, nw_ref[...]).astype(y_ref.dtype)

    @pl.when(c == nc - 1)
    def _():
        hout_ref[...] = st_ref[...]


def _ssd(cols3, prev8, h0, cw8, cb, dtb, alog, dskip, nw, t_valid):
    B, T, _ = cols3.shape
    C = SSD_CHUNK
    nc = T // C
    G = GROUP_WIDTH
    vec = lambda n: pl.BlockSpec((1, n), lambda b, c: (0, 0))
    return pl.pallas_call(
        functools.partial(_ssd_kernel, C=C, nc=nc, t_valid=t_valid),
        grid=(B, nc),
        in_specs=[
            pl.BlockSpec((None, C, G), lambda b, c: (b, c, C_Z // G)),
            pl.BlockSpec((None, C, G), lambda b, c: (b, c, C_X // G)),
            pl.BlockSpec((None, C, G), lambda b, c: (b, c, C_BC // G)),
            pl.BlockSpec((None, C, LANE), lambda b, c: (b, c, C_DT // LANE)),
            pl.BlockSpec((None, 8, SSD_CONV_DIM), lambda b, c: (b, 0, 0)),
            pl.BlockSpec((8, SSD_CONV_DIM), lambda b, c: (0, 0)),
            vec(SSD_CONV_DIM), vec(LANE), vec(LANE), vec(G), vec(G),
            pl.BlockSpec((None, SSD_HEADS * SSD_HEADDIM, SSD_STATE), lambda b, c: (b, 0, 0)),
        ],
        out_specs=[
            pl.BlockSpec((None, C, G), lambda b, c: (b, c, 0)),
            pl.BlockSpec((None, SSD_HEADS * SSD_HEADDIM, SSD_STATE), lambda b, c: (b, 0, 0)),
        ],
        out_shape=[
            jax.ShapeDtypeStruct((B, T, G), BF16),
            jax.ShapeDtypeStruct((B, SSD_HEADS * SSD_HEADDIM, SSD_STATE), F32),
        ],
        scratch_shapes=[pltpu.VMEM((C + 8, SSD_CONV_DIM), F32), pltpu.VMEM((SSD_HEADS * SSD_HEADDIM, SSD_STATE), F32)],
        compiler_params=_params(2),
        name="ssd",
    )(cols3, cols3, cols3, cols3, prev8, cw8, cb, dtb, alog, dskip, nw, h0)


def _gmlp_kernel(u_ref, v_ref, vw_ref, ws_ref, bs_ref, ow_ref, y_ref, vout_ref, *, C):
    u = jax.nn.gelu(u_ref[...])
    v = _rms(jax.nn.gelu(v_ref[...]), vw_ref[...])
    vout_ref[...] = v
    tri = _iota((C, C), 0) >= _iota((C, C), 1)
    mixed = []
    for h in range(GMLP_HEADS):
        ws = jnp.where(tri, ws_ref[h], 0.0)
        mixed.append(_dot(ws, v[:, h * LANE:(h + 1) * LANE], HI) + bs_ref[h])
    y = u * jnp.concatenate(mixed, axis=1)
    y_ref[...] = _rms(y, ow_ref[...]).astype(y_ref.dtype)


def _gmlp(cols3, vw, ws, bs, ow):
    B, T, _ = cols3.shape
    C = GMLP_CHUNK
    G = GROUP_WIDTH
    vec = pl.BlockSpec((1, G), lambda b, c: (0, 0))
    return pl.pallas_call(
        functools.partial(_gmlp_kernel, C=C),
        grid=(B, T // C),
        in_specs=[
            pl.BlockSpec((None, C, G), lambda b, c: (b, c, C_GU // G)),
            pl.BlockSpec((None, C, G), lambda b, c: (b, c, C_GV // G)),
            vec,
            pl.BlockSpec((GMLP_HEADS, C, C), lambda b, c: (0, 0, 0)),
            pl.BlockSpec((GMLP_HEADS, C, 1), lambda b, c: (0, 0, 0)),
            vec,
        ],
        out_specs=[pl.BlockSpec((None, C, G), lambda b, c: (b, c, 0))] * 2,
        out_shape=[jax.ShapeDtypeStruct((B, T, G), BF16), jax.ShapeDtypeStruct((B, T, G), F32)],
        compiler_params=_params(2),
        name="gmlp",
    )(cols3, cols3, vw, ws, bs, ow)


def _block_diag(x, same_head):
    return jnp.where(same_head, jnp.concatenate([x, x], axis=0), 0.0)


def _rwkv_kernel(r_ref, k_ref, v_ref, lo_ref, prev_ref, mu_ref, w0_ref, wup_ref, a0_ref, aup_ref, gup_ref,
                 kk_ref, ka_ref, rk_ref, lnw_ref, lnb_ref, s0_ref, y_ref, sout_ref, ext_ref, st_ref,
                 *, C, nc, t_valid):
    c = pl.program_id(1)
    G = GROUP_WIDTH
    N = RWKV_HEAD
    P = 2 * N

    @pl.when(c == 0)
    def _():
        ext_ref[0:8, :] = prev_ref[...]
        st_ref[...] = s0_ref[...]

    ext_ref[8:8 + C, 0:G] = r_ref[...]
    ext_ref[8:8 + C, G:2 * G] = k_ref[...]
    ext_ref[8:8 + C, 2 * G:3 * G] = v_ref[...]
    ext_ref[8:8 + C, 3 * G:RWKV_COLS] = lo_ref[...]
    cur = ext_ref[8:8 + C, :]
    prev = ext_ref[7:7 + C, :]
    xs = cur + (prev - cur) * mu_ref[...]
    ext_ref[0:8, :] = ext_ref[C:C + 8, :]

    r, k, v = xs[:, 0:G], xs[:, G:2 * G], xs[:, 2 * G:3 * G]
    la = xs[:, 3 * G:3 * G + P]
    gl = xs[:, 3 * G + P:RWKV_COLS]
    w_log = -RWKV_DECAY_SCALE * jax.nn.sigmoid(w0_ref[...] + _dot(jnp.tanh(la), wup_ref[...], HI))
    a = jax.nn.sigmoid(a0_ref[...] + _dot(la, aup_ref[...], HI))
    g = _dot(jax.nn.sigmoid(gl), gup_ref[...], HI)

    ones_bd = ((_iota((P, P), 0) < N) == (_iota((P, P), 1) < N)).astype(F32)

    def head_sum(x):
        return jnp.concatenate([_dot(x[:, p * P:(p + 1) * P], ones_bd, HI) for p in range(G // P)], axis=1)

    kk = k * kk_ref[...]
    kk = kk * lax.rsqrt(jnp.maximum(head_sum(kk * kk), 1e-12))
    k2 = k * (1 + (a - 1) * ka_ref[...])
    if t_valid is not None:
        ok = c * C + _iota((C, G), 0) < t_valid
        w_log = jnp.where(ok, w_log, 0.0)
        kk = jnp.where(ok, kk, 0.0)
        k2 = jnp.where(ok, k2, 0.0)
    b = kk * a

    tri_cc = (_iota((C, C), 0) >= _iota((C, C), 1)).astype(F32)
    cl = _dot(tri_cc, w_log, HI)
    cl_last = cl[C - 1:C, :]
    kkp = kk * jnp.exp(cl - w_log)
    rp = r * jnp.exp(cl)
    einv = jnp.exp(-cl)
    bi, ki = b * einv, k2 * einv
    e_c = jnp.exp(cl_last - cl)
    bt, kt = b * e_c, k2 * e_c
    p_c = jnp.exp(cl_last)

    t_i = _iota((C, P), 0)
    s_i = _iota((C, P), 1) % C
    strict = s_i < t_i
    incl = s_i <= t_i
    diag_blk = (s_i // RWKV_SUB) == (t_i // RWKV_SUB)
    eye = (s_i == t_i).astype(F32)
    same_head = (_iota((P, P), 0) < C) == (_iota((P, P), 1) < C)
    same_head_n = (_iota((P, P), 0) < N) == (_iota((P, P), 1) < N)

    def mm(x, y):
        return _dot(x, _block_diag(y, same_head), HI)

    ys = []
    for p in range(G // P):
        sl = slice(p * P, (p + 1) * P)
        lhs = jnp.concatenate([kkp[:, sl], rp[:, sl]], axis=0)
        ab = _dot_t(lhs, _block_diag(bi[:, sl], same_head_n), HI)
        ak = _dot_t(lhs, _block_diag(ki[:, sl], same_head_n), HI)
        a_m = jnp.where(strict, ab[0:C], 0.0)
        b_k = jnp.where(strict, ak[0:C], 0.0)
        r_b = jnp.where(incl, ab[C:2 * C], 0.0)
        r_k = jnp.where(incl, ak[C:2 * C], 0.0)

        n1 = jnp.where(diag_blk, -a_m, 0.0)
        a_o = jnp.where(diag_blk, 0.0, a_m)
        t_d = eye + n1
        n_pow = n1
        for _ in range(int(math.log2(RWKV_SUB)) - 1):
            n_pow = mm(n_pow, n_pow)
            t_d = t_d + mm(t_d, n_pow)
        m1 = mm(t_d, a_o)
        m2 = mm(m1, m1)
        im = eye - m1
        t_full = mm(im + mm(im, m2), t_d)

        st = st_ref[p]
        v_p = v[:, sl]
        rhs = _dot_t(kkp[:, sl], st, HI) + mm(b_k, v_p)
        u = mm(t_full, rhs)
        y_p = _dot_t(rp[:, sl], st, HI) + mm(r_k, v_p) - mm(r_b, u)
        upd = _dot_0(jnp.concatenate([v_p, -u], axis=0), jnp.concatenate([kt[:, sl], bt[:, sl]], axis=0), HI)
        st_ref[p] = st * p_c[:, sl] + jnp.where(same_head_n, upd, 0.0)
        ys.append(y_p)

    y = jnp.concatenate(ys, axis=1)
    mean = head_sum(y) * (1.0 / N)
    d = y - mean
    var = head_sum(d * d) * (1.0 / N)
    yn = d * lax.rsqrt(var + RWKV_LN_EPS) * lnw_ref[...] + lnb_ref[...]
    bonus = head_sum(r * k2 * rk_ref[...]) * v
    y_ref[...] = ((yn + bonus) * g).astype(y_ref.dtype)

    @pl.when(c == nc - 1)
    def _():
        sout_ref[...] = st_ref[...]


def _rwkv(cols3, prev8, s0_bd, mu, w0, wup, a0, aup, gup, kk, ka, rk, lnw, lnb, t_valid):
    B, T, _ = cols3.shape
    C = RWKV_CHUNK
    nc = T // C
    G = GROUP_WIDTH
    P = 2 * RWKV_HEAD
    n_pairs = RWKV_HEADS // 2
    vec = lambda n: pl.BlockSpec((1, n), lambda b, c: (0, 0))
    mat = lambda m, n: pl.BlockSpec((m, n), lambda b, c: (0, 0))
    return pl.pallas_call(
        functools.partial(_rwkv_kernel, C=C, nc=nc, t_valid=t_valid),
        grid=(B, nc),
        in_specs=[
            pl.BlockSpec((None, C, G), lambda b, c: (b, c, C_RR // G)),
            pl.BlockSpec((None, C, G), lambda b, c: (b, c, C_RK // G)),
            pl.BlockSpec((None, C, G), lambda b, c: (b, c, C_RV // G)),
            pl.BlockSpec((None, C, 2 * P), lambda b, c: (b, c, C_RL // (2 * P))),
            pl.BlockSpec((None, 8, RWKV_COLS), lambda b, c: (b, 0, 0)),
            vec(RWKV_COLS), vec(G), mat(P, G), vec(G), mat(P, G), mat(P, G),
            vec(G), vec(G), vec(G), vec(G), vec(G),
            pl.BlockSpec((None, n_pairs, P, P), lambda b, c: (b, 0, 0, 0)),
        ],
        out_specs=[
            pl.BlockSpec((None, C, G), lambda b, c: (b, c, 0)),
            pl.BlockSpec((None, n_pairs, P, P), lambda b, c: (b, 0, 0, 0)),
        ],
        out_shape=[
            jax.ShapeDtypeStruct((B, T, G), BF16),
            jax.ShapeDtypeStruct((B, n_pairs, P, P), F32),
        ],
        scratch_shapes=[pltpu.VMEM((C + 8, RWKV_COLS), F32), pltpu.VMEM((n_pairs, P, P), F32)],
        compiler_params=_params(2),
        name="rwkv",
    )(cols3, cols3, cols3, cols3, prev8, mu, w0, wup, a0, aup, gup, kk, ka, rk, lnw, lnb, s0_bd)


def _moba_prep_kernel(q_ref, k_ref, cos_ref, sin_ref, qw_ref, kw_ref, qo_ref, ko_ref):
    cos, sin = cos_ref[...], sin_ref[...]
    for h in range(MOBA_HEADS):
        sl = slice(h * LANE, (h + 1) * LANE)
        for src, w_ref, dst in ((q_ref, qw_ref, qo_ref), (k_ref, kw_ref, ko_ref)):
            xn = _rms(src[:, sl], w_ref[...])
            dst[:, sl] = xn * cos + pltpu.roll(xn, MOBA_HEAD_DIM // 2, 1) * sin


def _moba_prep(cols, cos, sin, qw, kw, tm):
    rows = cols.shape[0]
    G = GROUP_WIDTH
    tab = pl.BlockSpec((tm, LANE), lambda r: (r, 0))
    vec = pl.BlockSpec((1, LANE), lambda r: (0, 0))
    return pl.pallas_call(
        _moba_prep_kernel,
        grid=(rows // tm,),
        in_specs=[
            pl.BlockSpec((tm, G), lambda r: (r, C_MQ // G)),
            pl.BlockSpec((tm, G), lambda r: (r, C_MK // G)),
            tab, tab, vec, vec,
        ],
        out_specs=[pl.BlockSpec((tm, G), lambda r: (r, 0))] * 2,
        out_shape=[jax.ShapeDtypeStruct((rows, G), F32)] * 2,
        compiler_params=_params(1),
        name="moba_prep",
    )(cols, cols, cos, sin, qw, kw)


def _top3(gate, lane_f):
    sel = jnp.zeros(gate.shape, jnp.bool_)
    g = gate
    big = float(gate.shape[-1])
    idxs = []
    for _ in range(MOBA_TOPK):
        m = jnp.max(g, axis=-1, keepdims=True)
        idx = jnp.min(jnp.where(g == m, lane_f, big), axis=-1, keepdims=True)
        pick = lane_f == idx
        sel = sel | pick
        g = jnp.where(pick, -jnp.inf, g)
        idxs.append(idx)
    return sel, idxs


def _moba_attn_kernel(q_ref, k_ref, v_ref, ow_ref, y_ref, km_ref, o_ref, *, nb):
    qi = pl.program_id(1)
    BLK = MOBA_BLOCK
    scale = MOBA_HEAD_DIM ** -0.5

    @pl.when(qi == 0)
    def _():
        km_ref[...] = jnp.zeros_like(km_ref)
        for j in range(nb):
            km_ref[j:j + 1, :] = jnp.mean(k_ref[j * BLK:(j + 1) * BLK, :], axis=0, keepdims=True)

    lane_i = _iota((BLK, LANE), 1)
    lane_f = lane_i.astype(F32)
    past = lane_i < qi
    causal = _iota((BLK, BLK), 0) >= _iota((BLK, BLK), 1)
    own = pl.multiple_of(qi * BLK, BLK)

    for h in range(MOBA_HEADS):
        sl = slice(h * LANE, (h + 1) * LANE)
        q = q_ref[:, sl]
        gate = jnp.where(past, _dot_t(q, km_ref[:, sl], HI), NEG)
        sel, _ = _top3(gate, lane_f)
        sel_f = (sel & past).astype(F32)
        qb = q.astype(BF16)

        s = _dot_t(qb, k_ref[pl.ds(own, BLK), sl].astype(BF16)) * scale
        s = jnp.where(causal, s, NEG)
        m0 = jnp.max(s, axis=-1, keepdims=True)
        p0 = jnp.exp(s - m0)
        l0 = jnp.sum(p0, axis=-1, keepdims=True)
        acc0 = _dot(p0.astype(BF16), v_ref[pl.ds(own, BLK), sl].astype(BF16))

        def body(j, carry):
            m_i, l_i, acc = carry
            start = pl.multiple_of(j * BLK, BLK)
            s = _dot_t(qb, k_ref[pl.ds(start, BLK), sl].astype(BF16)) * scale
            hit = jnp.sum(jnp.where(lane_i == j, sel_f, 0.0), axis=-1, keepdims=True) > 0.0
            s = jnp.where(hit, s, NEG)
            m_n = jnp.maximum(m_i, jnp.max(s, axis=-1, keepdims=True))
            alpha = jnp.exp(m_i - m_n)
            p = jnp.exp(s - m_n)
            l_n = alpha * l_i + jnp.sum(p, axis=-1, keepdims=True)
            acc = alpha * acc + _dot(p.astype(BF16), v_ref[pl.ds(start, BLK), sl].astype(BF16))
            return m_n, l_n, acc

        _, l_f, acc_f = lax.fori_loop(0, qi, body, (m0, l0, acc0))
        o_ref[:, sl] = acc_f / l_f

    y_ref[...] = _rms(o_ref[...], ow_ref[...]).astype(y_ref.dtype)


def _moba_attn(q3, k3, cols3, ow):
    B, T, G = q3.shape
    BLK = MOBA_BLOCK
    nb = T // BLK
    return pl.pallas_call(
        functools.partial(_moba_attn_kernel, nb=nb),
        grid=(B, nb),
        in_specs=[
            pl.BlockSpec((None, BLK, G), lambda b, i: (b, i, 0)),
            pl.BlockSpec((None, T, G), lambda b, i: (b, 0, 0)),
            pl.BlockSpec((None, T, G), lambda b, i: (b, 0, C_MV // G)),
            pl.BlockSpec((1, G), lambda b, i: (0, 0)),
        ],
        out_specs=pl.BlockSpec((None, BLK, G), lambda b, i: (b, i, 0)),
        out_shape=jax.ShapeDtypeStruct((B, T, G), BF16),
        scratch_shapes=[pltpu.VMEM((LANE, G), F32), pltpu.VMEM((BLK, G), F32)],
        compiler_params=_params(2),
        name="moba_attn",
    )(q3, k3, cols3, ow)


def _kmean_kernel(pt_ref, p0_ref, p1_ref, o_ref):
    s = jnp.sum(p0_ref[...], axis=0, keepdims=True) + jnp.sum(p1_ref[...], axis=0, keepdims=True)
    o_ref[...] = s * (1.0 / MOBA_BLOCK)


def _kmean_pages(cache_k4, page_table_flat, layer, batch, n_pages):
    G = GROUP_WIDTH
    nbk = n_pages // 2
    page = lambda o: pl.BlockSpec((None, None, PAGE_SIZE, G),
                                  lambda b, j, pt: (layer, pt[b * n_pages + 2 * j + o], 0, 0))
    return pl.pallas_call(
        _kmean_kernel,
        grid_spec=pltpu.PrefetchScalarGridSpec(
            num_scalar_prefetch=1,
            grid=(batch, nbk),
            in_specs=[page(0), page(1)],
            out_specs=pl.BlockSpec((None, None, 1, G), lambda b, j, pt: (b, j, 0, 0)),
        ),
        out_shape=jax.ShapeDtypeStruct((batch, nbk, 1, G), F32),
        compiler_params=_params(2),
        name="moba_kmean_pages",
    )(page_table_flat, cache_k4, cache_k4)


def _select_kernel(q_ref, km_ref, idx_ref, *, batch, t_new, nbk):
    rows = batch * t_new
    lane_i = _iota((rows, LANE), 1)
    lane_f = lane_i.astype(F32)
    row_b = _iota((rows, LANE), 0) // t_new
    out = jnp.zeros((rows, LANE), F32)
    pad = jnp.zeros((LANE - nbk, LANE), F32)
    for h in range(MOBA_HEADS):
        sl = slice(h * LANE, (h + 1) * LANE)
        q = q_ref[:, sl]
        gate = jnp.zeros((rows, LANE), F32)
        for b in range(batch):
            km = jnp.concatenate([km_ref[b, :, sl], pad], axis=0)
            gate = jnp.where(row_b == b, _dot_t(q, km, HI), gate)
        gate = jnp.where(lane_i < nbk, gate, NEG)
        _, idxs = _top3(gate, lane_f)
        for kth, idx in enumerate(idxs):
            out = jnp.where(lane_i == h * MOBA_HEADS + kth, idx, out)
    idx_ref[...] = out.astype(jnp.int32)


def _select(q_rot, kmean, batch, t_new):
    rows = batch * t_new
    nbk = kmean.shape[1]
    G = GROUP_WIDTH
    return pl.pallas_call(
        functools.partial(_select_kernel, batch=batch, t_new=t_new, nbk=nbk),
        grid=(1,),
        in_specs=[
            pl.BlockSpec((rows, G), lambda i: (0, 0)),
            pl.BlockSpec((batch, nbk, G), lambda i: (0, 0, 0)),
        ],
        out_specs=pl.BlockSpec((rows, LANE), lambda i: (0, 0)),
        out_shape=jax.ShapeDtypeStruct((rows, LANE), jnp.int32),
        compiler_params=_params(1),
        name="moba_select",
    )(q_rot, kmean)


def _sample_attn_kernel(pt_ref, idx_ref, q_ref, kn_ref, vn_ref, *refs, t_new):
    n_blk = 2 * MOBA_TOPK
    k_refs, v_refs, o_ref = refs[:n_blk], refs[n_blk:2 * n_blk], refs[2 * n_blk]
    b, t = pl.program_id(0), pl.program_id(1)
    rows = q_ref.shape[0]
    scale = MOBA_HEAD_DIM ** -0.5
    row = b * t_new + t
    q8 = jnp.broadcast_to(q_ref[pl.ds(row, 1), :], (8, LANE)).astype(BF16)
    s_new = _dot_t(q8, kn_ref[...].astype(BF16)) * scale
    r_i = _iota((8, rows), 1)
    s_new = jnp.where((r_i >= b * t_new) & (r_i <= row), s_new, NEG)
    s_old = [_dot_t(q8, kr[...].astype(BF16)) * scale for kr in k_refs]
    m = jnp.max(s_new, axis=-1, keepdims=True)
    for s in s_old:
        m = jnp.maximum(m, jnp.max(s, axis=-1, keepdims=True))
    p_new = jnp.exp(s_new - m)
    l = jnp.sum(p_new, axis=-1, keepdims=True)
    acc = _dot(p_new.astype(BF16), vn_ref[...].astype(BF16))
    for s, vr in zip(s_old, v_refs):
        p = jnp.exp(s - m)
        l = l + jnp.sum(p, axis=-1, keepdims=True)
        acc = acc + _dot(p.astype(BF16), vr[...].astype(BF16))
    o_ref[...] = (acc / l)[0:1, :]


def _sample_attn(q_rot, k_new, cols, cache_k4, cache_v4, page_table_flat, idx_flat, layer, batch, t_new, n_pages):
    rows = batch * t_new
    G = GROUP_WIDTH

    def page(kth, o):
        def index_map(b, t, h, pt, idx):
            blk = idx[(b * t_new + t) * LANE + h * MOBA_HEADS + kth]
            return (layer, pt[b * n_pages + 2 * blk + o], 0, h)
        return pl.BlockSpec((None, None, PAGE_SIZE, LANE), index_map)

    pages = [page(kth, o) for kth in range(MOBA_TOPK) for o in range(2)]
    return pl.pallas_call(
        functools.partial(_sample_attn_kernel, t_new=t_new),
        grid_spec=pltpu.PrefetchScalarGridSpec(
            num_scalar_prefetch=2,
            grid=(batch, t_new, MOBA_HEADS),
            in_specs=[
                pl.BlockSpec((rows, LANE), lambda b, t, h, pt, idx: (0, h)),
                pl.BlockSpec((rows, LANE), lambda b, t, h, pt, idx: (0, h)),
                pl.BlockSpec((rows, LANE), lambda b, t, h, pt, idx: (0, C_MV // LANE + h)),
            ] + pages + pages,
            out_specs=pl.BlockSpec((None, 1, LANE), lambda b, t, h, pt, idx: (b * t_new + t, 0, h)),
        ),
        out_shape=jax.ShapeDtypeStruct((rows, 1, G), F32),
        compiler_params=_params(3),
        name="moba_sample_attn",
    )(page_table_flat, idx_flat, q_rot, k_new, cols, *([cache_k4] * len(pages)), *([cache_v4] * len(pages)))


def _rownorm_kernel(x_ref, w_ref, y_ref):
    y_ref[...] = _rms(x_ref[...], w_ref[...]).astype(y_ref.dtype)


def _rownorm(x, w):
    rows, n = x.shape
    return pl.pallas_call(
        _rownorm_kernel,
        grid=(1,),
        in_specs=[pl.BlockSpec((rows, n), lambda i: (0, 0)), pl.BlockSpec((1, n), lambda i: (0, 0))],
        out_specs=pl.BlockSpec((rows, n), lambda i: (0, 0)),
        out_shape=jax.ShapeDtypeStruct((rows, n), BF16),
        compiler_params=_params(1),
        name="rownorm",
    )(x, w)


def _rope_tables(pos):
    half = MOBA_HEAD_DIM // 2
    freq = ROPE_THETA ** (-jnp.arange(half, dtype=F32) / half)
    ang = pos.astype(F32)[:, None] * freq[None, :]
    cos, sin = jnp.cos(ang), jnp.sin(ang)
    return jnp.concatenate([cos, cos], -1), jnp.concatenate([-sin, sin], -1)


def _pad_rows_front(x, rows):
    return jnp.pad(x, ((0, 0), (rows - x.shape[1], 0), (0, 0)))


def _pair_block_diag(s):
    B, H, N, _ = s.shape
    s = s.reshape(B, H // 2, 2, N, 1, N) * jnp.eye(2, dtype=s.dtype)[None, None, :, None, :, None]
    return s.reshape(B, H // 2, 2 * N, 2 * N)


def _pair_diag_blocks(s_bd):
    B, n_pairs, P, _ = s_bd.shape
    N = P // 2
    s = s_bd.reshape(B, n_pairs, 2, N, 2, N)
    return jnp.stack([s[:, :, 0, :, 0, :], s[:, :, 1, :, 1, :]], axis=2).reshape(B, 2 * n_pairs, N, N)


def _layer_weights(W, i):
    G = GROUP_WIDTH
    row = lambda v: v.reshape(1, -1)
    lane_pad = lambda v: jnp.pad(v, (0, LANE - v.shape[0])).reshape(1, LANE)
    zeros_r = jnp.zeros((RWKV_HEAD, G), F32)
    return dict(
        cw8=jnp.pad(W['ssd_conv_w'][i].T, ((0, 8 - SSD_CONV), (0, 0))),
        cb=row(W['ssd_conv_b'][i]),
        dtb=lane_pad(W['ssd_dt_bias'][i]),
        alog=lane_pad(W['ssd_a_log'][i]),
        dskip=row(jnp.repeat(W['ssd_d'][i], GROUP_WIDTH // SSD_HEADS)),
        ssd_nw=row(W['ssd_norm_w'][i]),
        qw=row(W['moba_q_norm_w'][i]), kw=row(W['moba_k_norm_w'][i]), ow=row(W['moba_out_norm_w'][i]),
        gvw=row(W['gmlp_v_norm_w'][i]), gow=row(W['gmlp_out_norm_w'][i]),
        ws=W['gmlp_w_s'][i], bs=W['gmlp_b_s'][i][:, :, None],
        mu=row(W['rwkv_mu'][i]), w0=row(W['rwkv_w0'][i]), a0=row(W['rwkv_a0'][i]),
        wup=jnp.concatenate([W['rwkv_w_up'][i], zeros_r], 0),
        aup=jnp.concatenate([zeros_r, W['rwkv_a_up'][i]], 0),
        gup=W['rwkv_g_up'][i],
        kk=row(W['rwkv_k_k'][i]), ka=row(W['rwkv_k_a'][i]), rk=row(W['rwkv_r_k'][i]),
        lnw=row(W['rwkv_ln_w'][i]), lnb=row(W['rwkv_ln_b'][i]),
    )


def _mixers(cols, B, T, Tp, lw, conv_prev, ssd_prev, rwkv_prev, shift_prev, t_valid):
    G = GROUP_WIDTH
    cols3 = cols.reshape(B, T, COLS)
    if Tp != T:
        cols3 = jnp.pad(cols3, ((0, 0), (0, Tp - T), (0, 0)))
    y_ssd, ssd_new = _ssd(cols3, _pad_rows_front(conv_prev, 8), ssd_prev.reshape(B, SSD_HEADS * SSD_HEADDIM, SSD_STATE),
                          lw['cw8'], lw['cb'], lw['dtb'], lw['alog'], lw['dskip'], lw['ssd_nw'], t_valid)
    y_gm, v_gm = _gmlp(cols3, lw['gvw'], lw['ws'], lw['bs'], lw['gow'])
    y_rw, rwkv_new = _rwkv(cols3, _pad_rows_front(shift_prev[:, None, :], 8), _pair_block_diag(rwkv_prev),
                           lw['mu'], lw['w0'], lw['wup'], lw['a0'], lw['aup'], lw['gup'], lw['kk'], lw['ka'],
                           lw['rk'], lw['lnw'], lw['lnb'], t_valid)
    crop = lambda y: y[:, :T].reshape(B * T, G)
    raw = cols.reshape(B, T, COLS)
    conv_new = raw[:, T - (SSD_CONV - 1):, C_X:C_X + SSD_CONV_DIM]
    shift_new = raw[:, T - 1, C_RR:C_RR + RWKV_COLS]
    states = (ssd_new.reshape(B, SSD_HEADS, SSD_HEADDIM, SSD_STATE), conv_new, _pair_diag_blocks(rwkv_new), shift_new)
    return crop(y_ssd), crop(y_gm), crop(y_rw), v_gm[:, :T], states


def kernel(x_prompt, x_sample, c_prompt, c_sample, cache_k, cache_v, page_table, state_ssd, state_ssd_conv, state_rwkv, state_rwkv_shift, norm_w, w_ada, b_ada, ffn_w1, ffn_w3, ffn_w2, w_in, w_out, ssd_conv_w, ssd_conv_b, ssd_dt_bias, ssd_a_log, ssd_d, ssd_norm_w, moba_q_norm_w, moba_k_norm_w, moba_out_norm_w, gmlp_v_norm_w, gmlp_w_s, gmlp_b_s, gmlp_out_norm_w, rwkv_mu, rwkv_w0, rwkv_w_up, rwkv_a0, rwkv_a_up, rwkv_g_up, rwkv_k_k, rwkv_k_a, rwkv_r_k, rwkv_ln_w, rwkv_ln_b):
    W = dict(ssd_conv_w=ssd_conv_w, ssd_conv_b=ssd_conv_b, ssd_dt_bias=ssd_dt_bias, ssd_a_log=ssd_a_log,
             ssd_d=ssd_d, ssd_norm_w=ssd_norm_w, moba_q_norm_w=moba_q_norm_w, moba_k_norm_w=moba_k_norm_w,
             moba_out_norm_w=moba_out_norm_w, gmlp_v_norm_w=gmlp_v_norm_w, gmlp_w_s=gmlp_w_s, gmlp_b_s=gmlp_b_s,
             gmlp_out_norm_w=gmlp_out_norm_w, rwkv_mu=rwkv_mu, rwkv_w0=rwkv_w0, rwkv_w_up=rwkv_w_up,
             rwkv_a0=rwkv_a0, rwkv_a_up=rwkv_a_up, rwkv_g_up=rwkv_g_up, rwkv_k_k=rwkv_k_k, rwkv_k_a=rwkv_k_a,
             rwkv_r_k=rwkv_r_k, rwkv_ln_w=rwkv_ln_w, rwkv_ln_b=rwkv_ln_b)
    Bp, Tq, D = x_prompt.shape
    Bs, Ts, _ = x_sample.shape
    n_pages = page_table.shape[1]
    assert n_pages * PAGE_SIZE == PAST_LEN and PAST_LEN % MOBA_BLOCK == 0
    assert PAST_LEN // MOBA_BLOCK >= MOBA_TOPK and Ts <= MOBA_BLOCK
    Rp, Rs = Bp * Tq, Bs * Ts

    w1_b, w3_b, w2_b = ffn_w1.astype(BF16), ffn_w3.astype(BF16), ffn_w2.astype(BF16)
    dt_lo = GROUP_WIDTH + SSD_CONV_DIM
    dt_hi = dt_lo + SSD_HEADS
    w_in_b = jnp.concatenate(
        [w_in[:, :, :dt_lo], w_in[:, :, dt_hi:], w_in[:, :, dt_lo:dt_hi],
         jnp.zeros((DEPTH, D, COLS - C_DT - SSD_HEADS), w_in.dtype)], axis=-1).astype(BF16)
    w_out_b = w_out.astype(BF16)
    nw4 = norm_w.reshape(DEPTH, 3, 1, D)
    b_ada3 = b_ada.reshape(DEPTH, 1, N_MOD * D)

    n_c = Bp + Bs
    c_all = jnp.pad(jnp.concatenate([c_prompt, c_sample], 0), ((0, -n_c % 8), (0, 0)))

    pos_p = jnp.arange(Tq, dtype=jnp.int32)
    pos_s = PAST_LEN + jnp.arange(Ts, dtype=jnp.int32)
    cos_p, sin_p = (jnp.tile(t, (Bp, 1)) for t in _rope_tables(pos_p))
    cos_s, sin_s = (jnp.tile(t, (Bs, 1)) for t in _rope_tables(pos_s))

    cache_k4 = cache_k.reshape(DEPTH, -1, PAGE_SIZE, GROUP_WIDTH)
    cache_v4 = cache_v.reshape(DEPTH, -1, PAGE_SIZE, GROUP_WIDTH)
    pt_flat = page_table.reshape(-1)

    zeros = lambda *s: jnp.zeros(s, F32)
    xp = x_prompt.reshape(Rp, D)
    xs = x_sample.reshape(Rs, D)
    TM = 512
    outs_p, outs_s = [], []
    for i in range(DEPTH):
        lw = _layer_weights(W, i)
        mod = _ada(c_all, w_ada, b_ada3, i).reshape(-1, N_MOD, D)
        mod_p = _Mod(mod[:Bp].reshape(Bp, N_MOD, 1, D), False, Tq, TM)
        mod_s = _Mod(jnp.repeat(mod[Bp:n_c], Ts, axis=0).transpose(1, 0, 2), True, Ts, Rs)

        xp = _ffn(xp, mod_p, 0, nw4, w1_b, w3_b, w2_b, i, 0, TM, 512)
        cols = _inproj(xp, mod_p, nw4, w_in_b, i, TM, 512)
        q_rot, k_rot = _moba_prep(cols, cos_p, sin_p, lw['qw'], lw['kw'], TM)
        y_ssd, y_gm, y_rw, _, st = _mixers(cols, Bp, Tq, Tq, lw, zeros(Bp, SSD_CONV - 1, SSD_CONV_DIM),
                                           zeros(Bp, SSD_HEADS, SSD_HEADDIM, SSD_STATE),
                                           zeros(Bp, RWKV_HEADS, RWKV_HEAD, RWKV_HEAD), zeros(Bp, RWKV_COLS), None)
        y_att = _moba_attn(q_rot.reshape(Bp, Tq, -1), k_rot.reshape(Bp, Tq, -1), cols.reshape(Bp, Tq, COLS),
                           lw['ow']).reshape(Rp, -1)
        xp = _outproj(xp, mod_p, (y_ssd, y_att, y_gm, y_rw), w_out_b, i, TM)
        xp = _ffn(xp, mod_p, 6, nw4, w1_b, w3_b, w2_b, i, 1, TM, 512)
        shp = (Bp, Tq, MOBA_HEADS, MOBA_HEAD_DIM)
        outs_p.append((k_rot.reshape(shp), cols[:, C_MV:C_MV + GROUP_WIDTH].reshape(shp)) + st)

        xs = _ffn(xs, mod_s, 0, nw4, w1_b, w3_b, w2_b, i, 0, Rs, 512)
        cols = _inproj(xs, mod_s, nw4, w_in_b, i, Rs, 512)
        q_rot, k_rot = _moba_prep(cols, cos_s, sin_s, lw['qw'], lw['kw'], Rs)
        y_ssd, y_gm, y_rw, v_gm, st = _mixers(cols, Bs, Ts, SAMPLE_PAD, lw, state_ssd_conv[i], state_ssd[i],
                                              state_rwkv[i], state_rwkv_shift[i], Ts)
        kmean = _kmean_pages(cache_k4, pt_flat, i, Bs, n_pages).reshape(Bs, n_pages // 2, GROUP_WIDTH)
        idx = _select(q_rot, kmean, Bs, Ts)
        o_att = _sample_attn(q_rot, k_rot, cols, cache_k4, cache_v4, pt_flat, idx.reshape(-1), i, Bs, Ts, n_pages)
        y_att = _rownorm(o_att.reshape(Rs, GROUP_WIDTH), lw['ow'])
        xs = _outproj(xs, mod_s, (y_ssd, y_att, y_gm, y_rw), w_out_b, i, Rs)
        xs = _ffn(xs, mod_s, 6, nw4, w1_b, w3_b, w2_b, i, 1, Rs, 512)
        shp = (Bs, Ts, MOBA_HEADS, MOBA_HEAD_DIM)
        outs_s.append((k_rot.reshape(shp), cols[:, C_MV:C_MV + GROUP_WIDTH].reshape(shp)) + st + (v_gm,))

    k_p, v_p, ssd_p, conv_p, rwkv_p, shift_p = (jnp.stack(s) for s in zip(*outs_p))
    k_s, v_s, ssd_s, conv_s, rwkv_s, shift_s, gmlp_v_s = (jnp.stack(s) for s in zip(*outs_s))
    return (xp.reshape(Bp, Tq, D), xs.reshape(Bs, Ts, D), k_p, v_p, k_s, v_s, ssd_p, ssd_s, conv_p, conv_s,
            rwkv_p, rwkv_s, shift_p, shift_s, gmlp_v_s)
```

```python
import functools
import math

import jax
import jax.numpy as jnp
from jax import lax
from jax.experimental import pallas as pl
from jax.experimental.pallas import tpu as pltpu

F32 = jnp.float32
BF16 = jnp.bfloat16
HI = lax.Precision.HIGHEST

D_MODEL = 2048
DEPTH = 2
PAST_LEN = 16384
PAGE_SIZE = 128
GROUP_WIDTH = 512
SSD_HEADS = 8
SSD_HEADDIM = 64
SSD_STATE = 128
SSD_CONV = 4
SSD_CHUNK = 128
SSD_CONV_DIM = 1024
MOBA_HEADS = 4
MOBA_HEAD_DIM = 128
MOBA_BLOCK = 256
MOBA_TOPK = 3
ROPE_THETA = 10000.0
GMLP_CHUNK = 128
GMLP_HEADS = 4
RWKV_HEADS = 8
RWKV_HEAD = 64
RWKV_COLS = 1792
RWKV_CHUNK = 64
RWKV_SUB = 16
RWKV_DECAY_SCALE = 0.606531
RWKV_LN_EPS = 64e-5
D_FF = 5632
N_MOD = 9
FFN_RES = 0.5
EPS = 1e-6
NEG = -1e30

COLS = 6144
C_Z, C_X, C_BC = 0, 512, 1024
C_MQ, C_MK, C_MV = 1536, 2048, 2560
C_GU, C_GV = 3072, 3584
C_RR, C_RK, C_RV, C_RL = 4096, 4608, 5120, 5632
C_DT = 5888

LANE = 128
SAMPLE_PAD = 128
VMEM_LIMIT = 56 * 1024 * 1024


def _params(n_axes, vmem=VMEM_LIMIT):
    return pltpu.CompilerParams(dimension_semantics=("arbitrary",) * n_axes, vmem_limit_bytes=vmem)


def _dot(a, b, precision=None):
    return jnp.dot(a, b, preferred_element_type=F32, precision=precision)


def _dot_t(a, b, precision=None):
    return lax.dot_general(a, b, (((1,), (1,)), ((), ())), preferred_element_type=F32, precision=precision)


def _dot_0(a, b, precision=None):
    return lax.dot_general(a, b, (((0,), (0,)), ((), ())), preferred_element_type=F32, precision=precision)


def _iota(shape, dim):
    return lax.broadcasted_iota(jnp.int32, shape, dim)


def _rms(x, w):
    return x * lax.rsqrt(jnp.mean(x * x, -1, keepdims=True) + EPS) * w


def _silu(x):
    return x * jax.nn.sigmoid(x)


def _ada_kernel(c_ref, w_ref, b_ref, o_ref):
    s = _silu(c_ref[...]).astype(BF16)
    o_ref[...] = _dot(s, w_ref[...].astype(BF16)) + b_ref[...]


def _ada(c_all, w_ada, b_ada3, layer):
    rows = c_all.shape[0]
    n_out = w_ada.shape[-1]
    tn = 1024
    return pl.pallas_call(
        _ada_kernel,
        grid=(n_out // tn,),
        in_specs=[
            pl.BlockSpec((rows, D_MODEL), lambda n: (0, 0)),
            pl.BlockSpec((None, D_MODEL, tn), lambda n: (layer, 0, n)),
            pl.BlockSpec((None, 1, tn), lambda n: (layer, 0, n)),
        ],
        out_specs=pl.BlockSpec((rows, tn), lambda n: (0, n)),
        out_shape=jax.ShapeDtypeStruct((rows, n_out), F32),
        compiler_params=_params(1),
        name="ada",
    )(c_all, w_ada, b_ada3)


class _Mod:
    def __init__(self, arr, per_row, rows_per_batch, tm):
        self.arr = arr
        self.per_row = per_row
        self.tiles_per_batch = None if per_row else rows_per_batch // tm
        self.tm = tm

    def spec(self, j):
        if self.per_row:
            return pl.BlockSpec((None, self.tm, D_MODEL), lambda r, *_: (j, r, 0))
        tpb = self.tiles_per_batch
        return pl.BlockSpec((None, None, 1, D_MODEL), lambda r, *_: (r // tpb, j, 0, 0))


def _ffn_kernel(x_ref, nw_ref, sh_ref, sc_ref, g_ref, w1_ref, w3_ref, w2_ref, o_ref, h_ref, acc_ref, *, nf):
    f = pl.program_id(1)

    @pl.when(f == 0)
    def _():
        xn = _rms(x_ref[...], nw_ref[...])
        h_ref[...] = (xn * (1 + sc_ref[...]) + sh_ref[...]).astype(BF16)
        acc_ref[...] = jnp.zeros_like(acc_ref)

    h = h_ref[...]
    a = _dot(h, w1_ref[...])
    b = _dot(h, w3_ref[...])
    acc_ref[...] += _dot((_silu(a) * b).astype(BF16), w2_ref[...])

    @pl.when(f == nf - 1)
    def _():
        o_ref[...] = x_ref[...] + FFN_RES * g_ref[...] * acc_ref[...]


def _ffn(x, mod, j0, nw4, w1, w3, w2, layer, slot, tm, tf):
    rows = x.shape[0]
    nf = D_FF // tf
    return pl.pallas_call(
        functools.partial(_ffn_kernel, nf=nf),
        grid=(rows // tm, nf),
        in_specs=[
            pl.BlockSpec((tm, D_MODEL), lambda r, f: (r, 0)),
            pl.BlockSpec((None, None, 1, D_MODEL), lambda r, f: (layer, 2 * slot, 0, 0)),
            mod.spec(j0), mod.spec(j0 + 1), mod.spec(j0 + 2),
            pl.BlockSpec((None, None, D_MODEL, tf), lambda r, f: (layer, slot, 0, f)),
            pl.BlockSpec((None, None, D_MODEL, tf), lambda r, f: (layer, slot, 0, f)),
            pl.BlockSpec((None, None, tf, D_MODEL), lambda r, f: (layer, slot, f, 0)),
        ],
        out_specs=pl.BlockSpec((tm, D_MODEL), lambda r, f: (r, 0)),
        out_shape=jax.ShapeDtypeStruct((rows, D_MODEL), F32),
        scratch_shapes=[pltpu.VMEM((tm, D_MODEL), BF16), pltpu.VMEM((tm, D_MODEL), F32)],
        compiler_params=_params(2),
        name="ffn",
    )(x, nw4, mod.arr, mod.arr, mod.arr, w1, w3, w2)


def _inproj_kernel(x_ref, nw_ref, sh_ref, sc_ref, w_ref, o_ref, h_ref):
    @pl.when(pl.program_id(1) == 0)
    def _():
        xn = _rms(x_ref[...], nw_ref[...])
        h_ref[...] = (xn * (1 + sc_ref[...]) + sh_ref[...]).astype(BF16)

    o_ref[...] = _dot(h_ref[...], w_ref[...])


def _inproj(x, mod, nw4, w_in, layer, tm, tn):
    rows = x.shape[0]
    return pl.pallas_call(
        _inproj_kernel,
        grid=(rows // tm, COLS // tn),
        in_specs=[
            pl.BlockSpec((tm, D_MODEL), lambda r, n: (r, 0)),
            pl.BlockSpec((None, None, 1, D_MODEL), lambda r, n: (layer, 1, 0, 0)),
            mod.spec(3), mod.spec(4),
            pl.BlockSpec((None, D_MODEL, tn), lambda r, n: (layer, 0, n)),
        ],
        out_specs=pl.BlockSpec((tm, tn), lambda r, n: (r, n)),
        out_shape=jax.ShapeDtypeStruct((rows, COLS), F32),
        scratch_shapes=[pltpu.VMEM((tm, D_MODEL), BF16)],
        compiler_params=_params(2),
        name="inproj",
    )(x, nw4, mod.arr, mod.arr, w_in)


def _outproj_kernel(x_ref, g_ref, y0_ref, y1_ref, y2_ref, y3_ref, w_ref, o_ref):
    G = GROUP_WIDTH
    acc = _dot(y0_ref[...], w_ref[0:G, :])
    acc += _dot(y1_ref[...], w_ref[G:2 * G, :])
    acc += _dot(y2_ref[...], w_ref[2 * G:3 * G, :])
    acc += _dot(y3_ref[...], w_ref[3 * G:4 * G, :])
    o_ref[...] = x_ref[...] + g_ref[...] * acc


def _outproj(x, mod, ys, w_out, layer, tm):
    rows = x.shape[0]
    yspec = pl.BlockSpec((tm, GROUP_WIDTH), lambda r: (r, 0))
    return pl.pallas_call(
        _outproj_kernel,
        grid=(rows // tm,),
        in_specs=[
            pl.BlockSpec((tm, D_MODEL), lambda r: (r, 0)),
            mod.spec(5), yspec, yspec, yspec, yspec,
            pl.BlockSpec((None, D_MODEL, D_MODEL), lambda r: (layer, 0, 0)),
        ],
        out_specs=pl.BlockSpec((tm, D_MODEL), lambda r: (r, 0)),
        out_shape=jax.ShapeDtypeStruct((rows, D_MODEL), F32),
        compiler_params=_params(1),
        name="outproj",
    )(x, mod.arr, *ys, w_out)


def _softplus(x):
    return jnp.maximum(x, 0.0) + jnp.log1p(jnp.exp(-jnp.abs(x)))


def _ssd_kernel(z_ref, x_ref, bc_ref, dt_ref, prev_ref, cw_ref, cb_ref, dtb_ref, alog_ref, dskip_ref, nw_ref,
                h0_ref, y_ref, hout_ref, ext_ref, st_ref, *, C, nc, t_valid):
    c = pl.program_id(1)
    G = GROUP_WIDTH

    @pl.when(c == 0)
    def _():
        ext_ref[0:8, :] = prev_ref[...]
        st_ref[...] = h0_ref[...]

    ext_ref[8:8 + C, 0:G] = x_ref[...]
    ext_ref[8:8 + C, G:2 * G] = bc_ref[...]
    conv = cb_ref[...] + ext_ref[5:5 + C, :] * cw_ref[0:1, :]
    for i in range(1, SSD_CONV):
        conv = conv + ext_ref[5 + i:5 + i + C, :] * cw_ref[i:i + 1, :]
    ext_ref[0:8, :] = ext_ref[C:C + 8, :]
    xbc = _silu(conv)
    xs = xbc[:, 0:G]

    dt = _softplus(dt_ref[...] + dtb_ref[...])
    if t_valid is not None:
        dt = jnp.where(c * C + _iota((C, LANE), 0) < t_valid, dt, 0.0)
    a = dt * (-jnp.exp(alog_ref[...]))
    tri = _iota((C, C), 0) >= _iota((C, C), 1)
    a_cs = _dot(tri.astype(F32), a, HI)
    a_cs_t = a_cs.T
    lane_lo = _iota((C, LANE), 1) < SSD_HEADDIM
    row_lo = _iota((LANE, 1), 0) < SSD_HEADDIM

    ys = []
    for p in range(SSD_HEADS // 2):
        g = p // 2
        h0, h1 = 2 * p, 2 * p + 1
        bm = xbc[:, G + g * SSD_STATE:G + (g + 1) * SSD_STATE]
        cm = xbc[:, G + 2 * SSD_STATE + g * SSD_STATE:G + 2 * SSD_STATE + (g + 1) * SSD_STATE]
        col0, col1 = a_cs[:, h0:h0 + 1], a_cs[:, h1:h1 + 1]
        row0, row1 = a_cs_t[h0:h0 + 1, :], a_cs_t[h1:h1 + 1, :]
        scores = _dot_t(cm, bm, HI)
        m0 = scores * jnp.exp(jnp.where(tri, col0 - row0, -jnp.inf))
        m1 = scores * jnp.exp(jnp.where(tri, col1 - row1, -jnp.inf))
        xs_p = xs[:, p * LANE:(p + 1) * LANE]
        xdt = xs_p * jnp.where(lane_lo, dt[:, h0:h0 + 1], dt[:, h1:h1 + 1])
        y_diag = jnp.where(lane_lo, _dot(m0, xdt, HI), _dot(m1, xdt, HI))
        st = st_ref[p * LANE:(p + 1) * LANE, :]
        y_off = _dot_t(cm, st, HI) * jnp.where(lane_lo, jnp.exp(col0), jnp.exp(col1))
        last0, last1 = a_cs[C - 1:C, h0:h0 + 1], a_cs[C - 1:C, h1:h1 + 1]
        decay = jnp.where(lane_lo, jnp.exp(last0 - col0), jnp.exp(last1 - col1))
        new = _dot_0(xdt * decay, bm, HI)
        st_ref[p * LANE:(p + 1) * LANE, :] = st * jnp.where(row_lo, jnp.exp(last0), jnp.exp(last1)) + new
        ys.append(y_diag + y_off + xs_p * dskip_ref[:, p * LANE:(p + 1) * LANE])

    y = jnp.concatenate(ys, axis=1) * _silu(z_ref[...])
    y_ref[...] = _rms(y, nw_ref[...]).astype(y_ref.dtype)

    @pl.when(c == nc - 1)
    def _():
        hout_ref[...] = st_ref[...]


def _ssd(cols3, prev8, h0, cw8, cb, dtb, alog, dskip, nw, t_valid):
    B, T, _ = cols3.shape
    C = SSD_CHUNK
    nc = T // C
    G = GROUP_WIDTH
    vec = lambda n: pl.BlockSpec((1, n), lambda b, c: (0, 0))
    return pl.pallas_call(
        functools.partial(_ssd_kernel, C=C, nc=nc, t_valid=t_valid),
        grid=(B, nc),
        in_specs=[
            pl.BlockSpec((None, C, G), lambda b, c: (b, c, C_Z // G)),
            pl.BlockSpec((None, C, G), lambda b, c: (b, c, C_X // G)),
            pl.BlockSpec((None, C, G), lambda b, c: (b, c, C_BC // G)),
            pl.BlockSpec((None, C, LANE), lambda b, c: (b, c, C_DT // LANE)),
            pl.BlockSpec((None, 8, SSD_CONV_DIM), lambda b, c: (b, 0, 0)),
            pl.BlockSpec((8, SSD_CONV_DIM), lambda b, c: (0, 0)),
            vec(SSD_CONV_DIM), vec(LANE), vec(LANE), vec(G), vec(G),
            pl.BlockSpec((None, SSD_HEADS * SSD_HEADDIM, SSD_STATE), lambda b, c: (b, 0, 0)),
        ],
        out_specs=[
            pl.BlockSpec((None, C, G), lambda b, c: (b, c, 0)),
            pl.BlockSpec((None, SSD_HEADS * SSD_HEADDIM, SSD_STATE), lambda b, c: (b, 0, 0)),
        ],
        out_shape=[
            jax.ShapeDtypeStruct((B, T, G), BF16),
            jax.ShapeDtypeStruct((B, SSD_HEADS * SSD_HEADDIM, SSD_STATE), F32),
        ],
        scratch_shapes=[pltpu.VMEM((C + 8, SSD_CONV_DIM), F32), pltpu.VMEM((SSD_HEADS * SSD_HEADDIM, SSD_STATE), F32)],
        compiler_params=_params(2),
        name="ssd",
    )(cols3, cols3, cols3, cols3, prev8, cw8, cb, dtb, alog, dskip, nw, h0)


def _gmlp_kernel(u_ref, v_ref, vw_ref, ws_ref, bs_ref, ow_ref, y_ref, vout_ref, *, C):
    u = jax.nn.gelu(u_ref[...])
    v = _rms(jax.nn.gelu(v_ref[...]), vw_ref[...])
    vout_ref[...] = v
    tri = _iota((C, C), 0) >= _iota((C, C), 1)
    mixed = []
    for h in range(GMLP_HEADS):
        ws = jnp.where(tri, ws_ref[h], 0.0)
        mixed.append(_dot(ws, v[:, h * LANE:(h + 1) * LANE], HI) + bs_ref[h])
    y = u * jnp.concatenate(mixed, axis=1)
    y_ref[...] = _rms(y, ow_ref[...]).astype(y_ref.dtype)


def _gmlp(cols3, vw, ws, bs, ow):
    B, T, _ = cols3.shape
    C = GMLP_CHUNK
    G = GROUP_WIDTH
    vec = pl.BlockSpec((1, G), lambda b, c: (0, 0))
    return pl.pallas_call(
        functools.partial(_gmlp_kernel, C=C),
        grid=(B, T // C),
        in_specs=[
            pl.BlockSpec((None, C, G), lambda b, c: (b, c, C_GU // G)),
            pl.BlockSpec((None, C, G), lambda b, c: (b, c, C_GV // G)),
            vec,
            pl.BlockSpec((GMLP_HEADS, C, C), lambda b, c: (0, 0, 0)),
            pl.BlockSpec((GMLP_HEADS, C, 1), lambda b, c: (0, 0, 0)),
            vec,
        ],
        out_specs=[pl.BlockSpec((None, C, G), lambda b, c: (b, c, 0))] * 2,
        out_shape=[jax.ShapeDtypeStruct((B, T, G), BF16), jax.ShapeDtypeStruct((B, T, G), F32)],
        compiler_params=_params(2),
        name="gmlp",
    )(cols3, cols3, vw, ws, bs, ow)


_NN = ((1,), (0,))
_NT = ((1,), (1,))
_TN = ((0,), (0,))


def _split2(x):
    hi = x.astype(BF16)
    return hi, (x - hi.astype(F32)).astype(BF16)


def _split3(x):
    hi = x.astype(BF16)
    rest = x - hi.astype(F32)
    mid = rest.astype(BF16)
    return hi, mid, (rest - mid.astype(F32)).astype(BF16)


def _dg(a, b, dims):
    return lax.dot_general(a, b, (dims, ((), ())), preferred_element_type=F32)


def _dot3(a, b, dims=_NN):
    (ah, al), (bh, bl) = a, b
    return _dg(ah, bh, dims) + (_dg(ah, bl, dims) + _dg(al, bh, dims))


def _block_diag(pieces, same_head):
    return tuple(jnp.where(same_head, jnp.concatenate([x, x], axis=0), jnp.zeros((), x.dtype)) for x in pieces)


def _rwkv_kernel(r_ref, k_ref, v_ref, lo_ref, prev_ref, mu_ref, w0_ref, wup_ref, a0_ref, aup_ref, gup_ref,
                 kk_ref, ka_ref, rk_ref, lnw_ref, lnb_ref, s0_ref, y_ref, sout_ref, ext_ref, st_ref,
                 *, C, nc, t_valid):
    c = pl.program_id(1)
    G = GROUP_WIDTH
    N = RWKV_HEAD
    P = 2 * N

    @pl.when(c == 0)
    def _():
        ext_ref[0:8, :] = prev_ref[...]
        st_ref[...] = s0_ref[...]

    ext_ref[8:8 + C, 0:G] = r_ref[...]
    ext_ref[8:8 + C, G:2 * G] = k_ref[...]
    ext_ref[8:8 + C, 2 * G:3 * G] = v_ref[...]
    ext_ref[8:8 + C, 3 * G:RWKV_COLS] = lo_ref[...]
    cur = ext_ref[8:8 + C, :]
    prev = ext_ref[7:7 + C, :]
    xs = cur + (prev - cur) * mu_ref[...]
    ext_ref[0:8, :] = ext_ref[C:C + 8, :]

    r, k, v = xs[:, 0:G], xs[:, G:2 * G], xs[:, 2 * G:3 * G]
    la = xs[:, 3 * G:3 * G + P]
    gl = xs[:, 3 * G + P:RWKV_COLS]
    w_log = -RWKV_DECAY_SCALE * jax.nn.sigmoid(
        w0_ref[...] + _dot3(_split2(jnp.tanh(la)), _split2(wup_ref[...])))
    a = jax.nn.sigmoid(a0_ref[...] + _dot3(_split2(la), _split2(aup_ref[...])))
    g = _dot3(_split2(jax.nn.sigmoid(gl)), _split2(gup_ref[...]))

    n_pairs = G // P
    sls = [slice(p * P, (p + 1) * P) for p in range(n_pairs)]
    each = lambda f: [f(p) for p in range(n_pairs)]
    same_head_n = (_iota((P, P), 0) < N) == (_iota((P, P), 1) < N)
    ones_bd = same_head_n.astype(BF16)

    def head_sum(x):
        def pair_sum(p):
            hi, lo = _split2(x[:, sls[p]])
            return _dg(hi, ones_bd, _NN) + _dg(lo, ones_bd, _NN)
        return jnp.concatenate(each(pair_sum), axis=1)

    kk = k * kk_ref[...]
    kk = kk * lax.rsqrt(jnp.maximum(head_sum(kk * kk), 1e-12))
    k2 = k * (1 + (a - 1) * ka_ref[...])
    if t_valid is not None:
        ok = c * C + _iota((C, G), 0) < t_valid
        w_log = jnp.where(ok, w_log, 0.0)
        kk = jnp.where(ok, kk, 0.0)
        k2 = jnp.where(ok, k2, 0.0)
    b = kk * a

    tri_cc = (_iota((C, C), 0) >= _iota((C, C), 1)).astype(BF16)
    w_hi, w_mid, w_lo = _split3(w_log)
    cl = _dg(tri_cc, w_hi, _NN) + (_dg(tri_cc, w_mid, _NN) + _dg(tri_cc, w_lo, _NN))
    cl_last = cl[C - 1:C, :]
    kkp = kk * jnp.exp(cl - w_log)
    rp = r * jnp.exp(cl)
    einv = jnp.exp(-cl)
    bi, ki = b * einv, k2 * einv
    e_c = jnp.exp(cl_last - cl)
    bt, kt = b * e_c, k2 * e_c
    p_c = jnp.exp(cl_last)

    t_i = _iota((C, P), 0)
    s_i = _iota((C, P), 1) % C
    strict = s_i < t_i
    incl = s_i <= t_i
    diag_blk = (s_i // RWKV_SUB) == (t_i // RWKV_SUB)
    eye = (s_i == t_i).astype(F32)
    same_head = (_iota((P, P), 0) < C) == (_iota((P, P), 1) < C)

    memo = {}

    def sp(x):
        if id(x) not in memo:
            memo[id(x)] = (x, _split2(x))
        return memo[id(x)][1]

    def bd(x):
        if ("bd", id(x)) not in memo:
            memo["bd", id(x)] = (x, _block_diag(sp(x), same_head))
        return memo["bd", id(x)][1]

    def mm(xs, ys):
        return each(lambda p: _dot3(sp(xs[p]), bd(ys[p])))

    kkp_s = each(lambda p: _split2(kkp[:, sls[p]]))
    rp_s = each(lambda p: _split2(rp[:, sls[p]]))
    lhs_s = each(lambda p: tuple(jnp.concatenate([x, y], axis=0) for x, y in zip(kkp_s[p], rp_s[p])))
    ab = each(lambda p: _dot3(lhs_s[p], _block_diag(_split2(bi[:, sls[p]]), same_head_n), _NT))
    ak = each(lambda p: _dot3(lhs_s[p], _block_diag(_split2(ki[:, sls[p]]), same_head_n), _NT))
    a_m = each(lambda p: jnp.where(strict, ab[p][0:C], 0.0))
    b_k = each(lambda p: jnp.where(strict, ak[p][0:C], 0.0))
    r_b = each(lambda p: jnp.where(incl, ab[p][C:2 * C], 0.0))
    r_k = each(lambda p: jnp.where(incl, ak[p][C:2 * C], 0.0))

    n_pow = each(lambda p: jnp.where(diag_blk, -a_m[p], 0.0))
    a_o = each(lambda p: jnp.where(diag_blk, 0.0, a_m[p]))
    t_d = each(lambda p: eye + n_pow[p])
    for _ in range(int(math.log2(RWKV_SUB)) - 1):
        n_pow = mm(n_pow, n_pow)
        step = mm(t_d, n_pow)
        t_d = each(lambda p: t_d[p] + step[p])
    m1 = mm(t_d, a_o)
    m2 = mm(m1, m1)
    im = each(lambda p: eye - m1[p])
    im_m2 = mm(im, m2)
    t_full = mm(each(lambda p: im[p] + im_m2[p]), t_d)

    st = each(lambda p: st_ref[p])
    st_s = each(lambda p: _split2(st[p]))
    v_p = each(lambda p: v[:, sls[p]])
    bkv = mm(b_k, v_p)
    rhs = each(lambda p: _dot3(kkp_s[p], st_s[p], _NT) + bkv[p])
    u = mm(t_full, rhs)
    rkv = mm(r_k, v_p)
    rbu = mm(r_b, u)
    ys = each(lambda p: _dot3(rp_s[p], st_s[p], _NT) + rkv[p] - rbu[p])
    upd = each(lambda p: _dot3(_split2(jnp.concatenate([v_p[p], -u[p]], axis=0)),
                               _split2(jnp.concatenate([kt[:, sls[p]], bt[:, sls[p]]], axis=0)), _TN))
    for p in range(n_pairs):
        st_ref[p] = st[p] * p_c[:, sls[p]] + jnp.where(same_head_n, upd[p], 0.0)

    y = jnp.concatenate(ys, axis=1)
    mean = head_sum(y) * (1.0 / N)
    d = y - mean
    var = head_sum(d * d) * (1.0 / N)
    yn = d * lax.rsqrt(var + RWKV_LN_EPS) * lnw_ref[...] + lnb_ref[...]
    bonus = head_sum(r * k2 * rk_ref[...]) * v
    y_ref[...] = ((yn + bonus) * g).astype(y_ref.dtype)

    @pl.when(c == nc - 1)
    def _():
        sout_ref[...] = st_ref[...]


def _rwkv(cols3, prev8, s0_bd, mu, w0, wup, a0, aup, gup, kk, ka, rk, lnw, lnb, t_valid):
    B, T, _ = cols3.shape
    C = RWKV_CHUNK
    nc = T // C
    G = GROUP_WIDTH
    P = 2 * RWKV_HEAD
    n_pairs = RWKV_HEADS // 2
    vec = lambda n: pl.BlockSpec((1, n), lambda b, c: (0, 0))
    mat = lambda m, n: pl.BlockSpec((m, n), lambda b, c: (0, 0))
    return pl.pallas_call(
        functools.partial(_rwkv_kernel, C=C, nc=nc, t_valid=t_valid),
        grid=(B, nc),
        in_specs=[
            pl.BlockSpec((None, C, G), lambda b, c: (b, c, C_RR // G)),
            pl.BlockSpec((None, C, G), lambda b, c: (b, c, C_RK // G)),
            pl.BlockSpec((None, C, G), lambda b, c: (b, c, C_RV // G)),
            pl.BlockSpec((None, C, 2 * P), lambda b, c: (b, c, C_RL // (2 * P))),
            pl.BlockSpec((None, 8, RWKV_COLS), lambda b, c: (b, 0, 0)),
            vec(RWKV_COLS), vec(G), mat(P, G), vec(G), mat(P, G), mat(P, G),
            vec(G), vec(G), vec(G), vec(G), vec(G),
            pl.BlockSpec((None, n_pairs, P, P), lambda b, c: (b, 0, 0, 0)),
        ],
        out_specs=[
            pl.BlockSpec((None, C, G), lambda b, c: (b, c, 0)),
            pl.BlockSpec((None, n_pairs, P, P), lambda b, c: (b, 0, 0, 0)),
        ],
        out_shape=[
            jax.ShapeDtypeStruct((B, T, G), BF16),
            jax.ShapeDtypeStruct((B, n_pairs, P, P), F32),
        ],
        scratch_shapes=[pltpu.VMEM((C + 8, RWKV_COLS), F32), pltpu.VMEM((n_pairs, P, P), F32)],
        compiler_params=_params(2),
        name="rwkv",
    )(cols3, cols3, cols3, cols3, prev8, mu, w0, wup, a0, aup, gup, kk, ka, rk, lnw, lnb, s0_bd)


def _moba_prep_kernel(q_ref, k_ref, cos_ref, sin_ref, qw_ref, kw_ref, qo_ref, ko_ref):
    cos, sin = cos_ref[...], sin_ref[...]
    for h in range(MOBA_HEADS):
        sl = slice(h * LANE, (h + 1) * LANE)
        for src, w_ref, dst in ((q_ref, qw_ref, qo_ref), (k_ref, kw_ref, ko_ref)):
            xn = _rms(src[:, sl], w_ref[...])
            dst[:, sl] = xn * cos + pltpu.roll(xn, MOBA_HEAD_DIM // 2, 1) * sin


def _moba_prep(cols, cos, sin, qw, kw, tm):
    rows = cols.shape[0]
    G = GROUP_WIDTH
    tab = pl.BlockSpec((tm, LANE), lambda r: (r, 0))
    vec = pl.BlockSpec((1, LANE), lambda r: (0, 0))
    return pl.pallas_call(
        _moba_prep_kernel,
        grid=(rows // tm,),
        in_specs=[
            pl.BlockSpec((tm, G), lambda r: (r, C_MQ // G)),
            pl.BlockSpec((tm, G), lambda r: (r, C_MK // G)),
            tab, tab, vec, vec,
        ],
        out_specs=[pl.BlockSpec((tm, G), lambda r: (r, 0))] * 2,
        out_shape=[jax.ShapeDtypeStruct((rows, G), F32)] * 2,
        compiler_params=_params(1),
        name="moba_prep",
    )(cols, cols, cos, sin, qw, kw)


def _top3(gate, lane_f):
    sel = jnp.zeros(gate.shape, jnp.bool_)
    g = gate
    big = float(gate.shape[-1])
    idxs = []
    for _ in range(MOBA_TOPK):
        m = jnp.max(g, axis=-1, keepdims=True)
        idx = jnp.min(jnp.where(g == m, lane_f, big), axis=-1, keepdims=True)
        pick = lane_f == idx
        sel = sel | pick
        g = jnp.where(pick, -jnp.inf, g)
        idxs.append(idx)
    return sel, idxs


def _moba_attn_kernel(q_ref, k_ref, v_ref, ow_ref, y_ref, km_ref, o_ref, *, nb):
    qi = pl.program_id(1)
    BLK = MOBA_BLOCK
    scale = MOBA_HEAD_DIM ** -0.5

    @pl.when(qi == 0)
    def _():
        km_ref[...] = jnp.zeros_like(km_ref)
        for j in range(nb):
            km_ref[j:j + 1, :] = jnp.mean(k_ref[j * BLK:(j + 1) * BLK, :], axis=0, keepdims=True)

    lane_i = _iota((BLK, LANE), 1)
    lane_f = lane_i.astype(F32)
    past = lane_i < qi
    causal = _iota((BLK, BLK), 0) >= _iota((BLK, BLK), 1)
    own = pl.multiple_of(qi * BLK, BLK)

    for h in range(MOBA_HEADS):
        sl = slice(h * LANE, (h + 1) * LANE)
        q = q_ref[:, sl]
        gate = jnp.where(past, _dot_t(q, km_ref[:, sl], HI), NEG)
        sel, _ = _top3(gate, lane_f)
        sel_f = (sel & past).astype(F32)
        qb = q.astype(BF16)

        s = _dot_t(qb, k_ref[pl.ds(own, BLK), sl].astype(BF16)) * scale
        s = jnp.where(causal, s, NEG)
        m0 = jnp.max(s, axis=-1, keepdims=True)
        p0 = jnp.exp(s - m0)
        l0 = jnp.sum(p0, axis=-1, keepdims=True)
        acc0 = _dot(p0.astype(BF16), v_ref[pl.ds(own, BLK), sl].astype(BF16))

        def body(j, carry):
            m_i, l_i, acc = carry
            start = pl.multiple_of(j * BLK, BLK)
            s = _dot_t(qb, k_ref[pl.ds(start, BLK), sl].astype(BF16)) * scale
            hit = jnp.sum(jnp.where(lane_i == j, sel_f, 0.0), axis=-1, keepdims=True) > 0.0
            s = jnp.where(hit, s, NEG)
            m_n = jnp.maximum(m_i, jnp.max(s, axis=-1, keepdims=True))
            alpha = jnp.exp(m_i - m_n)
            p = jnp.exp(s - m_n)
            l_n = alpha * l_i + jnp.sum(p, axis=-1, keepdims=True)
            acc = alpha * acc + _dot(p.astype(BF16), v_ref[pl.ds(start, BLK), sl].astype(BF16))
            return m_n, l_n, acc

        _, l_f, acc_f = lax.fori_loop(0, qi, body, (m0, l0, acc0))
        o_ref[:, sl] = acc_f / l_f

    y_ref[...] = _rms(o_ref[...], ow_ref[...]).astype(y_ref.dtype)


def _moba_attn(q3, k3, cols3, ow):
    B, T, G = q3.shape
    BLK = MOBA_BLOCK
    nb = T // BLK
    return pl.pallas_call(
        functools.partial(_moba_attn_kernel, nb=nb),
        grid=(B, nb),
        in_specs=[
            pl.BlockSpec((None, BLK, G), lambda b, i: (b, i, 0)),
            pl.BlockSpec((None, T, G), lambda b, i: (b, 0, 0)),
            pl.BlockSpec((None, T, G), lambda b, i: (b, 0, C_MV // G)),
            pl.BlockSpec((1, G), lambda b, i: (0, 0)),
        ],
        out_specs=pl.BlockSpec((None, BLK, G), lambda b, i: (b, i, 0)),
        out_shape=jax.ShapeDtypeStruct((B, T, G), BF16),
        scratch_shapes=[pltpu.VMEM((LANE, G), F32), pltpu.VMEM((BLK, G), F32)],
        compiler_params=_params(2),
        name="moba_attn",
    )(q3, k3, cols3, ow)


PAGES_PER_BLOCK = MOBA_BLOCK // PAGE_SIZE
KMEAN_BLOCKS = 8


def _kmean_kernel(pt_ref, *refs):
    page_refs, o_ref = refs[:-1], refs[-1]
    for j in range(KMEAN_BLOCKS):
        s = jnp.sum(page_refs[PAGES_PER_BLOCK * j][...], axis=0)
        for o in range(1, PAGES_PER_BLOCK):
            s = s + jnp.sum(page_refs[PAGES_PER_BLOCK * j + o][...], axis=0)
        o_ref[j] = s * (1.0 / MOBA_BLOCK)


def _kmean_pages(cache_k, page_table_flat, layer, batch, n_pages):
    nbk = n_pages // PAGES_PER_BLOCK
    per_step = KMEAN_BLOCKS * PAGES_PER_BLOCK

    def page(o):
        return pl.BlockSpec((None, None, PAGE_SIZE, MOBA_HEADS, MOBA_HEAD_DIM),
                            lambda b, j, pt: (layer, pt[b * n_pages + j * per_step + o], 0, 0, 0))

    return pl.pallas_call(
        _kmean_kernel,
        grid_spec=pltpu.PrefetchScalarGridSpec(
            num_scalar_prefetch=1,
            grid=(batch, nbk // KMEAN_BLOCKS),
            in_specs=[page(o) for o in range(per_step)],
            out_specs=pl.BlockSpec((None, KMEAN_BLOCKS, MOBA_HEADS, MOBA_HEAD_DIM), lambda b, j, pt: (b, j, 0, 0)),
        ),
        out_shape=jax.ShapeDtypeStruct((batch, nbk, MOBA_HEADS, MOBA_HEAD_DIM), F32),
        compiler_params=_params(2),
        name="moba_kmean_pages",
    )(page_table_flat, *([cache_k] * per_step))


def _select_kernel(q_ref, km_ref, idx_ref, *, batch, t_new, nbk):
    rows = batch * t_new
    lane_i = _iota((rows, LANE), 1)
    lane_f = lane_i.astype(F32)
    row_b = _iota((rows, LANE), 0) // t_new
    out = jnp.zeros((rows, LANE), F32)
    pad = jnp.zeros((LANE - nbk, LANE), F32)
    for h in range(MOBA_HEADS):
        sl = slice(h * LANE, (h + 1) * LANE)
        q = q_ref[:, sl]
        gate = jnp.zeros((rows, LANE), F32)
        for b in range(batch):
            km = jnp.concatenate([km_ref[b, :, h, :], pad], axis=0)
            gate = jnp.where(row_b == b, _dot_t(q, km, HI), gate)
        gate = jnp.where(lane_i < nbk, gate, NEG)
        _, idxs = _top3(gate, lane_f)
        for kth, idx in enumerate(idxs):
            out = jnp.where(lane_i == h * MOBA_HEADS + kth, idx, out)
    idx_ref[...] = out.astype(jnp.int32)


def _select(q_rot, kmean, batch, t_new):
    rows = batch * t_new
    nbk = kmean.shape[1]
    G = GROUP_WIDTH
    return pl.pallas_call(
        functools.partial(_select_kernel, batch=batch, t_new=t_new, nbk=nbk),
        grid=(1,),
        in_specs=[
            pl.BlockSpec((rows, G), lambda i: (0, 0)),
            pl.BlockSpec((batch, nbk, MOBA_HEADS, MOBA_HEAD_DIM), lambda i: (0, 0, 0, 0)),
        ],
        out_specs=pl.BlockSpec((rows, LANE), lambda i: (0, 0)),
        out_shape=jax.ShapeDtypeStruct((rows, LANE), jnp.int32),
        compiler_params=_params(1),
        name="moba_select",
    )(q_rot, kmean)


def _sample_attn_kernel(pt_ref, idx_ref, q_ref, kn_ref, vn_ref, *refs, t_new):
    n_blk = PAGES_PER_BLOCK * MOBA_TOPK
    k_refs, v_refs, o_ref = refs[:n_blk], refs[n_blk:2 * n_blk], refs[2 * n_blk]
    b, t = pl.program_id(0), pl.program_id(1)
    rows = q_ref.shape[0]
    scale = MOBA_HEAD_DIM ** -0.5
    row = b * t_new + t
    q8 = jnp.broadcast_to(q_ref[pl.ds(row, 1), :], (8, LANE)).astype(BF16)
    s_new = _dot_t(q8, kn_ref[...].astype(BF16)) * scale
    r_i = _iota((8, rows), 1)
    s_new = jnp.where((r_i >= b * t_new) & (r_i <= row), s_new, NEG)
    p_scale = jnp.max(s_new, axis=-1, keepdims=True)

    for head in range(MOBA_HEADS):
        @pl.when(pl.program_id(2) == head)
        def _():
            s_old = [_dot_t(q8, kr[:, head, :].astype(BF16)) * scale for kr in k_refs]
            m = p_scale
            for s in s_old:
                m = jnp.maximum(m, jnp.max(s, axis=-1, keepdims=True))
            p_new = jnp.exp(s_new - m)
            l = jnp.sum(p_new, axis=-1, keepdims=True)
            acc = _dot(p_new.astype(BF16), vn_ref[...].astype(BF16))
            for s, vr in zip(s_old, v_refs):
                p = jnp.exp(s - m)
                l = l + jnp.sum(p, axis=-1, keepdims=True)
                acc = acc + _dot(p.astype(BF16), vr[:, head, :].astype(BF16))
            o_ref[...] = (acc / l)[0:1, :]


def _sample_attn(q_rot, k_new, cols, cache_k, cache_v, page_table_flat, idx_flat, layer, batch, t_new, n_pages):
    rows = batch * t_new
    G = GROUP_WIDTH

    def page(kth, o):
        def index_map(b, t, h, pt, idx):
            blk = idx[(b * t_new + t) * LANE + h * MOBA_HEADS + kth]
            return (layer, pt[b * n_pages + PAGES_PER_BLOCK * blk + o], 0, 0, 0)
        return pl.BlockSpec((None, None, PAGE_SIZE, MOBA_HEADS, MOBA_HEAD_DIM), index_map)

    pages = [page(kth, o) for kth in range(MOBA_TOPK) for o in range(PAGES_PER_BLOCK)]
    return pl.pallas_call(
        functools.partial(_sample_attn_kernel, t_new=t_new),
        grid_spec=pltpu.PrefetchScalarGridSpec(
            num_scalar_prefetch=2,
            grid=(batch, t_new, MOBA_HEADS),
            in_specs=[
                pl.BlockSpec((rows, LANE), lambda b, t, h, pt, idx: (0, h)),
                pl.BlockSpec((rows, LANE), lambda b, t, h, pt, idx: (0, h)),
                pl.BlockSpec((rows, LANE), lambda b, t, h, pt, idx: (0, C_MV // LANE + h)),
            ] + pages + pages,
            out_specs=pl.BlockSpec((None, 1, LANE), lambda b, t, h, pt, idx: (b * t_new + t, 0, h)),
        ),
        out_shape=jax.ShapeDtypeStruct((rows, 1, G), F32),
        compiler_params=_params(3),
        name="moba_sample_attn",
    )(page_table_flat, idx_flat, q_rot, k_new, cols, *([cache_k] * len(pages)), *([cache_v] * len(pages)))


def _rownorm_kernel(x_ref, w_ref, y_ref):
    y_ref[...] = _rms(x_ref[...], w_ref[...]).astype(y_ref.dtype)


def _rownorm(x, w):
    rows, n = x.shape
    return pl.pallas_call(
        _rownorm_kernel,
        grid=(1,),
        in_specs=[pl.BlockSpec((rows, n), lambda i: (0, 0)), pl.BlockSpec((1, n), lambda i: (0, 0))],
        out_specs=pl.BlockSpec((rows, n), lambda i: (0, 0)),
        out_shape=jax.ShapeDtypeStruct((rows, n), BF16),
        compiler_params=_params(1),
        name="rownorm",
    )(x, w)


def _rope_tables(pos):
    half = MOBA_HEAD_DIM // 2
    freq = ROPE_THETA ** (-jnp.arange(half, dtype=F32) / half)
    ang = pos.astype(F32)[:, None] * freq[None, :]
    cos, sin = jnp.cos(ang), jnp.sin(ang)
    return jnp.concatenate([cos, cos], -1), jnp.concatenate([-sin, sin], -1)


def _pad_rows_front(x, rows):
    return jnp.pad(x, ((0, 0), (rows - x.shape[1], 0), (0, 0)))


def _pair_block_diag(s):
    B, H, N, _ = s.shape
    s = s.reshape(B, H // 2, 2, N, 1, N) * jnp.eye(2, dtype=s.dtype)[None, None, :, None, :, None]
    return s.reshape(B, H // 2, 2 * N, 2 * N)


def _pair_diag_blocks(s_bd):
    B, n_pairs, P, _ = s_bd.shape
    N = P // 2
    s = s_bd.reshape(B, n_pairs, 2, N, 2, N)
    return jnp.stack([s[:, :, 0, :, 0, :], s[:, :, 1, :, 1, :]], axis=2).reshape(B, 2 * n_pairs, N, N)


def _layer_weights(W, i):
    G = GROUP_WIDTH
    row = lambda v: v.reshape(1, -1)
    lane_pad = lambda v: jnp.pad(v, (0, LANE - v.shape[0])).reshape(1, LANE)
    zeros_r = jnp.zeros((RWKV_HEAD, G), F32)
    return dict(
        cw8=jnp.pad(W['ssd_conv_w'][i].T, ((0, 8 - SSD_CONV), (0, 0))),
        cb=row(W['ssd_conv_b'][i]),
        dtb=lane_pad(W['ssd_dt_bias'][i]),
        alog=lane_pad(W['ssd_a_log'][i]),
        dskip=row(jnp.repeat(W['ssd_d'][i], GROUP_WIDTH // SSD_HEADS)),
        ssd_nw=row(W['ssd_norm_w'][i]),
        qw=row(W['moba_q_norm_w'][i]), kw=row(W['moba_k_norm_w'][i]), ow=row(W['moba_out_norm_w'][i]),
        gvw=row(W['gmlp_v_norm_w'][i]), gow=row(W['gmlp_out_norm_w'][i]),
        ws=W['gmlp_w_s'][i], bs=W['gmlp_b_s'][i][:, :, None],
        mu=row(W['rwkv_mu'][i]), w0=row(W['rwkv_w0'][i]), a0=row(W['rwkv_a0'][i]),
        wup=jnp.concatenate([W['rwkv_w_up'][i], zeros_r], 0),
        aup=jnp.concatenate([zeros_r, W['rwkv_a_up'][i]], 0),
        gup=W['rwkv_g_up'][i],
        kk=row(W['rwkv_k_k'][i]), ka=row(W['rwkv_k_a'][i]), rk=row(W['rwkv_r_k'][i]),
        lnw=row(W['rwkv_ln_w'][i]), lnb=row(W['rwkv_ln_b'][i]),
    )


def _mixers(cols, B, T, Tp, lw, conv_prev, ssd_prev, rwkv_prev, shift_prev, t_valid):
    G = GROUP_WIDTH
    cols3 = cols.reshape(B, T, COLS)
    if Tp != T:
        cols3 = jnp.pad(cols3, ((0, 0), (0, Tp - T), (0, 0)))
    y_ssd, ssd_new = _ssd(cols3, _pad_rows_front(conv_prev, 8), ssd_prev.reshape(B, SSD_HEADS * SSD_HEADDIM, SSD_STATE),
                          lw['cw8'], lw['cb'], lw['dtb'], lw['alog'], lw['dskip'], lw['ssd_nw'], t_valid)
    y_gm, v_gm = _gmlp(cols3, lw['gvw'], lw['ws'], lw['bs'], lw['gow'])
    y_rw, rwkv_new = _rwkv(cols3, _pad_rows_front(shift_prev[:, None, :], 8), _pair_block_diag(rwkv_prev),
                           lw['mu'], lw['w0'], lw['wup'], lw['a0'], lw['aup'], lw['gup'], lw['kk'], lw['ka'],
                           lw['rk'], lw['lnw'], lw['lnb'], t_valid)
    crop = lambda y: y[:, :T].reshape(B * T, G)
    raw = cols.reshape(B, T, COLS)
    conv_new = raw[:, T - (SSD_CONV - 1):, C_X:C_X + SSD_CONV_DIM]
    shift_new = raw[:, T - 1, C_RR:C_RR + RWKV_COLS]
    states = (ssd_new.reshape(B, SSD_HEADS, SSD_HEADDIM, SSD_STATE), conv_new, _pair_diag_blocks(rwkv_new), shift_new)
    return crop(y_ssd), crop(y_gm), crop(y_rw), v_gm[:, :T], states


def kernel(x_prompt, x_sample, c_prompt, c_sample, cache_k, cache_v, page_table, state_ssd, state_ssd_conv, state_rwkv, state_rwkv_shift, norm_w, w_ada, b_ada, ffn_w1, ffn_w3, ffn_w2, w_in, w_out, ssd_conv_w, ssd_conv_b, ssd_dt_bias, ssd_a_log, ssd_d, ssd_norm_w, moba_q_norm_w, moba_k_norm_w, moba_out_norm_w, gmlp_v_norm_w, gmlp_w_s, gmlp_b_s, gmlp_out_norm_w, rwkv_mu, rwkv_w0, rwkv_w_up, rwkv_a0, rwkv_a_up, rwkv_g_up, rwkv_k_k, rwkv_k_a, rwkv_r_k, rwkv_ln_w, rwkv_ln_b):
    W = dict(ssd_conv_w=ssd_conv_w, ssd_conv_b=ssd_conv_b, ssd_dt_bias=ssd_dt_bias, ssd_a_log=ssd_a_log,
             ssd_d=ssd_d, ssd_norm_w=ssd_norm_w, moba_q_norm_w=moba_q_norm_w, moba_k_norm_w=moba_k_norm_w,
             moba_out_norm_w=moba_out_norm_w, gmlp_v_norm_w=gmlp_v_norm_w, gmlp_w_s=gmlp_w_s, gmlp_b_s=gmlp_b_s,
             gmlp_out_norm_w=gmlp_out_norm_w, rwkv_mu=rwkv_mu, rwkv_w0=rwkv_w0, rwkv_w_up=rwkv_w_up,
             rwkv_a0=rwkv_a0, rwkv_a_up=rwkv_a_up, rwkv_g_up=rwkv_g_up, rwkv_k_k=rwkv_k_k, rwkv_k_a=rwkv_k_a,
             rwkv_r_k=rwkv_r_k, rwkv_ln_w=rwkv_ln_w, rwkv_ln_b=rwkv_ln_b)
    Bp, Tq, D = x_prompt.shape
    Bs, Ts, _ = x_sample.shape
    n_pages = page_table.shape[1]
    assert n_pages * PAGE_SIZE == PAST_LEN and PAST_LEN % MOBA_BLOCK == 0
    assert PAST_LEN // MOBA_BLOCK >= MOBA_TOPK and Ts <= MOBA_BLOCK
    assert n_pages % (KMEAN_BLOCKS * PAGES_PER_BLOCK) == 0 and RWKV_CHUNK == RWKV_HEAD
    Rp, Rs = Bp * Tq, Bs * Ts

    w1_b, w3_b, w2_b = ffn_w1.astype(BF16), ffn_w3.astype(BF16), ffn_w2.astype(BF16)
    dt_lo = GROUP_WIDTH + SSD_CONV_DIM
    dt_hi = dt_lo + SSD_HEADS
    w_in_b = jnp.concatenate(
        [w_in[:, :, :dt_lo], w_in[:, :, dt_hi:], w_in[:, :, dt_lo:dt_hi],
         jnp.zeros((DEPTH, D, COLS - C_DT - SSD_HEADS), w_in.dtype)], axis=-1).astype(BF16)
    w_out_b = w_out.astype(BF16)
    nw4 = norm_w.reshape(DEPTH, 3, 1, D)
    b_ada3 = b_ada.reshape(DEPTH, 1, N_MOD * D)

    n_c = Bp + Bs
    c_all = jnp.pad(jnp.concatenate([c_prompt, c_sample], 0), ((0, -n_c % 8), (0, 0)))

    pos_p = jnp.arange(Tq, dtype=jnp.int32)
    pos_s = PAST_LEN + jnp.arange(Ts, dtype=jnp.int32)
    cos_p, sin_p = (jnp.tile(t, (Bp, 1)) for t in _rope_tables(pos_p))
    cos_s, sin_s = (jnp.tile(t, (Bs, 1)) for t in _rope_tables(pos_s))

    pt_flat = page_table.reshape(-1)

    zeros = lambda *s: jnp.zeros(s, F32)
    xp = x_prompt.reshape(Rp, D)
    xs = x_sample.reshape(Rs, D)
    TM = 512
    outs_p, outs_s = [], []
    for i in range(DEPTH):
        lw = _layer_weights(W, i)
        mod = _ada(c_all, w_ada, b_ada3, i).reshape(-1, N_MOD, D)
        mod_p = _Mod(mod[:Bp].reshape(Bp, N_MOD, 1, D), False, Tq, TM)
        mod_s = _Mod(jnp.repeat(mod[Bp:n_c], Ts, axis=0).transpose(1, 0, 2), True, Ts, Rs)

        xp = _ffn(xp, mod_p, 0, nw4, w1_b, w3_b, w2_b, i, 0, TM, 512)
        cols = _inproj(xp, mod_p, nw4, w_in_b, i, TM, 1536)
        q_rot, k_rot = _moba_prep(cols, cos_p, sin_p, lw['qw'], lw['kw'], TM)
        y_ssd, y_gm, y_rw, _, st = _mixers(cols, Bp, Tq, Tq, lw, zeros(Bp, SSD_CONV - 1, SSD_CONV_DIM),
                                           zeros(Bp, SSD_HEADS, SSD_HEADDIM, SSD_STATE),
                                           zeros(Bp, RWKV_HEADS, RWKV_HEAD, RWKV_HEAD), zeros(Bp, RWKV_COLS), None)
        y_att = _moba_attn(q_rot.reshape(Bp, Tq, -1), k_rot.reshape(Bp, Tq, -1), cols.reshape(Bp, Tq, COLS),
                           lw['ow']).reshape(Rp, -1)
        xp = _outproj(xp, mod_p, (y_ssd, y_att, y_gm, y_rw), w_out_b, i, TM)
        xp = _ffn(xp, mod_p, 6, nw4, w1_b, w3_b, w2_b, i, 1, TM, 512)
        shp = (Bp, Tq, MOBA_HEADS, MOBA_HEAD_DIM)
        outs_p.append((k_rot.reshape(shp), cols[:, C_MV:C_MV + GROUP_WIDTH].reshape(shp)) + st)

        xs = _ffn(xs, mod_s, 0, nw4, w1_b, w3_b, w2_b, i, 0, Rs, 512)
        cols = _inproj(xs, mod_s, nw4, w_in_b, i, Rs, 1536)
        q_rot, k_rot = _moba_prep(cols, cos_s, sin_s, lw['qw'], lw['kw'], Rs)
        y_ssd, y_gm, y_rw, v_gm, st = _mixers(cols, Bs, Ts, SAMPLE_PAD, lw, state_ssd_conv[i], state_ssd[i],
                                              state_rwkv[i], state_rwkv_shift[i], Ts)
        kmean = _kmean_pages(cache_k, pt_flat, i, Bs, n_pages)
        idx = _select(q_rot, kmean, Bs, Ts)
        o_att = _sample_attn(q_rot, k_rot, cols, cache_k, cache_v, pt_flat, idx.reshape(-1), i, Bs, Ts, n_pages)
        y_att = _rownorm(o_att.reshape(Rs, GROUP_WIDTH), lw['ow'])
        xs = _outproj(xs, mod_s, (y_ssd, y_att, y_gm, y_rw), w_out_b, i, Rs)
        xs = _ffn(xs, mod_s, 6, nw4, w1_b, w3_b, w2_b, i, 1, Rs, 512)
        shp = (Bs, Ts, MOBA_HEADS, MOBA_HEAD_DIM)
        outs_s.append((k_rot.reshape(shp), cols[:, C_MV:C_MV + GROUP_WIDTH].reshape(shp)) + st + (v_gm,))

    k_p, v_p, ssd_p, conv_p, rwkv_p, shift_p = (jnp.stack(s) for s in zip(*outs_p))
    k_s, v_s, ssd_s, conv_s, rwkv_s, shift_s, gmlp_v_s = (jnp.stack(s) for s in zip(*outs_s))
    return (xp.reshape(Bp, Tq, D), xs.reshape(Bs, Ts, D), k_p, v_p, k_s, v_s, ssd_p, ssd_s, conv_p, conv_s,
            rwkv_p, rwkv_s, shift_p, shift_s, gmlp_v_s)
```

```python
import functools
import math

import jax
import jax.numpy as jnp
from jax import lax
from jax.experimental import pallas as pl
from jax.experimental.pallas import tpu as pltpu

F32 = jnp.float32
BF16 = jnp.bfloat16
HI = lax.Precision.HIGHEST

D_MODEL = 2048
DEPTH = 2
PAST_LEN = 16384
PAGE_SIZE = 128
GROUP_WIDTH = 512
SSD_HEADS = 8
SSD_HEADDIM = 64
SSD_STATE = 128
SSD_CONV = 4
SSD_CHUNK = 128
SSD_CONV_DIM = 1024
MOBA_HEADS = 4
MOBA_HEAD_DIM = 128
MOBA_BLOCK = 256
MOBA_TOPK = 3
ROPE_THETA = 10000.0
GMLP_CHUNK = 128
GMLP_HEADS = 4
RWKV_HEADS = 8
RWKV_HEAD = 64
RWKV_COLS = 1792
RWKV_CHUNK = 64
RWKV_SUB = 16
RWKV_SEQS_PER_STEP = 4
RWKV_DECAY_SCALE = 0.606531
RWKV_LN_EPS = 64e-5
D_FF = 5632
N_MOD = 9
FFN_RES = 0.5
EPS = 1e-6
NEG = -1e30

COLS = 6144
C_Z, C_X, C_BC = 0, 512, 1024
C_MQ, C_MK, C_MV = 1536, 2048, 2560
C_GU, C_GV = 3072, 3584
C_RR, C_RK, C_RV, C_RL = 4096, 4608, 5120, 5632
C_DT = 5888

LANE = 128
SAMPLE_PAD = 128
VMEM_LIMIT = 56 * 1024 * 1024


def _params(n_axes, vmem=VMEM_LIMIT):
    return pltpu.CompilerParams(dimension_semantics=("arbitrary",) * n_axes, vmem_limit_bytes=vmem)


def _dot(a, b, precision=None):
    return jnp.dot(a, b, preferred_element_type=F32, precision=precision)


def _dot_t(a, b, precision=None):
    return lax.dot_general(a, b, (((1,), (1,)), ((), ())), preferred_element_type=F32, precision=precision)


def _dot_0(a, b, precision=None):
    return lax.dot_general(a, b, (((0,), (0,)), ((), ())), preferred_element_type=F32, precision=precision)


def _iota(shape, dim):
    return lax.broadcasted_iota(jnp.int32, shape, dim)


def _rms(x, w):
    return x * lax.rsqrt(jnp.mean(x * x, -1, keepdims=True) + EPS) * w


def _silu(x):
    return x * jax.nn.sigmoid(x)


def _ada_kernel(c_ref, w_ref, b_ref, o_ref):
    s = _silu(c_ref[...]).astype(BF16)
    o_ref[...] = _dot(s, w_ref[...].astype(BF16)) + b_ref[...]


def _ada(c_all, w_ada, b_ada3, layer):
    rows = c_all.shape[0]
    n_out = w_ada.shape[-1]
    tn = 1024
    return pl.pallas_call(
        _ada_kernel,
        grid=(n_out // tn,),
        in_specs=[
            pl.BlockSpec((rows, D_MODEL), lambda n: (0, 0)),
            pl.BlockSpec((None, D_MODEL, tn), lambda n: (layer, 0, n)),
            pl.BlockSpec((None, 1, tn), lambda n: (layer, 0, n)),
        ],
        out_specs=pl.BlockSpec((rows, tn), lambda n: (0, n)),
        out_shape=jax.ShapeDtypeStruct((rows, n_out), F32),
        compiler_params=_params(1),
        name="ada",
    )(c_all, w_ada, b_ada3)


class _Mod:
    def __init__(self, arr, per_row, rows_per_batch, tm):
        self.arr = arr
        self.per_row = per_row
        self.tiles_per_batch = None if per_row else rows_per_batch // tm
        self.tm = tm

    def spec(self, j):
        if self.per_row:
            return pl.BlockSpec((None, self.tm, D_MODEL), lambda r, *_: (j, r, 0))
        tpb = self.tiles_per_batch
        return pl.BlockSpec((None, None, 1, D_MODEL), lambda r, *_: (r // tpb, j, 0, 0))


def _ffn_kernel(x_ref, nw_ref, sh_ref, sc_ref, g_ref, w1_ref, w3_ref, w2_ref, o_ref, h_ref, acc_ref, *, nf):
    f = pl.program_id(1)

    @pl.when(f == 0)
    def _():
        xn = _rms(x_ref[...], nw_ref[...])
        h_ref[...] = (xn * (1 + sc_ref[...]) + sh_ref[...]).astype(BF16)
        acc_ref[...] = jnp.zeros_like(acc_ref)

    h = h_ref[...]
    a = _dot(h, w1_ref[...])
    b = _dot(h, w3_ref[...])
    acc_ref[...] += _dot((_silu(a) * b).astype(BF16), w2_ref[...])

    @pl.when(f == nf - 1)
    def _():
        o_ref[...] = x_ref[...] + FFN_RES * g_ref[...] * acc_ref[...]


def _ffn(x, mod, j0, nw4, w1, w3, w2, layer, slot, tm, tf):
    rows = x.shape[0]
    nf = D_FF // tf
    return pl.pallas_call(
        functools.partial(_ffn_kernel, nf=nf),
        grid=(rows // tm, nf),
        in_specs=[
            pl.BlockSpec((tm, D_MODEL), lambda r, f: (r, 0)),
            pl.BlockSpec((None, None, 1, D_MODEL), lambda r, f: (layer, 2 * slot, 0, 0)),
            mod.spec(j0), mod.spec(j0 + 1), mod.spec(j0 + 2),
            pl.BlockSpec((None, None, D_MODEL, tf), lambda r, f: (layer, slot, 0, f)),
            pl.BlockSpec((None, None, D_MODEL, tf), lambda r, f: (layer, slot, 0, f)),
            pl.BlockSpec((None, None, tf, D_MODEL), lambda r, f: (layer, slot, f, 0)),
        ],
        out_specs=pl.BlockSpec((tm, D_MODEL), lambda r, f: (r, 0)),
        out_shape=jax.ShapeDtypeStruct((rows, D_MODEL), F32),
        scratch_shapes=[pltpu.VMEM((tm, D_MODEL), BF16), pltpu.VMEM((tm, D_MODEL), F32)],
        compiler_params=_params(2),
        name="ffn",
    )(x, nw4, mod.arr, mod.arr, mod.arr, w1, w3, w2)


def _inproj_kernel(x_ref, nw_ref, sh_ref, sc_ref, w_ref, o_ref, h_ref):
    @pl.when(pl.program_id(1) == 0)
    def _():
        xn = _rms(x_ref[...], nw_ref[...])
        h_ref[...] = (xn * (1 + sc_ref[...]) + sh_ref[...]).astype(BF16)

    o_ref[...] = _dot(h_ref[...], w_ref[...])


def _inproj(x, mod, nw4, w_in, layer, tm, tn):
    rows = x.shape[0]
    return pl.pallas_call(
        _inproj_kernel,
        grid=(rows // tm, COLS // tn),
        in_specs=[
            pl.BlockSpec((tm, D_MODEL), lambda r, n: (r, 0)),
            pl.BlockSpec((None, None, 1, D_MODEL), lambda r, n: (layer, 1, 0, 0)),
            mod.spec(3), mod.spec(4),
            pl.BlockSpec((None, D_MODEL, tn), lambda r, n: (layer, 0, n)),
        ],
        out_specs=pl.BlockSpec((tm, tn), lambda r, n: (r, n)),
        out_shape=jax.ShapeDtypeStruct((rows, COLS), F32),
        scratch_shapes=[pltpu.VMEM((tm, D_MODEL), BF16)],
        compiler_params=_params(2),
        name="inproj",
    )(x, nw4, mod.arr, mod.arr, w_in)


def _outproj_kernel(x_ref, g_ref, y0_ref, y1_ref, y2_ref, y3_ref, w_ref, o_ref):
    G = GROUP_WIDTH
    acc = _dot(y0_ref[...], w_ref[0:G, :])
    acc += _dot(y1_ref[...], w_ref[G:2 * G, :])
    acc += _dot(y2_ref[...], w_ref[2 * G:3 * G, :])
    acc += _dot(y3_ref[...], w_ref[3 * G:4 * G, :])
    o_ref[...] = x_ref[...] + g_ref[...] * acc


def _outproj(x, mod, ys, w_out, layer, tm):
    rows = x.shape[0]
    yspec = pl.BlockSpec((tm, GROUP_WIDTH), lambda r: (r, 0))
    return pl.pallas_call(
        _outproj_kernel,
        grid=(rows // tm,),
        in_specs=[
            pl.BlockSpec((tm, D_MODEL), lambda r: (r, 0)),
            mod.spec(5), yspec, yspec, yspec, yspec,
            pl.BlockSpec((None, D_MODEL, D_MODEL), lambda r: (layer, 0, 0)),
        ],
        out_specs=pl.BlockSpec((tm, D_MODEL), lambda r: (r, 0)),
        out_shape=jax.ShapeDtypeStruct((rows, D_MODEL), F32),
        compiler_params=_params(1),
        name="outproj",
    )(x, mod.arr, *ys, w_out)


def _softplus(x):
    return jnp.maximum(x, 0.0) + jnp.log1p(jnp.exp(-jnp.abs(x)))


def _ssd_kernel(z_ref, x_ref, bc_ref, dt_ref, prev_ref, cw_ref, cb_ref, dtb_ref, alog_ref, dskip_ref, nw_ref,
                h0_ref, y_ref, hout_ref, ext_ref, st_ref, *, C, nc, t_valid):
    c = pl.program_id(1)
    G = GROUP_WIDTH

    @pl.when(c == 0)
    def _():
        ext_ref[0:8, :] = prev_ref[...]
        st_ref[...] = h0_ref[...]

    ext_ref[8:8 + C, 0:G] = x_ref[...]
    ext_ref[8:8 + C, G:2 * G] = bc_ref[...]
    conv = cb_ref[...] + ext_ref[5:5 + C, :] * cw_ref[0:1, :]
    for i in range(1, SSD_CONV):
        conv = conv + ext_ref[5 + i:5 + i + C, :] * cw_ref[i:i + 1, :]
    ext_ref[0:8, :] = ext_ref[C:C + 8, :]
    xbc = _silu(conv)
    xs = xbc[:, 0:G]

    dt = _softplus(dt_ref[...] + dtb_ref[...])
    if t_valid is not None:
        dt = jnp.where(c * C + _iota((C, LANE), 0) < t_valid, dt, 0.0)
    a = dt * (-jnp.exp(alog_ref[...]))
    tri = _iota((C, C), 0) >= _iota((C, C), 1)
    tri_b = tri.astype(BF16)
    a_hi, a_mid, a_lo = _split3(a)
    a_cs = _dg(tri_b, a_hi, _NN) + (_dg(tri_b, a_mid, _NN) + _dg(tri_b, a_lo, _NN))
    a_cs_t = a_cs.T
    lane_lo = _iota((C, LANE), 1) < SSD_HEADDIM
    row_lo = _iota((LANE, 1), 0) < SSD_HEADDIM
    col = lambda h: a_cs[:, h:h + 1]
    row = lambda h: a_cs_t[h:h + 1, :]
    last = lambda h: a_cs[C - 1:C, h:h + 1]
    by_head = lambda p, f: jnp.where(lane_lo, f(2 * p), f(2 * p + 1))
    decay_mat = lambda h: jnp.exp(jnp.where(tri, col(h) - row(h), -jnp.inf))

    n_groups = 2
    heads_per_group = SSD_HEADS // n_groups
    bm_s = [_split2(xbc[:, G + g * SSD_STATE:G + (g + 1) * SSD_STATE]) for g in range(n_groups)]
    cm_s = [_split2(xbc[:, G + (n_groups + g) * SSD_STATE:G + (n_groups + g + 1) * SSD_STATE]) for g in range(n_groups)]
    scores = [_dot3(cm_s[g], bm_s[g], _NT) for g in range(n_groups)]
    group = lambda p: 2 * p // heads_per_group
    each = lambda f: [f(p) for p in range(SSD_HEADS // 2)]
    m0_s = each(lambda p: _split2(scores[group(p)] * decay_mat(2 * p)))
    m1_s = each(lambda p: _split2(scores[group(p)] * decay_mat(2 * p + 1)))
    xs_p = each(lambda p: xs[:, p * LANE:(p + 1) * LANE])
    xdt = each(lambda p: xs_p[p] * by_head(p, lambda h: dt[:, h:h + 1]))
    xdt_s = each(lambda p: _split2(xdt[p]))
    y_diag = each(lambda p: jnp.where(lane_lo, _dot3(m0_s[p], xdt_s[p]), _dot3(m1_s[p], xdt_s[p])))
    st = each(lambda p: st_ref[p * LANE:(p + 1) * LANE, :])
    y_off = each(lambda p: _dot3(cm_s[group(p)], _split2(st[p]), _NT) * by_head(p, lambda h: jnp.exp(col(h))))
    decay = each(lambda p: by_head(p, lambda h: jnp.exp(last(h) - col(h))))
    new = each(lambda p: _dot3(_split2(xdt[p] * decay[p]), bm_s[group(p)], _TN))
    for p in range(SSD_HEADS // 2):
        keep = jnp.where(row_lo, jnp.exp(last(2 * p)), jnp.exp(last(2 * p + 1)))
        st_ref[p * LANE:(p + 1) * LANE, :] = st[p] * keep + new[p]
    ys = each(lambda p: y_diag[p] + y_off[p] + xs_p[p] * dskip_ref[:, p * LANE:(p + 1) * LANE])

    y = jnp.concatenate(ys, axis=1) * _silu(z_ref[...])
    y_ref[...] = _rms(y, nw_ref[...]).astype(y_ref.dtype)

    @pl.when(c == nc - 1)
    def _():
        hout_ref[...] = st_ref[...]


def _ssd(cols3, prev8, h0, cw8, cb, dtb, alog, dskip, nw, t_valid):
    B, T, _ = cols3.shape
    C = SSD_CHUNK
    nc = T // C
    G = GROUP_WIDTH
    vec = lambda n: pl.BlockSpec((1, n), lambda b, c: (0, 0))
    return pl.pallas_call(
        functools.partial(_ssd_kernel, C=C, nc=nc, t_valid=t_valid),
        grid=(B, nc),
        in_specs=[
            pl.BlockSpec((None, C, G), lambda b, c: (b, c, C_Z // G)),
            pl.BlockSpec((None, C, G), lambda b, c: (b, c, C_X // G)),
            pl.BlockSpec((None, C, G), lambda b, c: (b, c, C_BC // G)),
            pl.BlockSpec((None, C, LANE), lambda b, c: (b, c, C_DT // LANE)),
            pl.BlockSpec((None, 8, SSD_CONV_DIM), lambda b, c: (b, 0, 0)),
            pl.BlockSpec((8, SSD_CONV_DIM), lambda b, c: (0, 0)),
            vec(SSD_CONV_DIM), vec(LANE), vec(LANE), vec(G), vec(G),
            pl.BlockSpec((None, SSD_HEADS * SSD_HEADDIM, SSD_STATE), lambda b, c: (b, 0, 0)),
        ],
        out_specs=[
            pl.BlockSpec((None, C, G), lambda b, c: (b, c, 0)),
            pl.BlockSpec((None, SSD_HEADS * SSD_HEADDIM, SSD_STATE), lambda b, c: (b, 0, 0)),
        ],
        out_shape=[
            jax.ShapeDtypeStruct((B, T, G), BF16),
            jax.ShapeDtypeStruct((B, SSD_HEADS * SSD_HEADDIM, SSD_STATE), F32),
        ],
        scratch_shapes=[pltpu.VMEM((C + 8, SSD_CONV_DIM), F32), pltpu.VMEM((SSD_HEADS * SSD_HEADDIM, SSD_STATE), F32)],
        compiler_params=_params(2),
        name="ssd",
    )(cols3, cols3, cols3, cols3, prev8, cw8, cb, dtb, alog, dskip, nw, h0)


def _gmlp_kernel(u_ref, v_ref, vw_ref, ws_ref, bs_ref, ow_ref, y_ref, vout_ref, *, C):
    u = jax.nn.gelu(u_ref[...])
    v = _rms(jax.nn.gelu(v_ref[...]), vw_ref[...])
    vout_ref[...] = v
    tri = _iota((C, C), 0) >= _iota((C, C), 1)
    mixed = []
    for h in range(GMLP_HEADS):
        ws = jnp.where(tri, ws_ref[h], 0.0)
        mixed.append(_dot3(_split2(ws), _split2(v[:, h * LANE:(h + 1) * LANE])) + bs_ref[h])
    y = u * jnp.concatenate(mixed, axis=1)
    y_ref[...] = _rms(y, ow_ref[...]).astype(y_ref.dtype)


def _gmlp(cols3, vw, ws, bs, ow):
    B, T, _ = cols3.shape
    C = GMLP_CHUNK
    G = GROUP_WIDTH
    vec = pl.BlockSpec((1, G), lambda b, c: (0, 0))
    return pl.pallas_call(
        functools.partial(_gmlp_kernel, C=C),
        grid=(B, T // C),
        in_specs=[
            pl.BlockSpec((None, C, G), lambda b, c: (b, c, C_GU // G)),
            pl.BlockSpec((None, C, G), lambda b, c: (b, c, C_GV // G)),
            vec,
            pl.BlockSpec((GMLP_HEADS, C, C), lambda b, c: (0, 0, 0)),
            pl.BlockSpec((GMLP_HEADS, C, 1), lambda b, c: (0, 0, 0)),
            vec,
        ],
        out_specs=[pl.BlockSpec((None, C, G), lambda b, c: (b, c, 0))] * 2,
        out_shape=[jax.ShapeDtypeStruct((B, T, G), BF16), jax.ShapeDtypeStruct((B, T, G), F32)],
        compiler_params=_params(2),
        name="gmlp",
    )(cols3, cols3, vw, ws, bs, ow)


_NN = ((1,), (0,))
_NT = ((1,), (1,))
_TN = ((0,), (0,))


def _split2(x):
    hi = x.astype(BF16)
    return hi, (x - hi.astype(F32)).astype(BF16)


def _split3(x):
    hi = x.astype(BF16)
    rest = x - hi.astype(F32)
    mid = rest.astype(BF16)
    return hi, mid, (rest - mid.astype(F32)).astype(BF16)


def _dg(a, b, dims):
    return lax.dot_general(a, b, (dims, ((), ())), preferred_element_type=F32)


def _dot3(a, b, dims=_NN):
    (ah, al), (bh, bl) = a, b
    return _dg(ah, bh, dims) + (_dg(ah, bl, dims) + _dg(al, bh, dims))


def _block_diag(pieces, same_head):
    return tuple(jnp.where(same_head, jnp.concatenate([x, x], axis=0), jnp.zeros((), x.dtype)) for x in pieces)


def _rwkv_kernel(r_ref, k_ref, v_ref, lo_ref, prev_ref, mu_ref, w0_ref, wup_ref, a0_ref, aup_ref, gup_ref,
                 kk_ref, ka_ref, rk_ref, lnw_ref, lnb_ref, s0_ref, y_ref, sout_ref, ext_ref, st_ref,
                 *, C, nc, t_valid):
    c = pl.program_id(1)
    G = GROUP_WIDTH
    N = RWKV_HEAD
    P = 2 * N

    n_batch = r_ref.shape[0]
    n_pairs = G // P
    sls = [slice(p * P, (p + 1) * P) for p in range(n_pairs)]
    same_head_n = (_iota((P, P), 0) < N) == (_iota((P, P), 1) < N)
    ones_bd = same_head_n.astype(BF16)
    tri_cc = (_iota((C, C), 0) >= _iota((C, C), 1)).astype(BF16)

    @pl.when(c == 0)
    def _():
        ext_ref[:, 0:8, :] = prev_ref[...]
        st_ref[...] = s0_ref[...]

    def head_sum(x):
        def pair_sum(p):
            hi, lo = _split2(x[:, sls[p]])
            return _dg(hi, ones_bd, _NN) + _dg(lo, ones_bd, _NN)
        return jnp.concatenate([pair_sum(p) for p in range(n_pairs)], axis=1)

    def chunk_inputs(n):
        ext_ref[n, 8:8 + C, 0:G] = r_ref[n]
        ext_ref[n, 8:8 + C, G:2 * G] = k_ref[n]
        ext_ref[n, 8:8 + C, 2 * G:3 * G] = v_ref[n]
        ext_ref[n, 8:8 + C, 3 * G:RWKV_COLS] = lo_ref[n]
        cur = ext_ref[n, 8:8 + C, :]
        prev = ext_ref[n, 7:7 + C, :]
        xs = cur + (prev - cur) * mu_ref[...]
        ext_ref[n, 0:8, :] = ext_ref[n, C:C + 8, :]

        r, k, v = xs[:, 0:G], xs[:, G:2 * G], xs[:, 2 * G:3 * G]
        la = xs[:, 3 * G:3 * G + P]
        gl = xs[:, 3 * G + P:RWKV_COLS]
        w_log = -RWKV_DECAY_SCALE * jax.nn.sigmoid(
            w0_ref[...] + _dot3(_split2(jnp.tanh(la)), _split2(wup_ref[...])))
        a = jax.nn.sigmoid(a0_ref[...] + _dot3(_split2(la), _split2(aup_ref[...])))
        g = _dot3(_split2(jax.nn.sigmoid(gl)), _split2(gup_ref[...]))

        kk = k * kk_ref[...]
        kk = kk * lax.rsqrt(jnp.maximum(head_sum(kk * kk), 1e-12))
        k2 = k * (1 + (a - 1) * ka_ref[...])
        if t_valid is not None:
            ok = c * C + _iota((C, G), 0) < t_valid
            w_log = jnp.where(ok, w_log, 0.0)
            kk = jnp.where(ok, kk, 0.0)
            k2 = jnp.where(ok, k2, 0.0)
        b = kk * a

        w_hi, w_mid, w_lo = _split3(w_log)
        cl = _dg(tri_cc, w_hi, _NN) + (_dg(tri_cc, w_mid, _NN) + _dg(tri_cc, w_lo, _NN))
        cl_last = cl[C - 1:C, :]
        einv = jnp.exp(-cl)
        e_c = jnp.exp(cl_last - cl)
        return dict(r=r, v=v, k2=k2, g=g, kkp=kk * jnp.exp(cl - w_log), rp=r * jnp.exp(cl), bi=b * einv,
                    ki=k2 * einv, bt=b * e_c, kt=k2 * e_c, p_c=jnp.exp(cl_last))

    seqs = [chunk_inputs(n) for n in range(n_batch)]
    units = [(n, p) for n in range(n_batch) for p in range(n_pairs)]
    each = lambda f: [f(u) for u in range(len(units))]
    tile = lambda name: each(lambda u: seqs[units[u][0]][name][:, sls[units[u][1]]])
    kkp, rp, bi, ki, bt, kt, v_p = (tile(name) for name in ("kkp", "rp", "bi", "ki", "bt", "kt", "v"))

    t_i = _iota((C, P), 0)
    s_i = _iota((C, P), 1) % C
    strict = s_i < t_i
    incl = s_i <= t_i
    diag_blk = (s_i // RWKV_SUB) == (t_i // RWKV_SUB)
    eye = (s_i == t_i).astype(F32)
    same_head = (_iota((P, P), 0) < C) == (_iota((P, P), 1) < C)

    memo = {}

    def sp(x):
        if id(x) not in memo:
            memo[id(x)] = (x, _split2(x))
        return memo[id(x)][1]

    def bd(x):
        if ("bd", id(x)) not in memo:
            memo["bd", id(x)] = (x, _block_diag(sp(x), same_head))
        return memo["bd", id(x)][1]

    def mm(xs, ys):
        return each(lambda p: _dot3(sp(xs[p]), bd(ys[p])))

    kkp_s = each(lambda p: _split2(kkp[p]))
    rp_s = each(lambda p: _split2(rp[p]))
    lhs_s = each(lambda p: tuple(jnp.concatenate([x, y], axis=0) for x, y in zip(kkp_s[p], rp_s[p])))
    ab = each(lambda p: _dot3(lhs_s[p], _block_diag(_split2(bi[p]), same_head_n), _NT))
    ak = each(lambda p: _dot3(lhs_s[p], _block_diag(_split2(ki[p]), same_head_n), _NT))
    a_m = each(lambda p: jnp.where(strict, ab[p][0:C], 0.0))
    b_k = each(lambda p: jnp.where(strict, ak[p][0:C], 0.0))
    r_b = each(lambda p: jnp.where(incl, ab[p][C:2 * C], 0.0))
    r_k = each(lambda p: jnp.where(incl, ak[p][C:2 * C], 0.0))

    n_pow = each(lambda p: jnp.where(diag_blk, -a_m[p], 0.0))
    a_o = each(lambda p: jnp.where(diag_blk, 0.0, a_m[p]))
    t_d = each(lambda p: eye + n_pow[p])
    for _ in range(int(math.log2(RWKV_SUB)) - 1):
        n_pow = mm(n_pow, n_pow)
        step = mm(t_d, n_pow)
        t_d = each(lambda p: t_d[p] + step[p])
    m1 = mm(t_d, a_o)
    m2 = mm(m1, m1)
    im = each(lambda p: eye - m1[p])
    im_m2 = mm(im, m2)
    t_full = mm(each(lambda p: im[p] + im_m2[p]), t_d)

    st = each(lambda u: st_ref[units[u]])
    st_s = each(lambda p: _split2(st[p]))
    bkv = mm(b_k, v_p)
    rhs = each(lambda p: _dot3(kkp_s[p], st_s[p], _NT) + bkv[p])
    u = mm(t_full, rhs)
    rkv = mm(r_k, v_p)
    rbu = mm(r_b, u)
    ys = each(lambda p: _dot3(rp_s[p], st_s[p], _NT) + rkv[p] - rbu[p])
    upd = each(lambda p: _dot3(_split2(jnp.concatenate([v_p[p], -u[p]], axis=0)),
                               _split2(jnp.concatenate([kt[p], bt[p]], axis=0)), _TN))
    for u_i, (n, p) in enumerate(units):
        st_ref[n, p] = st[u_i] * seqs[n]["p_c"][:, sls[p]] + jnp.where(same_head_n, upd[u_i], 0.0)

    for n, seq in enumerate(seqs):
        y = jnp.concatenate(ys[n * n_pairs:(n + 1) * n_pairs], axis=1)
        mean = head_sum(y) * (1.0 / N)
        d = y - mean
        var = head_sum(d * d) * (1.0 / N)
        yn = d * lax.rsqrt(var + RWKV_LN_EPS) * lnw_ref[...] + lnb_ref[...]
        bonus = head_sum(seq["r"] * seq["k2"] * rk_ref[...]) * seq["v"]
        y_ref[n] = ((yn + bonus) * seq["g"]).astype(y_ref.dtype)

    @pl.when(c == nc - 1)
    def _():
        sout_ref[...] = st_ref[...]


def _rwkv(cols3, prev8, s0_bd, mu, w0, wup, a0, aup, gup, kk, ka, rk, lnw, lnb, t_valid):
    B, T, _ = cols3.shape
    C = RWKV_CHUNK
    nc = T // C
    G = GROUP_WIDTH
    P = 2 * RWKV_HEAD
    n_pairs = RWKV_HEADS // 2
    S = RWKV_SEQS_PER_STEP
    vec = lambda n: pl.BlockSpec((1, n), lambda b, c: (0, 0))
    mat = lambda m, n: pl.BlockSpec((m, n), lambda b, c: (0, 0))
    return pl.pallas_call(
        functools.partial(_rwkv_kernel, C=C, nc=nc, t_valid=t_valid),
        grid=(B // S, nc),
        in_specs=[
            pl.BlockSpec((S, C, G), lambda b, c: (b, c, C_RR // G)),
            pl.BlockSpec((S, C, G), lambda b, c: (b, c, C_RK // G)),
            pl.BlockSpec((S, C, G), lambda b, c: (b, c, C_RV // G)),
            pl.BlockSpec((S, C, 2 * P), lambda b, c: (b, c, C_RL // (2 * P))),
            pl.BlockSpec((S, 8, RWKV_COLS), lambda b, c: (b, 0, 0)),
            vec(RWKV_COLS), vec(G), mat(P, G), vec(G), mat(P, G), mat(P, G),
            vec(G), vec(G), vec(G), vec(G), vec(G),
            pl.BlockSpec((S, n_pairs, P, P), lambda b, c: (b, 0, 0, 0)),
        ],
        out_specs=[
            pl.BlockSpec((S, C, G), lambda b, c: (b, c, 0)),
            pl.BlockSpec((S, n_pairs, P, P), lambda b, c: (b, 0, 0, 0)),
        ],
        out_shape=[
            jax.ShapeDtypeStruct((B, T, G), BF16),
            jax.ShapeDtypeStruct((B, n_pairs, P, P), F32),
        ],
        scratch_shapes=[pltpu.VMEM((S, C + 8, RWKV_COLS), F32), pltpu.VMEM((S, n_pairs, P, P), F32)],
        compiler_params=_params(2),
        name="rwkv",
    )(cols3, cols3, cols3, cols3, prev8, mu, w0, wup, a0, aup, gup, kk, ka, rk, lnw, lnb, s0_bd)


def _moba_prep_kernel(q_ref, k_ref, cos_ref, sin_ref, qw_ref, kw_ref, qo_ref, ko_ref):
    cos, sin = cos_ref[...], sin_ref[...]
    for h in range(MOBA_HEADS):
        sl = slice(h * LANE, (h + 1) * LANE)
        for src, w_ref, dst in ((q_ref, qw_ref, qo_ref), (k_ref, kw_ref, ko_ref)):
            xn = _rms(src[:, sl], w_ref[...])
            dst[:, sl] = xn * cos + pltpu.roll(xn, MOBA_HEAD_DIM // 2, 1) * sin


def _moba_prep(cols, cos, sin, qw, kw, tm):
    rows = cols.shape[0]
    G = GROUP_WIDTH
    tab = pl.BlockSpec((tm, LANE), lambda r: (r, 0))
    vec = pl.BlockSpec((1, LANE), lambda r: (0, 0))
    return pl.pallas_call(
        _moba_prep_kernel,
        grid=(rows // tm,),
        in_specs=[
            pl.BlockSpec((tm, G), lambda r: (r, C_MQ // G)),
            pl.BlockSpec((tm, G), lambda r: (r, C_MK // G)),
            tab, tab, vec, vec,
        ],
        out_specs=[pl.BlockSpec((tm, G), lambda r: (r, 0))] * 2,
        out_shape=[jax.ShapeDtypeStruct((rows, G), F32)] * 2,
        compiler_params=_params(1),
        name="moba_prep",
    )(cols, cols, cos, sin, qw, kw)


def _top3(gate, lane_f):
    sel = jnp.zeros(gate.shape, jnp.bool_)
    g = gate
    big = float(gate.shape[-1])
    idxs = []
    for _ in range(MOBA_TOPK):
        m = jnp.max(g, axis=-1, keepdims=True)
        idx = jnp.min(jnp.where(g == m, lane_f, big), axis=-1, keepdims=True)
        pick = lane_f == idx
        sel = sel | pick
        g = jnp.where(pick, -jnp.inf, g)
        idxs.append(idx)
    return sel, idxs


def _moba_attn_kernel(q_ref, k_ref, v_ref, ow_ref, y_ref, km_ref, kb_ref, vt_ref, sel_ref, o_ref, *, nb):
    qi = pl.program_id(1)
    BLK = MOBA_BLOCK
    scale = MOBA_HEAD_DIM ** -0.5

    @pl.when(qi == 0)
    def _():
        for j in range(nb):
            rows = slice(j * BLK, (j + 1) * BLK)
            k_blk = k_ref[rows, :]
            km_ref[j:j + 1, :] = jnp.mean(k_blk, axis=0, keepdims=True)
            kb_ref[rows, :] = k_blk.astype(BF16)
            vt_ref[:, rows] = v_ref[rows, :].T.astype(BF16)

    blk_i = _iota((nb, BLK), 0)
    blk_f = blk_i.astype(F32)
    past = blk_i < qi
    causal = _iota((BLK, BLK), 0) <= _iota((BLK, BLK), 1)
    own = pl.multiple_of(qi * BLK, BLK)

    for h in range(MOBA_HEADS):
        sl = slice(h * LANE, (h + 1) * LANE)
        q_t = q_ref[:, sl].T
        gate = jnp.where(past, _dot3(_split2(km_ref[:, sl]), _split2(q_t)), NEG)
        sel = jnp.zeros(gate.shape, jnp.bool_)
        for _ in range(MOBA_TOPK):
            best = jnp.max(gate, axis=0, keepdims=True)
            idx = jnp.min(jnp.where(gate == best, blk_f, float(nb)), axis=0, keepdims=True)
            pick = blk_f == idx
            sel = sel | pick
            gate = jnp.where(pick, -jnp.inf, gate)
        sel_ref[...] = (sel & past).astype(F32)
        qb_t = q_t.astype(BF16)

        s = _dot(kb_ref[pl.ds(own, BLK), sl], qb_t) * scale
        s = jnp.where(causal, s, NEG)
        m0 = jnp.max(s, axis=0, keepdims=True)
        p0 = jnp.exp(s - m0)
        l0 = jnp.sum(p0, axis=0, keepdims=True)
        acc0 = _dot(vt_ref[sl, pl.ds(own, BLK)], p0.astype(BF16))

        def body(j, carry):
            m_i, l_i, acc = carry
            start = pl.multiple_of(j * BLK, BLK)
            s = _dot(kb_ref[pl.ds(start, BLK), sl], qb_t) * scale
            s = jnp.where(sel_ref[pl.ds(j, 1), :] > 0.0, s, NEG)
            m_n = jnp.maximum(m_i, jnp.max(s, axis=0, keepdims=True))
            alpha = jnp.exp(m_i - m_n)
            p = jnp.exp(s - m_n)
            l_n = alpha * l_i + jnp.sum(p, axis=0, keepdims=True)
            acc = alpha * acc + _dot(vt_ref[sl, pl.ds(start, BLK)], p.astype(BF16))
            return m_n, l_n, acc

        _, l_f, acc_f = lax.fori_loop(0, qi, body, (m0, l0, acc0))
        o_ref[:, sl] = (acc_f / l_f).T

    y_ref[...] = _rms(o_ref[...], ow_ref[...]).astype(y_ref.dtype)


def _moba_attn(q3, k3, cols3, ow):
    B, T, G = q3.shape
    BLK = MOBA_BLOCK
    nb = T // BLK
    return pl.pallas_call(
        functools.partial(_moba_attn_kernel, nb=nb),
        grid=(B, nb),
        in_specs=[
            pl.BlockSpec((None, BLK, G), lambda b, i: (b, i, 0)),
            pl.BlockSpec((None, T, G), lambda b, i: (b, 0, 0)),
            pl.BlockSpec((None, T, G), lambda b, i: (b, 0, C_MV // G)),
            pl.BlockSpec((1, G), lambda b, i: (0, 0)),
        ],
        out_specs=pl.BlockSpec((None, BLK, G), lambda b, i: (b, i, 0)),
        out_shape=jax.ShapeDtypeStruct((B, T, G), BF16),
        scratch_shapes=[pltpu.VMEM((nb, G), F32), pltpu.VMEM((T, G), BF16), pltpu.VMEM((G, T), BF16),
                        pltpu.VMEM((nb, BLK), F32), pltpu.VMEM((BLK, G), F32)],
        compiler_params=_params(2),
        name="moba_attn",
    )(q3, k3, cols3, ow)


PAGES_PER_BLOCK = MOBA_BLOCK // PAGE_SIZE
KMEAN_BLOCKS = 8


def _kmean_kernel(pt_ref, *refs):
    page_refs, o_ref = refs[:-1], refs[-1]
    for j in range(KMEAN_BLOCKS):
        s = jnp.sum(page_refs[PAGES_PER_BLOCK * j][...], axis=0)
        for o in range(1, PAGES_PER_BLOCK):
            s = s + jnp.sum(page_refs[PAGES_PER_BLOCK * j + o][...], axis=0)
        o_ref[j] = s * (1.0 / MOBA_BLOCK)


def _kmean_pages(cache_k, page_table_flat, layer, batch, n_pages):
    nbk = n_pages // PAGES_PER_BLOCK
    per_step = KMEAN_BLOCKS * PAGES_PER_BLOCK

    def page(o):
        return pl.BlockSpec((None, None, PAGE_SIZE, MOBA_HEADS, MOBA_HEAD_DIM),
                            lambda b, j, pt: (layer, pt[b * n_pages + j * per_step + o], 0, 0, 0))

    return pl.pallas_call(
        _kmean_kernel,
        grid_spec=pltpu.PrefetchScalarGridSpec(
            num_scalar_prefetch=1,
            grid=(batch, nbk // KMEAN_BLOCKS),
            in_specs=[page(o) for o in range(per_step)],
            out_specs=pl.BlockSpec((None, KMEAN_BLOCKS, MOBA_HEADS, MOBA_HEAD_DIM), lambda b, j, pt: (b, j, 0, 0)),
        ),
        out_shape=jax.ShapeDtypeStruct((batch, nbk, MOBA_HEADS, MOBA_HEAD_DIM), F32),
        compiler_params=_params(2),
        name="moba_kmean_pages",
    )(page_table_flat, *([cache_k] * per_step))


def _select_kernel(q_ref, km_ref, idx_ref, *, batch, t_new, nbk):
    rows = batch * t_new
    lane_i = _iota((rows, LANE), 1)
    lane_f = lane_i.astype(F32)
    row_b = _iota((rows, LANE), 0) // t_new
    out = jnp.zeros((rows, LANE), F32)
    pad = jnp.zeros((LANE - nbk, LANE), F32)
    for h in range(MOBA_HEADS):
        sl = slice(h * LANE, (h + 1) * LANE)
        q = q_ref[:, sl]
        gate = jnp.zeros((rows, LANE), F32)
        for b in range(batch):
            km = jnp.concatenate([km_ref[b, :, h, :], pad], axis=0)
            gate = jnp.where(row_b == b, _dot_t(q, km, HI), gate)
        gate = jnp.where(lane_i < nbk, gate, NEG)
        _, idxs = _top3(gate, lane_f)
        for kth, idx in enumerate(idxs):
            out = jnp.where(lane_i == h * MOBA_HEADS + kth, idx, out)
    idx_ref[...] = out.astype(jnp.int32)


def _select(q_rot, kmean, batch, t_new):
    rows = batch * t_new
    nbk = kmean.shape[1]
    G = GROUP_WIDTH
    return pl.pallas_call(
        functools.partial(_select_kernel, batch=batch, t_new=t_new, nbk=nbk),
        grid=(1,),
        in_specs=[
            pl.BlockSpec((rows, G), lambda i: (0, 0)),
            pl.BlockSpec((batch, nbk, MOBA_HEADS, MOBA_HEAD_DIM), lambda i: (0, 0, 0, 0)),
        ],
        out_specs=pl.BlockSpec((rows, LANE), lambda i: (0, 0)),
        out_shape=jax.ShapeDtypeStruct((rows, LANE), jnp.int32),
        compiler_params=_params(1),
        name="moba_select",
    )(q_rot, kmean)


def _sample_attn_kernel(pt_ref, idx_ref, q_ref, kn_ref, vn_ref, *refs, t_new):
    n_blk = PAGES_PER_BLOCK * MOBA_TOPK
    k_refs, v_refs, o_ref = refs[:n_blk], refs[n_blk:2 * n_blk], refs[2 * n_blk]
    b, t = pl.program_id(0), pl.program_id(1)
    rows = q_ref.shape[0]
    scale = MOBA_HEAD_DIM ** -0.5
    row = b * t_new + t
    q8 = jnp.broadcast_to(q_ref[pl.ds(row, 1), :], (8, LANE)).astype(BF16)
    s_new = _dot_t(q8, kn_ref[...].astype(BF16)) * scale
    r_i = _iota((8, rows), 1)
    s_new = jnp.where((r_i >= b * t_new) & (r_i <= row), s_new, NEG)
    p_scale = jnp.max(s_new, axis=-1, keepdims=True)

    for head in range(MOBA_HEADS):
        @pl.when(pl.program_id(2) == head)
        def _():
            s_old = [_dot_t(q8, kr[:, head, :].astype(BF16)) * scale for kr in k_refs]
            m = p_scale
            for s in s_old:
                m = jnp.maximum(m, jnp.max(s, axis=-1, keepdims=True))
            p_new = jnp.exp(s_new - m)
            l = jnp.sum(p_new, axis=-1, keepdims=True)
            acc = _dot(p_new.astype(BF16), vn_ref[...].astype(BF16))
            for s, vr in zip(s_old, v_refs):
                p = jnp.exp(s - m)
                l = l + jnp.sum(p, axis=-1, keepdims=True)
                acc = acc + _dot(p.astype(BF16), vr[:, head, :].astype(BF16))
            o_ref[...] = (acc / l)[0:1, :]


def _sample_attn(q_rot, k_new, cols, cache_k, cache_v, page_table_flat, idx_flat, layer, batch, t_new, n_pages):
    rows = batch * t_new
    G = GROUP_WIDTH

    def page(kth, o):
        def index_map(b, t, h, pt, idx):
            blk = idx[(b * t_new + t) * LANE + h * MOBA_HEADS + kth]
            return (layer, pt[b * n_pages + PAGES_PER_BLOCK * blk + o], 0, 0, 0)
        return pl.BlockSpec((None, None, PAGE_SIZE, MOBA_HEADS, MOBA_HEAD_DIM), index_map)

    pages = [page(kth, o) for kth in range(MOBA_TOPK) for o in range(PAGES_PER_BLOCK)]
    return pl.pallas_call(
        functools.partial(_sample_attn_kernel, t_new=t_new),
        grid_spec=pltpu.PrefetchScalarGridSpec(
            num_scalar_prefetch=2,
            grid=(batch, t_new, MOBA_HEADS),
            in_specs=[
                pl.BlockSpec((rows, LANE), lambda b, t, h, pt, idx: (0, h)),
                pl.BlockSpec((rows, LANE), lambda b, t, h, pt, idx: (0, h)),
                pl.BlockSpec((rows, LANE), lambda b, t, h, pt, idx: (0, C_MV // LANE + h)),
            ] + pages + pages,
            out_specs=pl.BlockSpec((None, 1, LANE), lambda b, t, h, pt, idx: (b * t_new + t, 0, h)),
        ),
        out_shape=jax.ShapeDtypeStruct((rows, 1, G), F32),
        compiler_params=_params(3),
        name="moba_sample_attn",
    )(page_table_flat, idx_flat, q_rot, k_new, cols, *([cache_k] * len(pages)), *([cache_v] * len(pages)))


def _rownorm_kernel(x_ref, w_ref, y_ref):
    y_ref[...] = _rms(x_ref[...], w_ref[...]).astype(y_ref.dtype)


def _rownorm(x, w):
    rows, n = x.shape
    return pl.pallas_call(
        _rownorm_kernel,
        grid=(1,),
        in_specs=[pl.BlockSpec((rows, n), lambda i: (0, 0)), pl.BlockSpec((1, n), lambda i: (0, 0))],
        out_specs=pl.BlockSpec((rows, n), lambda i: (0, 0)),
        out_shape=jax.ShapeDtypeStruct((rows, n), BF16),
        compiler_params=_params(1),
        name="rownorm",
    )(x, w)


def _rope_tables(pos):
    half = MOBA_HEAD_DIM // 2
    freq = ROPE_THETA ** (-jnp.arange(half, dtype=F32) / half)
    ang = pos.astype(F32)[:, None] * freq[None, :]
    cos, sin = jnp.cos(ang), jnp.sin(ang)
    return jnp.concatenate([cos, cos], -1), jnp.concatenate([-sin, sin], -1)


def _pad_rows_front(x, rows):
    return jnp.pad(x, ((0, 0), (rows - x.shape[1], 0), (0, 0)))


def _pair_block_diag(s):
    B, H, N, _ = s.shape
    s = s.reshape(B, H // 2, 2, N, 1, N) * jnp.eye(2, dtype=s.dtype)[None, None, :, None, :, None]
    return s.reshape(B, H // 2, 2 * N, 2 * N)


def _pair_diag_blocks(s_bd):
    B, n_pairs, P, _ = s_bd.shape
    N = P // 2
    s = s_bd.reshape(B, n_pairs, 2, N, 2, N)
    return jnp.stack([s[:, :, 0, :, 0, :], s[:, :, 1, :, 1, :]], axis=2).reshape(B, 2 * n_pairs, N, N)


def _layer_weights(W, i):
    G = GROUP_WIDTH
    row = lambda v: v.reshape(1, -1)
    lane_pad = lambda v: jnp.pad(v, (0, LANE - v.shape[0])).reshape(1, LANE)
    zeros_r = jnp.zeros((RWKV_HEAD, G), F32)
    return dict(
        cw8=jnp.pad(W['ssd_conv_w'][i].T, ((0, 8 - SSD_CONV), (0, 0))),
        cb=row(W['ssd_conv_b'][i]),
        dtb=lane_pad(W['ssd_dt_bias'][i]),
        alog=lane_pad(W['ssd_a_log'][i]),
        dskip=row(jnp.repeat(W['ssd_d'][i], GROUP_WIDTH // SSD_HEADS)),
        ssd_nw=row(W['ssd_norm_w'][i]),
        qw=row(W['moba_q_norm_w'][i]), kw=row(W['moba_k_norm_w'][i]), ow=row(W['moba_out_norm_w'][i]),
        gvw=row(W['gmlp_v_norm_w'][i]), gow=row(W['gmlp_out_norm_w'][i]),
        ws=W['gmlp_w_s'][i], bs=W['gmlp_b_s'][i][:, :, None],
        mu=row(W['rwkv_mu'][i]), w0=row(W['rwkv_w0'][i]), a0=row(W['rwkv_a0'][i]),
        wup=jnp.concatenate([W['rwkv_w_up'][i], zeros_r], 0),
        aup=jnp.concatenate([zeros_r, W['rwkv_a_up'][i]], 0),
        gup=W['rwkv_g_up'][i],
        kk=row(W['rwkv_k_k'][i]), ka=row(W['rwkv_k_a'][i]), rk=row(W['rwkv_r_k'][i]),
        lnw=row(W['rwkv_ln_w'][i]), lnb=row(W['rwkv_ln_b'][i]),
    )


def _mixers(cols, B, T, Tp, lw, conv_prev, ssd_prev, rwkv_prev, shift_prev, t_valid):
    G = GROUP_WIDTH
    cols3 = cols.reshape(B, T, COLS)
    if Tp != T:
        cols3 = jnp.pad(cols3, ((0, 0), (0, Tp - T), (0, 0)))
    y_ssd, ssd_new = _ssd(cols3, _pad_rows_front(conv_prev, 8), ssd_prev.reshape(B, SSD_HEADS * SSD_HEADDIM, SSD_STATE),
                          lw['cw8'], lw['cb'], lw['dtb'], lw['alog'], lw['dskip'], lw['ssd_nw'], t_valid)
    y_gm, v_gm = _gmlp(cols3, lw['gvw'], lw['ws'], lw['bs'], lw['gow'])
    y_rw, rwkv_new = _rwkv(cols3, _pad_rows_front(shift_prev[:, None, :], 8), _pair_block_diag(rwkv_prev),
                           lw['mu'], lw['w0'], lw['wup'], lw['a0'], lw['aup'], lw['gup'], lw['kk'], lw['ka'],
                           lw['rk'], lw['lnw'], lw['lnb'], t_valid)
    crop = lambda y: y[:, :T].reshape(B * T, G)
    raw = cols.reshape(B, T, COLS)
    conv_new = raw[:, T - (SSD_CONV - 1):, C_X:C_X + SSD_CONV_DIM]
    shift_new = raw[:, T - 1, C_RR:C_RR + RWKV_COLS]
    states = (ssd_new.reshape(B, SSD_HEADS, SSD_HEADDIM, SSD_STATE), conv_new, _pair_diag_blocks(rwkv_new), shift_new)
    return crop(y_ssd), crop(y_gm), crop(y_rw), v_gm[:, :T], states


def kernel(x_prompt, x_sample, c_prompt, c_sample, cache_k, cache_v, page_table, state_ssd, state_ssd_conv, state_rwkv, state_rwkv_shift, norm_w, w_ada, b_ada, ffn_w1, ffn_w3, ffn_w2, w_in, w_out, ssd_conv_w, ssd_conv_b, ssd_dt_bias, ssd_a_log, ssd_d, ssd_norm_w, moba_q_norm_w, moba_k_norm_w, moba_out_norm_w, gmlp_v_norm_w, gmlp_w_s, gmlp_b_s, gmlp_out_norm_w, rwkv_mu, rwkv_w0, rwkv_w_up, rwkv_a0, rwkv_a_up, rwkv_g_up, rwkv_k_k, rwkv_k_a, rwkv_r_k, rwkv_ln_w, rwkv_ln_b):
    W = dict(ssd_conv_w=ssd_conv_w, ssd_conv_b=ssd_conv_b, ssd_dt_bias=ssd_dt_bias, ssd_a_log=ssd_a_log,
             ssd_d=ssd_d, ssd_norm_w=ssd_norm_w, moba_q_norm_w=moba_q_norm_w, moba_k_norm_w=moba_k_norm_w,
             moba_out_norm_w=moba_out_norm_w, gmlp_v_norm_w=gmlp_v_norm_w, gmlp_w_s=gmlp_w_s, gmlp_b_s=gmlp_b_s,
             gmlp_out_norm_w=gmlp_out_norm_w, rwkv_mu=rwkv_mu, rwkv_w0=rwkv_w0, rwkv_w_up=rwkv_w_up,
             rwkv_a0=rwkv_a0, rwkv_a_up=rwkv_a_up, rwkv_g_up=rwkv_g_up, rwkv_k_k=rwkv_k_k, rwkv_k_a=rwkv_k_a,
             rwkv_r_k=rwkv_r_k, rwkv_ln_w=rwkv_ln_w, rwkv_ln_b=rwkv_ln_b)
    Bp, Tq, D = x_prompt.shape
    Bs, Ts, _ = x_sample.shape
    n_pages = page_table.shape[1]
    assert n_pages * PAGE_SIZE == PAST_LEN and PAST_LEN % MOBA_BLOCK == 0
    assert PAST_LEN // MOBA_BLOCK >= MOBA_TOPK and Ts <= MOBA_BLOCK
    assert n_pages % (KMEAN_BLOCKS * PAGES_PER_BLOCK) == 0 and RWKV_CHUNK == RWKV_HEAD
    Rp, Rs = Bp * Tq, Bs * Ts

    w1_b, w3_b, w2_b = ffn_w1.astype(BF16), ffn_w3.astype(BF16), ffn_w2.astype(BF16)
    dt_lo = GROUP_WIDTH + SSD_CONV_DIM
    dt_hi = dt_lo + SSD_HEADS
    w_in_b = jnp.concatenate(
        [w_in[:, :, :dt_lo], w_in[:, :, dt_hi:], w_in[:, :, dt_lo:dt_hi],
         jnp.zeros((DEPTH, D, COLS - C_DT - SSD_HEADS), w_in.dtype)], axis=-1).astype(BF16)
    w_out_b = w_out.astype(BF16)
    nw4 = norm_w.reshape(DEPTH, 3, 1, D)
    b_ada3 = b_ada.reshape(DEPTH, 1, N_MOD * D)

    n_c = Bp + Bs
    c_all = jnp.pad(jnp.concatenate([c_prompt, c_sample], 0), ((0, -n_c % 8), (0, 0)))

    pos_p = jnp.arange(Tq, dtype=jnp.int32)
    pos_s = PAST_LEN + jnp.arange(Ts, dtype=jnp.int32)
    cos_p, sin_p = (jnp.tile(t, (Bp, 1)) for t in _rope_tables(pos_p))
    cos_s, sin_s = (jnp.tile(t, (Bs, 1)) for t in _rope_tables(pos_s))

    pt_flat = page_table.reshape(-1)

    zeros = lambda *s: jnp.zeros(s, F32)
    xp = x_prompt.reshape(Rp, D)
    xs = x_sample.reshape(Rs, D)
    TM = 512
    outs_p, outs_s = [], []
    for i in range(DEPTH):
        lw = _layer_weights(W, i)
        mod = _ada(c_all, w_ada, b_ada3, i).reshape(-1, N_MOD, D)
        mod_p = _Mod(mod[:Bp].reshape(Bp, N_MOD, 1, D), False, Tq, TM)
        mod_s = _Mod(jnp.repeat(mod[Bp:n_c], Ts, axis=0).transpose(1, 0, 2), True, Ts, Rs)

        xp = _ffn(xp, mod_p, 0, nw4, w1_b, w3_b, w2_b, i, 0, TM, 512)
        cols = _inproj(xp, mod_p, nw4, w_in_b, i, TM, 1536)
        q_rot, k_rot = _moba_prep(cols, cos_p, sin_p, lw['qw'], lw['kw'], TM)
        y_ssd, y_gm, y_rw, _, st = _mixers(cols, Bp, Tq, Tq, lw, zeros(Bp, SSD_CONV - 1, SSD_CONV_DIM),
                                           zeros(Bp, SSD_HEADS, SSD_HEADDIM, SSD_STATE),
                                           zeros(Bp, RWKV_HEADS, RWKV_HEAD, RWKV_HEAD), zeros(Bp, RWKV_COLS), None)
        y_att = _moba_attn(q_rot.reshape(Bp, Tq, -1), k_rot.reshape(Bp, Tq, -1), cols.reshape(Bp, Tq, COLS),
                           lw['ow']).reshape(Rp, -1)
        xp = _outproj(xp, mod_p, (y_ssd, y_att, y_gm, y_rw), w_out_b, i, TM)
        xp = _ffn(xp, mod_p, 6, nw4, w1_b, w3_b, w2_b, i, 1, TM, 512)
        shp = (Bp, Tq, MOBA_HEADS, MOBA_HEAD_DIM)
        outs_p.append((k_rot.reshape(shp), cols[:, C_MV:C_MV + GROUP_WIDTH].reshape(shp)) + st)

        xs = _ffn(xs, mod_s, 0, nw4, w1_b, w3_b, w2_b, i, 0, Rs, 512)
        cols = _inproj(xs, mod_s, nw4, w_in_b, i, Rs, 1536)
        q_rot, k_rot = _moba_prep(cols, cos_s, sin_s, lw['qw'], lw['kw'], Rs)
        y_ssd, y_gm, y_rw, v_gm, st = _mixers(cols, Bs, Ts, SAMPLE_PAD, lw, state_ssd_conv[i], state_ssd[i],
                                              state_rwkv[i], state_rwkv_shift[i], Ts)
        kmean = _kmean_pages(cache_k, pt_flat, i, Bs, n_pages)
        idx = _select(q_rot, kmean, Bs, Ts)
        o_att = _sample_attn(q_rot, k_rot, cols, cache_k, cache_v, pt_flat, idx.reshape(-1), i, Bs, Ts, n_pages)
        y_att = _rownorm(o_att.reshape(Rs, GROUP_WIDTH), lw['ow'])
        xs = _outproj(xs, mod_s, (y_ssd, y_att, y_gm, y_rw), w_out_b, i, Rs)
        xs = _ffn(xs, mod_s, 6, nw4, w1_b, w3_b, w2_b, i, 1, Rs, 512)
        shp = (Bs, Ts, MOBA_HEADS, MOBA_HEAD_DIM)
        outs_s.append((k_rot.reshape(shp), cols[:, C_MV:C_MV + GROUP_WIDTH].reshape(shp)) + st + (v_gm,))

    k_p, v_p, ssd_p, conv_p, rwkv_p, shift_p = (jnp.stack(s) for s in zip(*outs_p))
    k_s, v_s, ssd_s, conv_s, rwkv_s, shift_s, gmlp_v_s = (jnp.stack(s) for s in zip(*outs_s))
    return (xp.reshape(Bp, Tq, D), xs.reshape(Bs, Ts, D), k_p, v_p, k_s, v_s, ssd_p, ssd_s, conv_p, conv_s,
            rwkv_p, rwkv_s, shift_p, shift_s, gmlp_v_s)
```

```python
import functools
import math

import jax
import jax.numpy as jnp
from jax import lax
from jax.experimental import pallas as pl
from jax.experimental.pallas import tpu as pltpu

F32 = jnp.float32
BF16 = jnp.bfloat16
HI = lax.Precision.HIGHEST

D_MODEL = 2048
DEPTH = 2
PAST_LEN = 16384
PAGE_SIZE = 128
GROUP_WIDTH = 512
SSD_HEADS = 8
SSD_HEADDIM = 64
SSD_STATE = 128
SSD_CONV = 4
SSD_CHUNK = 128
SSD_CONV_DIM = 1024
MOBA_HEADS = 4
MOBA_HEAD_DIM = 128
MOBA_BLOCK = 256
MOBA_TOPK = 3
ROPE_THETA = 10000.0
GMLP_CHUNK = 128
GMLP_HEADS = 4
RWKV_HEADS = 8
RWKV_HEAD = 64
RWKV_COLS = 1792
RWKV_CHUNK = 64
RWKV_SUB = 16
RWKV_SEQS_PER_STEP = 4
RWKV_DECAY_SCALE = 0.606531
RWKV_LN_EPS = 64e-5
D_FF = 5632
N_MOD = 9
FFN_RES = 0.5
EPS = 1e-6
NEG = -1e30

COLS = 6144
C_Z, C_X, C_BC = 0, 512, 1024
C_MQ, C_MK, C_MV = 1536, 2048, 2560
C_GU, C_GV = 3072, 3584
C_RR, C_RK, C_RV, C_RL = 4096, 4608, 5120, 5632
C_DT = 5888

LANE = 128
SAMPLE_PAD = 128
VMEM_LIMIT = 56 * 1024 * 1024


def _params(n_axes, vmem=VMEM_LIMIT):
    return pltpu.CompilerParams(dimension_semantics=("arbitrary",) * n_axes, vmem_limit_bytes=vmem)


def _dot(a, b, precision=None):
    return jnp.dot(a, b, preferred_element_type=F32, precision=precision)


def _dot_t(a, b, precision=None):
    return lax.dot_general(a, b, (((1,), (1,)), ((), ())), preferred_element_type=F32, precision=precision)


def _dot_0(a, b, precision=None):
    return lax.dot_general(a, b, (((0,), (0,)), ((), ())), preferred_element_type=F32, precision=precision)


def _iota(shape, dim):
    return lax.broadcasted_iota(jnp.int32, shape, dim)


def _rms(x, w):
    return x * lax.rsqrt(jnp.mean(x * x, -1, keepdims=True) + EPS) * w


def _silu(x):
    return x * jax.nn.sigmoid(x)


def _ada_kernel(c_ref, w_ref, b_ref, o_ref):
    s = _silu(c_ref[...]).astype(BF16)
    o_ref[...] = _dot(s, w_ref[...].astype(BF16)) + b_ref[...]


def _ada(c_all, w_ada, b_ada3, layer):
    rows = c_all.shape[0]
    n_out = w_ada.shape[-1]
    tn = 1024
    return pl.pallas_call(
        _ada_kernel,
        grid=(n_out // tn,),
        in_specs=[
            pl.BlockSpec((rows, D_MODEL), lambda n: (0, 0)),
            pl.BlockSpec((None, D_MODEL, tn), lambda n: (layer, 0, n)),
            pl.BlockSpec((None, 1, tn), lambda n: (layer, 0, n)),
        ],
        out_specs=pl.BlockSpec((rows, tn), lambda n: (0, n)),
        out_shape=jax.ShapeDtypeStruct((rows, n_out), F32),
        compiler_params=_params(1),
        name="ada",
    )(c_all, w_ada, b_ada3)


class _Mod:
    def __init__(self, arr, per_row, rows_per_batch, tm):
        self.arr = arr
        self.per_row = per_row
        self.tiles_per_batch = None if per_row else rows_per_batch // tm
        self.tm = tm

    def spec(self, j):
        if self.per_row:
            return pl.BlockSpec((None, self.tm, D_MODEL), lambda r, *_: (j, r, 0))
        tpb = self.tiles_per_batch
        return pl.BlockSpec((None, None, 1, D_MODEL), lambda r, *_: (r // tpb, j, 0, 0))


def _ffn_kernel(x_ref, nw_ref, sh_ref, sc_ref, g_ref, w1_ref, w3_ref, w2_ref, o_ref, h_ref, acc_ref, *, nf):
    f = pl.program_id(1)

    @pl.when(f == 0)
    def _():
        xn = _rms(x_ref[...], nw_ref[...])
        h_ref[...] = (xn * (1 + sc_ref[...]) + sh_ref[...]).astype(BF16)
        acc_ref[...] = jnp.zeros_like(acc_ref)

    h = h_ref[...]
    a = _dot(h, w1_ref[...])
    b = _dot(h, w3_ref[...])
    acc_ref[...] += _dot((_silu(a) * b).astype(BF16), w2_ref[...])

    @pl.when(f == nf - 1)
    def _():
        o_ref[...] = x_ref[...] + FFN_RES * g_ref[...] * acc_ref[...]


def _ffn(x, mod, j0, nw4, w1, w3, w2, layer, slot, tm, tf):
    rows = x.shape[0]
    nf = D_FF // tf
    return pl.pallas_call(
        functools.partial(_ffn_kernel, nf=nf),
        grid=(rows // tm, nf),
        in_specs=[
            pl.BlockSpec((tm, D_MODEL), lambda r, f: (r, 0)),
            pl.BlockSpec((None, None, 1, D_MODEL), lambda r, f: (layer, 2 * slot, 0, 0)),
            mod.spec(j0), mod.spec(j0 + 1), mod.spec(j0 + 2),
            pl.BlockSpec((None, None, D_MODEL, tf), lambda r, f: (layer, slot, 0, f)),
            pl.BlockSpec((None, None, D_MODEL, tf), lambda r, f: (layer, slot, 0, f)),
            pl.BlockSpec((None, None, tf, D_MODEL), lambda r, f: (layer, slot, f, 0)),
        ],
        out_specs=pl.BlockSpec((tm, D_MODEL), lambda r, f: (r, 0)),
        out_shape=jax.ShapeDtypeStruct((rows, D_MODEL), F32),
        scratch_shapes=[pltpu.VMEM((tm, D_MODEL), BF16), pltpu.VMEM((tm, D_MODEL), F32)],
        compiler_params=_params(2),
        name="ffn",
    )(x, nw4, mod.arr, mod.arr, mod.arr, w1, w3, w2)


def _inproj_kernel(x_ref, nw_ref, sh_ref, sc_ref, w_ref, o_ref, h_ref):
    @pl.when(pl.program_id(1) == 0)
    def _():
        xn = _rms(x_ref[...], nw_ref[...])
        h_ref[...] = (xn * (1 + sc_ref[...]) + sh_ref[...]).astype(BF16)

    o_ref[...] = _dot(h_ref[...], w_ref[...])


def _inproj(x, mod, nw4, w_in, layer, tm, tn):
    rows = x.shape[0]
    return pl.pallas_call(
        _inproj_kernel,
        grid=(rows // tm, COLS // tn),
        in_specs=[
            pl.BlockSpec((tm, D_MODEL), lambda r, n: (r, 0)),
            pl.BlockSpec((None, None, 1, D_MODEL), lambda r, n: (layer, 1, 0, 0)),
            mod.spec(3), mod.spec(4),
            pl.BlockSpec((None, D_MODEL, tn), lambda r, n: (layer, 0, n)),
        ],
        out_specs=pl.BlockSpec((tm, tn), lambda r, n: (r, n)),
        out_shape=jax.ShapeDtypeStruct((rows, COLS), F32),
        scratch_shapes=[pltpu.VMEM((tm, D_MODEL), BF16)],
        compiler_params=_params(2),
        name="inproj",
    )(x, nw4, mod.arr, mod.arr, w_in)


def _outproj_kernel(x_ref, g_ref, y0_ref, y1_ref, y2_ref, y3_ref, w_ref, o_ref):
    G = GROUP_WIDTH
    acc = _dot(y0_ref[...], w_ref[0:G, :])
    acc += _dot(y1_ref[...], w_ref[G:2 * G, :])
    acc += _dot(y2_ref[...], w_ref[2 * G:3 * G, :])
    acc += _dot(y3_ref[...], w_ref[3 * G:4 * G, :])
    o_ref[...] = x_ref[...] + g_ref[...] * acc


def _outproj(x, mod, ys, w_out, layer, tm):
    rows = x.shape[0]
    yspec = pl.BlockSpec((tm, GROUP_WIDTH), lambda r: (r, 0))
    return pl.pallas_call(
        _outproj_kernel,
        grid=(rows // tm,),
        in_specs=[
            pl.BlockSpec((tm, D_MODEL), lambda r: (r, 0)),
            mod.spec(5), yspec, yspec, yspec, yspec,
            pl.BlockSpec((None, D_MODEL, D_MODEL), lambda r: (layer, 0, 0)),
        ],
        out_specs=pl.BlockSpec((tm, D_MODEL), lambda r: (r, 0)),
        out_shape=jax.ShapeDtypeStruct((rows, D_MODEL), F32),
        compiler_params=_params(1),
        name="outproj",
    )(x, mod.arr, *ys, w_out)


def _softplus(x):
    return jnp.maximum(x, 0.0) + jnp.log1p(jnp.exp(-jnp.abs(x)))


def _ssd_kernel(z_ref, x_ref, bc_ref, dt_ref, prev_ref, cw_ref, cb_ref, dtb_ref, alog_ref, dskip_ref, nw_ref,
                h0_ref, y_ref, hout_ref, ext_ref, st_ref, *, C, nc, t_valid):
    c = pl.program_id(1)
    G = GROUP_WIDTH

    @pl.when(c == 0)
    def _():
        ext_ref[0:8, :] = prev_ref[...]
        st_ref[...] = h0_ref[...]

    ext_ref[8:8 + C, 0:G] = x_ref[...]
    ext_ref[8:8 + C, G:2 * G] = bc_ref[...]
    conv = cb_ref[...] + ext_ref[5:5 + C, :] * cw_ref[0:1, :]
    for i in range(1, SSD_CONV):
        conv = conv + ext_ref[5 + i:5 + i + C, :] * cw_ref[i:i + 1, :]
    ext_ref[0:8, :] = ext_ref[C:C + 8, :]
    xbc = _silu(conv)
    xs = xbc[:, 0:G]

    dt = _softplus(dt_ref[...] + dtb_ref[...])
    if t_valid is not None:
        dt = jnp.where(c * C + _iota((C, LANE), 0) < t_valid, dt, 0.0)
    a = dt * (-jnp.exp(alog_ref[...]))
    tri = _iota((C, C), 0) >= _iota((C, C), 1)
    tri_b = tri.astype(BF16)
    a_hi, a_mid, a_lo = _split3(a)
    a_cs = _dg(tri_b, a_hi, _NN) + (_dg(tri_b, a_mid, _NN) + _dg(tri_b, a_lo, _NN))
    a_cs_t = a_cs.T
    lane_lo = _iota((C, LANE), 1) < SSD_HEADDIM
    row_lo = _iota((LANE, 1), 0) < SSD_HEADDIM
    col = lambda h: a_cs[:, h:h + 1]
    row = lambda h: a_cs_t[h:h + 1, :]
    last = lambda h: a_cs[C - 1:C, h:h + 1]
    by_head = lambda p, f: jnp.where(lane_lo, f(2 * p), f(2 * p + 1))
    decay_mat = lambda h: jnp.exp(jnp.where(tri, col(h) - row(h), -jnp.inf))

    n_groups = 2
    heads_per_group = SSD_HEADS // n_groups
    bm_s = [_split2(xbc[:, G + g * SSD_STATE:G + (g + 1) * SSD_STATE]) for g in range(n_groups)]
    cm_s = [_split2(xbc[:, G + (n_groups + g) * SSD_STATE:G + (n_groups + g + 1) * SSD_STATE]) for g in range(n_groups)]
    scores = [_dot3(cm_s[g], bm_s[g], _NT) for g in range(n_groups)]
    group = lambda p: 2 * p // heads_per_group
    each = lambda f: [f(p) for p in range(SSD_HEADS // 2)]
    m0_s = each(lambda p: _split2(scores[group(p)] * decay_mat(2 * p)))
    m1_s = each(lambda p: _split2(scores[group(p)] * decay_mat(2 * p + 1)))
    xs_p = each(lambda p: xs[:, p * LANE:(p + 1) * LANE])
    xdt = each(lambda p: xs_p[p] * by_head(p, lambda h: dt[:, h:h + 1]))
    xdt_s = each(lambda p: _split2(xdt[p]))
    y_diag = each(lambda p: jnp.where(lane_lo, _dot3(m0_s[p], xdt_s[p]), _dot3(m1_s[p], xdt_s[p])))
    st = each(lambda p: st_ref[p * LANE:(p + 1) * LANE, :])
    y_off = each(lambda p: _dot3(cm_s[group(p)], _split2(st[p]), _NT) * by_head(p, lambda h: jnp.exp(col(h))))
    decay = each(lambda p: by_head(p, lambda h: jnp.exp(last(h) - col(h))))
    new = each(lambda p: _dot3(_split2(xdt[p] * decay[p]), bm_s[group(p)], _TN))
    for p in range(SSD_HEADS // 2):
        keep = jnp.where(row_lo, jnp.exp(last(2 * p)), jnp.exp(last(2 * p + 1)))
        st_ref[p * LANE:(p + 1) * LANE, :] = st[p] * keep + new[p]
    ys = each(lambda p: y_diag[p] + y_off[p] + xs_p[p] * dskip_ref[:, p * LANE:(p + 1) * LANE])

    y = jnp.concatenate(ys, axis=1) * _silu(z_ref[...])
    y_ref[...] = _rms(y, nw_ref[...]).astype(y_ref.dtype)

    @pl.when(c == nc - 1)
    def _():
        hout_ref[...] = st_ref[...]


def _ssd(cols3, prev8, h0, cw8, cb, dtb, alog, dskip, nw, t_valid):
    B, T, _ = cols3.shape
    C = SSD_CHUNK
    nc = T // C
    G = GROUP_WIDTH
    vec = lambda n: pl.BlockSpec((1, n), lambda b, c: (0, 0))
    return pl.pallas_call(
        functools.partial(_ssd_kernel, C=C, nc=nc, t_valid=t_valid),
        grid=(B, nc),
        in_specs=[
            pl.BlockSpec((None, C, G), lambda b, c: (b, c, C_Z // G)),
            pl.BlockSpec((None, C, G), lambda b, c: (b, c, C_X // G)),
            pl.BlockSpec((None, C, G), lambda b, c: (b, c, C_BC // G)),
            pl.BlockSpec((None, C, LANE), lambda b, c: (b, c, C_DT // LANE)),
            pl.BlockSpec((None, 8, SSD_CONV_DIM), lambda b, c: (b, 0, 0)),
            pl.BlockSpec((8, SSD_CONV_DIM), lambda b, c: (0, 0)),
            vec(SSD_CONV_DIM), vec(LANE), vec(LANE), vec(G), vec(G),
            pl.BlockSpec((None, SSD_HEADS * SSD_HEADDIM, SSD_STATE), lambda b, c: (b, 0, 0)),
        ],
        out_specs=[
            pl.BlockSpec((None, C, G), lambda b, c: (b, c, 0)),
            pl.BlockSpec((None, SSD_HEADS * SSD_HEADDIM, SSD_STATE), lambda b, c: (b, 0, 0)),
        ],
        out_shape=[
            jax.ShapeDtypeStruct((B, T, G), BF16),
            jax.ShapeDtypeStruct((B, SSD_HEADS * SSD_HEADDIM, SSD_STATE), F32),
        ],
        scratch_shapes=[pltpu.VMEM((C + 8, SSD_CONV_DIM), F32), pltpu.VMEM((SSD_HEADS * SSD_HEADDIM, SSD_STATE), F32)],
        compiler_params=_params(2),
        name="ssd",
    )(cols3, cols3, cols3, cols3, prev8, cw8, cb, dtb, alog, dskip, nw, h0)


def _gmlp_kernel(u_ref, v_ref, vw_ref, ws_ref, bs_ref, ow_ref, y_ref, vout_ref, *, C):
    u = jax.nn.gelu(u_ref[...])
    v = _rms(jax.nn.gelu(v_ref[...]), vw_ref[...])
    vout_ref[...] = v
    tri = _iota((C, C), 0) >= _iota((C, C), 1)
    mixed = []
    for h in range(GMLP_HEADS):
        ws = jnp.where(tri, ws_ref[h], 0.0)
        mixed.append(_dot3(_split2(ws), _split2(v[:, h * LANE:(h + 1) * LANE])) + bs_ref[h])
    y = u * jnp.concatenate(mixed, axis=1)
    y_ref[...] = _rms(y, ow_ref[...]).astype(y_ref.dtype)


def _gmlp(cols3, vw, ws, bs, ow):
    B, T, _ = cols3.shape
    C = GMLP_CHUNK
    G = GROUP_WIDTH
    vec = pl.BlockSpec((1, G), lambda b, c: (0, 0))
    return pl.pallas_call(
        functools.partial(_gmlp_kernel, C=C),
        grid=(B, T // C),
        in_specs=[
            pl.BlockSpec((None, C, G), lambda b, c: (b, c, C_GU // G)),
            pl.BlockSpec((None, C, G), lambda b, c: (b, c, C_GV // G)),
            vec,
            pl.BlockSpec((GMLP_HEADS, C, C), lambda b, c: (0, 0, 0)),
            pl.BlockSpec((GMLP_HEADS, C, 1), lambda b, c: (0, 0, 0)),
            vec,
        ],
        out_specs=[pl.BlockSpec((None, C, G), lambda b, c: (b, c, 0))] * 2,
        out_shape=[jax.ShapeDtypeStruct((B, T, G), BF16), jax.ShapeDtypeStruct((B, T, G), F32)],
        compiler_params=_params(2),
        name="gmlp",
    )(cols3, cols3, vw, ws, bs, ow)


_NN = ((1,), (0,))
_NT = ((1,), (1,))
_TN = ((0,), (0,))


def _split2(x):
    hi = x.astype(BF16)
    return hi, (x - hi.astype(F32)).astype(BF16)


def _split3(x):
    hi = x.astype(BF16)
    rest = x - hi.astype(F32)
    mid = rest.astype(BF16)
    return hi, mid, (rest - mid.astype(F32)).astype(BF16)


def _dg(a, b, dims):
    return lax.dot_general(a, b, (dims, ((), ())), preferred_element_type=F32)


def _dot3(a, b, dims=_NN):
    (ah, al), (bh, bl) = a, b
    return _dg(ah, bh, dims) + (_dg(ah, bl, dims) + _dg(al, bh, dims))


def _block_diag(pieces, same_head):
    return tuple(jnp.where(same_head, jnp.concatenate([x, x], axis=0), jnp.zeros((), x.dtype)) for x in pieces)


def _rwkv_kernel(r_ref, k_ref, v_ref, lo_ref, prev_ref, mu_ref, w0_ref, wup_ref, a0_ref, aup_ref, gup_ref,
                 kk_ref, ka_ref, rk_ref, lnw_ref, lnb_ref, s0_ref, y_ref, sout_ref, ext_ref, st_ref,
                 *, C, nc, t_valid):
    c = pl.program_id(1)
    G = GROUP_WIDTH
    N = RWKV_HEAD
    P = 2 * N

    n_batch = r_ref.shape[0]
    n_pairs = G // P
    sls = [slice(p * P, (p + 1) * P) for p in range(n_pairs)]
    same_head_n = (_iota((P, P), 0) < N) == (_iota((P, P), 1) < N)
    ones_bd = same_head_n.astype(BF16)
    tri_cc = (_iota((C, C), 0) >= _iota((C, C), 1)).astype(BF16)

    @pl.when(c == 0)
    def _():
        ext_ref[:, 0:8, :] = prev_ref[...]
        st_ref[...] = s0_ref[...]

    def head_sum(x):
        def pair_sum(p):
            hi, lo = _split2(x[:, sls[p]])
            return _dg(hi, ones_bd, _NN) + _dg(lo, ones_bd, _NN)
        return jnp.concatenate([pair_sum(p) for p in range(n_pairs)], axis=1)

    def chunk_inputs(n):
        ext_ref[n, 8:8 + C, 0:G] = r_ref[n]
        ext_ref[n, 8:8 + C, G:2 * G] = k_ref[n]
        ext_ref[n, 8:8 + C, 2 * G:3 * G] = v_ref[n]
        ext_ref[n, 8:8 + C, 3 * G:RWKV_COLS] = lo_ref[n]
        cur = ext_ref[n, 8:8 + C, :]
        prev = ext_ref[n, 7:7 + C, :]
        xs = cur + (prev - cur) * mu_ref[...]
        ext_ref[n, 0:8, :] = ext_ref[n, C:C + 8, :]

        r, k, v = xs[:, 0:G], xs[:, G:2 * G], xs[:, 2 * G:3 * G]
        la = xs[:, 3 * G:3 * G + P]
        gl = xs[:, 3 * G + P:RWKV_COLS]
        w_log = -RWKV_DECAY_SCALE * jax.nn.sigmoid(
            w0_ref[...] + _dot3(_split2(jnp.tanh(la)), _split2(wup_ref[...])))
        a = jax.nn.sigmoid(a0_ref[...] + _dot3(_split2(la), _split2(aup_ref[...])))
        g = _dot3(_split2(jax.nn.sigmoid(gl)), _split2(gup_ref[...]))

        kk = k * kk_ref[...]
        kk = kk * lax.rsqrt(jnp.maximum(head_sum(kk * kk), 1e-12))
        k2 = k * (1 + (a - 1) * ka_ref[...])
        if t_valid is not None:
            ok = c * C + _iota((C, G), 0) < t_valid
            w_log = jnp.where(ok, w_log, 0.0)
            kk = jnp.where(ok, kk, 0.0)
            k2 = jnp.where(ok, k2, 0.0)
        b = kk * a

        w_hi, w_mid, w_lo = _split3(w_log)
        cl = _dg(tri_cc, w_hi, _NN) + (_dg(tri_cc, w_mid, _NN) + _dg(tri_cc, w_lo, _NN))
        cl_last = cl[C - 1:C, :]
        einv = jnp.exp(-cl)
        e_c = jnp.exp(cl_last - cl)
        return dict(r=r, v=v, k2=k2, g=g, kkp=kk * jnp.exp(cl - w_log), rp=r * jnp.exp(cl), bi=b * einv,
                    ki=k2 * einv, bt=b * e_c, kt=k2 * e_c, p_c=jnp.exp(cl_last))

    seqs = [chunk_inputs(n) for n in range(n_batch)]
    units = [(n, p) for n in range(n_batch) for p in range(n_pairs)]
    each = lambda f: [f(u) for u in range(len(units))]
    tile = lambda name: each(lambda u: seqs[units[u][0]][name][:, sls[units[u][1]]])
    kkp, rp, bi, ki, bt, kt, v_p = (tile(name) for name in ("kkp", "rp", "bi", "ki", "bt", "kt", "v"))

    t_i = _iota((C, P), 0)
    s_i = _iota((C, P), 1) % C
    strict = s_i < t_i
    incl = s_i <= t_i
    diag_blk = (s_i // RWKV_SUB) == (t_i // RWKV_SUB)
    eye = (s_i == t_i).astype(F32)
    same_head = (_iota((P, P), 0) < C) == (_iota((P, P), 1) < C)

    memo = {}

    def sp(x):
        if id(x) not in memo:
            memo[id(x)] = (x, _split2(x))
        return memo[id(x)][1]

    def bd(x):
        if ("bd", id(x)) not in memo:
            memo["bd", id(x)] = (x, _block_diag(sp(x), same_head))
        return memo["bd", id(x)][1]

    def mm(xs, ys):
        return each(lambda p: _dot3(sp(xs[p]), bd(ys[p])))

    kkp_s = each(lambda p: _split2(kkp[p]))
    rp_s = each(lambda p: _split2(rp[p]))
    lhs_s = each(lambda p: tuple(jnp.concatenate([x, y], axis=0) for x, y in zip(kkp_s[p], rp_s[p])))
    ab = each(lambda p: _dot3(lhs_s[p], _block_diag(_split2(bi[p]), same_head_n), _NT))
    ak = each(lambda p: _dot3(lhs_s[p], _block_diag(_split2(ki[p]), same_head_n), _NT))
    a_m = each(lambda p: jnp.where(strict, ab[p][0:C], 0.0))
    b_k = each(lambda p: jnp.where(strict, ak[p][0:C], 0.0))
    r_b = each(lambda p: jnp.where(incl, ab[p][C:2 * C], 0.0))
    r_k = each(lambda p: jnp.where(incl, ak[p][C:2 * C], 0.0))

    n_pow = each(lambda p: jnp.where(diag_blk, -a_m[p], 0.0))
    a_o = each(lambda p: jnp.where(diag_blk, 0.0, a_m[p]))
    t_d = each(lambda p: eye + n_pow[p])
    for _ in range(int(math.log2(RWKV_SUB)) - 1):
        n_pow = mm(n_pow, n_pow)
        step = mm(t_d, n_pow)
        t_d = each(lambda p: t_d[p] + step[p])
    m1 = mm(t_d, a_o)
    m2 = mm(m1, m1)
    im = each(lambda p: eye - m1[p])
    im_m2 = mm(im, m2)
    t_full = mm(each(lambda p: im[p] + im_m2[p]), t_d)

    st = each(lambda u: st_ref[units[u]])
    st_s = each(lambda p: _split2(st[p]))
    bkv = mm(b_k, v_p)
    rhs = each(lambda p: _dot3(kkp_s[p], st_s[p], _NT) + bkv[p])
    u = mm(t_full, rhs)
    rkv = mm(r_k, v_p)
    rbu = mm(r_b, u)
    ys = each(lambda p: _dot3(rp_s[p], st_s[p], _NT) + rkv[p] - rbu[p])
    upd = each(lambda p: _dot3(_split2(jnp.concatenate([v_p[p], -u[p]], axis=0)),
                               _split2(jnp.concatenate([kt[p], bt[p]], axis=0)), _TN))
    for u_i, (n, p) in enumerate(units):
        st_ref[n, p] = st[u_i] * seqs[n]["p_c"][:, sls[p]] + jnp.where(same_head_n, upd[u_i], 0.0)

    for n, seq in enumerate(seqs):
        y = jnp.concatenate(ys[n * n_pairs:(n + 1) * n_pairs], axis=1)
        mean = head_sum(y) * (1.0 / N)
        d = y - mean
        var = head_sum(d * d) * (1.0 / N)
        yn = d * lax.rsqrt(var + RWKV_LN_EPS) * lnw_ref[...] + lnb_ref[...]
        bonus = head_sum(seq["r"] * seq["k2"] * rk_ref[...]) * seq["v"]
        y_ref[n] = ((yn + bonus) * seq["g"]).astype(y_ref.dtype)

    @pl.when(c == nc - 1)
    def _():
        sout_ref[...] = st_ref[...]


def _rwkv(cols3, prev8, s0_bd, mu, w0, wup, a0, aup, gup, kk, ka, rk, lnw, lnb, t_valid):
    B, T, _ = cols3.shape
    C = RWKV_CHUNK
    nc = T // C
    G = GROUP_WIDTH
    P = 2 * RWKV_HEAD
    n_pairs = RWKV_HEADS // 2
    S = RWKV_SEQS_PER_STEP
    vec = lambda n: pl.BlockSpec((1, n), lambda b, c: (0, 0))
    mat = lambda m, n: pl.BlockSpec((m, n), lambda b, c: (0, 0))
    return pl.pallas_call(
        functools.partial(_rwkv_kernel, C=C, nc=nc, t_valid=t_valid),
        grid=(B // S, nc),
        in_specs=[
            pl.BlockSpec((S, C, G), lambda b, c: (b, c, C_RR // G)),
            pl.BlockSpec((S, C, G), lambda b, c: (b, c, C_RK // G)),
            pl.BlockSpec((S, C, G), lambda b, c: (b, c, C_RV // G)),
            pl.BlockSpec((S, C, 2 * P), lambda b, c: (b, c, C_RL // (2 * P))),
            pl.BlockSpec((S, 8, RWKV_COLS), lambda b, c: (b, 0, 0)),
            vec(RWKV_COLS), vec(G), mat(P, G), vec(G), mat(P, G), mat(P, G),
            vec(G), vec(G), vec(G), vec(G), vec(G),
            pl.BlockSpec((S, n_pairs, P, P), lambda b, c: (b, 0, 0, 0)),
        ],
        out_specs=[
            pl.BlockSpec((S, C, G), lambda b, c: (b, c, 0)),
            pl.BlockSpec((S, n_pairs, P, P), lambda b, c: (b, 0, 0, 0)),
        ],
        out_shape=[
            jax.ShapeDtypeStruct((B, T, G), BF16),
            jax.ShapeDtypeStruct((B, n_pairs, P, P), F32),
        ],
        scratch_shapes=[pltpu.VMEM((S, C + 8, RWKV_COLS), F32), pltpu.VMEM((S, n_pairs, P, P), F32)],
        compiler_params=_params(2),
        name="rwkv",
    )(cols3, cols3, cols3, cols3, prev8, mu, w0, wup, a0, aup, gup, kk, ka, rk, lnw, lnb, s0_bd)


def _moba_prep_kernel(q_ref, k_ref, v_ref, cos_ref, sin_ref, qw_ref, kw_ref, qo_ref, ko_ref, k4_ref, v4_ref):
    cos, sin = cos_ref[...], sin_ref[...]

    def rotate(x, w_ref):
        xn = _rms(x, w_ref[...])
        return xn * cos + pltpu.roll(xn, MOBA_HEAD_DIM // 2, 1) * sin

    for h in range(MOBA_HEADS):
        sl = slice(h * LANE, (h + 1) * LANE)
        qo_ref[:, sl] = rotate(q_ref[:, sl], qw_ref)
        k_h = rotate(k_ref[:, sl], kw_ref)
        ko_ref[:, sl] = k_h
        k4_ref[:, h, :] = k_h
        v4_ref[:, h, :] = v_ref[:, sl]


def _moba_prep(cols, cos, sin, qw, kw, tm):
    rows = cols.shape[0]
    G = GROUP_WIDTH
    tab = pl.BlockSpec((tm, LANE), lambda r: (r, 0))
    vec = pl.BlockSpec((1, LANE), lambda r: (0, 0))
    flat = pl.BlockSpec((tm, G), lambda r: (r, 0))
    heads = pl.BlockSpec((tm, MOBA_HEADS, MOBA_HEAD_DIM), lambda r: (r, 0, 0))
    return pl.pallas_call(
        _moba_prep_kernel,
        grid=(rows // tm,),
        in_specs=[
            pl.BlockSpec((tm, G), lambda r: (r, C_MQ // G)),
            pl.BlockSpec((tm, G), lambda r: (r, C_MK // G)),
            pl.BlockSpec((tm, G), lambda r: (r, C_MV // G)),
            tab, tab, vec, vec,
        ],
        out_specs=[flat, flat, heads, heads],
        out_shape=[jax.ShapeDtypeStruct((rows, G), F32)] * 2
        + [jax.ShapeDtypeStruct((rows, MOBA_HEADS, MOBA_HEAD_DIM), F32)] * 2,
        compiler_params=_params(1),
        name="moba_prep",
    )(cols, cols, cols, cos, sin, qw, kw)


def _top3(gate, lane_f):
    sel = jnp.zeros(gate.shape, jnp.bool_)
    g = gate
    big = float(gate.shape[-1])
    idxs = []
    for _ in range(MOBA_TOPK):
        m = jnp.max(g, axis=-1, keepdims=True)
        idx = jnp.min(jnp.where(g == m, lane_f, big), axis=-1, keepdims=True)
        pick = lane_f == idx
        sel = sel | pick
        g = jnp.where(pick, -jnp.inf, g)
        idxs.append(idx)
    return sel, idxs


def _moba_attn_kernel(q_ref, k_ref, v_ref, ow_ref, y_ref, km_ref, kb_ref, vt_ref, sel_ref, o_ref, *, nb):
    qi = pl.program_id(1)
    BLK = MOBA_BLOCK
    scale = MOBA_HEAD_DIM ** -0.5

    @pl.when(qi == 0)
    def _():
        for j in range(nb):
            rows = slice(j * BLK, (j + 1) * BLK)
            k_blk = k_ref[rows, :]
            km_ref[j:j + 1, :] = jnp.mean(k_blk, axis=0, keepdims=True)
            kb_ref[rows, :] = k_blk.astype(BF16)
            vt_ref[:, rows] = v_ref[rows, :].T.astype(BF16)

    blk_i = _iota((nb, BLK), 0)
    blk_f = blk_i.astype(F32)
    past = blk_i < qi
    causal = _iota((BLK, BLK), 0) <= _iota((BLK, BLK), 1)
    own = pl.multiple_of(qi * BLK, BLK)

    sls = [slice(h * LANE, (h + 1) * LANE) for h in range(MOBA_HEADS)]
    each = lambda f: [f(h) for h in range(MOBA_HEADS)]
    q_t = each(lambda h: q_ref[:, sls[h]].T)
    gate = each(lambda h: jnp.where(past, _dot3(_split2(km_ref[:, sls[h]]), _split2(q_t[h])), NEG))
    sel = each(lambda h: jnp.zeros((nb, BLK), jnp.bool_))
    for _ in range(MOBA_TOPK):
        best = each(lambda h: jnp.max(gate[h], axis=0, keepdims=True))
        idx = each(lambda h: jnp.min(jnp.where(gate[h] == best[h], blk_f, float(nb)), axis=0, keepdims=True))
        sel = each(lambda h: sel[h] | (blk_f == idx[h]))
        gate = each(lambda h: jnp.where(blk_f == idx[h], -jnp.inf, gate[h]))
    for h in range(MOBA_HEADS):
        sel_ref[h] = (sel[h] & past).astype(F32)
    qb_t = each(lambda h: q_t[h].astype(BF16))

    def block_scores(start, h):
        return _dot(kb_ref[pl.ds(start, BLK), sls[h]], qb_t[h]) * scale

    s0 = each(lambda h: jnp.where(causal, block_scores(own, h), NEG))
    m0 = each(lambda h: jnp.max(s0[h], axis=0, keepdims=True))
    p0 = each(lambda h: jnp.exp(s0[h] - m0[h]))
    l0 = each(lambda h: jnp.sum(p0[h], axis=0, keepdims=True))
    acc0 = each(lambda h: _dot(vt_ref[sls[h], pl.ds(own, BLK)], p0[h].astype(BF16)))

    def body(j, carry):
        m_i, l_i, acc = carry
        start = pl.multiple_of(j * BLK, BLK)
        s = each(lambda h: jnp.where(sel_ref[h, pl.ds(j, 1), :] > 0.0, block_scores(start, h), NEG))
        m_n = each(lambda h: jnp.maximum(m_i[h], jnp.max(s[h], axis=0, keepdims=True)))
        alpha = each(lambda h: jnp.exp(m_i[h] - m_n[h]))
        p = each(lambda h: jnp.exp(s[h] - m_n[h]))
        l_n = each(lambda h: alpha[h] * l_i[h] + jnp.sum(p[h], axis=0, keepdims=True))
        acc_n = each(lambda h: alpha[h] * acc[h] + _dot(vt_ref[sls[h], pl.ds(start, BLK)], p[h].astype(BF16)))
        return tuple(m_n), tuple(l_n), tuple(acc_n)

    _, l_f, acc_f = lax.fori_loop(0, qi, body, (tuple(m0), tuple(l0), tuple(acc0)))
    for h in range(MOBA_HEADS):
        o_ref[:, sls[h]] = (acc_f[h] / l_f[h]).T

    y_ref[...] = _rms(o_ref[...], ow_ref[...]).astype(y_ref.dtype)


def _moba_attn(q3, k3, cols3, ow):
    B, T, G = q3.shape
    BLK = MOBA_BLOCK
    nb = T // BLK
    return pl.pallas_call(
        functools.partial(_moba_attn_kernel, nb=nb),
        grid=(B, nb),
        in_specs=[
            pl.BlockSpec((None, BLK, G), lambda b, i: (b, i, 0)),
            pl.BlockSpec((None, T, G), lambda b, i: (b, 0, 0)),
            pl.BlockSpec((None, T, G), lambda b, i: (b, 0, C_MV // G)),
            pl.BlockSpec((1, G), lambda b, i: (0, 0)),
        ],
        out_specs=pl.BlockSpec((None, BLK, G), lambda b, i: (b, i, 0)),
        out_shape=jax.ShapeDtypeStruct((B, T, G), BF16),
        scratch_shapes=[pltpu.VMEM((nb, G), F32), pltpu.VMEM((T, G), BF16), pltpu.VMEM((G, T), BF16),
                        pltpu.VMEM((MOBA_HEADS, nb, BLK), F32), pltpu.VMEM((BLK, G), F32)],
        compiler_params=_params(2),
        name="moba_attn",
    )(q3, k3, cols3, ow)


PAGES_PER_BLOCK = MOBA_BLOCK // PAGE_SIZE
KMEAN_BLOCKS = 8


def _kmean_kernel(pt_ref, *refs):
    page_refs, o_ref = refs[:-1], refs[-1]
    for j in range(KMEAN_BLOCKS):
        s = jnp.sum(page_refs[PAGES_PER_BLOCK * j][...], axis=0)
        for o in range(1, PAGES_PER_BLOCK):
            s = s + jnp.sum(page_refs[PAGES_PER_BLOCK * j + o][...], axis=0)
        o_ref[j] = s * (1.0 / MOBA_BLOCK)


def _kmean_pages(cache_k, page_table_flat, layer, batch, n_pages):
    nbk = n_pages // PAGES_PER_BLOCK
    per_step = KMEAN_BLOCKS * PAGES_PER_BLOCK

    def page(o):
        return pl.BlockSpec((None, None, PAGE_SIZE, MOBA_HEADS, MOBA_HEAD_DIM),
                            lambda b, j, pt: (layer, pt[b * n_pages + j * per_step + o], 0, 0, 0))

    return pl.pallas_call(
        _kmean_kernel,
        grid_spec=pltpu.PrefetchScalarGridSpec(
            num_scalar_prefetch=1,
            grid=(batch, nbk // KMEAN_BLOCKS),
            in_specs=[page(o) for o in range(per_step)],
            out_specs=pl.BlockSpec((None, KMEAN_BLOCKS, MOBA_HEADS, MOBA_HEAD_DIM), lambda b, j, pt: (b, j, 0, 0)),
        ),
        out_shape=jax.ShapeDtypeStruct((batch, nbk, MOBA_HEADS, MOBA_HEAD_DIM), F32),
        compiler_params=_params(2),
        name="moba_kmean_pages",
    )(page_table_flat, *([cache_k] * per_step))


def _select_kernel(q_ref, km_ref, idx_ref, *, batch, t_new, nbk):
    rows = batch * t_new
    lane_i = _iota((rows, LANE), 1)
    lane_f = lane_i.astype(F32)
    row_b = _iota((rows, LANE), 0) // t_new
    out = jnp.zeros((rows, LANE), F32)
    pad = jnp.zeros((LANE - nbk, LANE), F32)
    for h in range(MOBA_HEADS):
        sl = slice(h * LANE, (h + 1) * LANE)
        q = q_ref[:, sl]
        gate = jnp.zeros((rows, LANE), F32)
        for b in range(batch):
            km = jnp.concatenate([km_ref[b, :, h, :], pad], axis=0)
            gate = jnp.where(row_b == b, _dot_t(q, km, HI), gate)
        gate = jnp.where(lane_i < nbk, gate, NEG)
        _, idxs = _top3(gate, lane_f)
        for kth, idx in enumerate(idxs):
            out = jnp.where(lane_i == h * MOBA_HEADS + kth, idx, out)
    idx_ref[...] = out.astype(jnp.int32)


def _select(q_rot, kmean, batch, t_new):
    rows = batch * t_new
    nbk = kmean.shape[1]
    G = GROUP_WIDTH
    return pl.pallas_call(
        functools.partial(_select_kernel, batch=batch, t_new=t_new, nbk=nbk),
        grid=(1,),
        in_specs=[
            pl.BlockSpec((rows, G), lambda i: (0, 0)),
            pl.BlockSpec((batch, nbk, MOBA_HEADS, MOBA_HEAD_DIM), lambda i: (0, 0, 0, 0)),
        ],
        out_specs=pl.BlockSpec((rows, LANE), lambda i: (0, 0)),
        out_shape=jax.ShapeDtypeStruct((rows, LANE), jnp.int32),
        compiler_params=_params(1),
        name="moba_select",
    )(q_rot, kmean)


def _sample_attn_kernel(pt_ref, idx_ref, q_ref, kn_ref, vn_ref, ow_ref, ck_ref, cv_ref, y_ref,
                        kbuf, vbuf, sems, o_ref, *, layer, t_new, n_pages):
    b = pl.program_id(0)
    rows = kn_ref.shape[0]
    scale = MOBA_HEAD_DIM ** -0.5
    per_query = PAGES_PER_BLOCK * MOBA_TOPK
    queries = [(t, h) for t in range(t_new) for h in range(MOBA_HEADS)]

    def page_copies(qn):
        t, h = queries[qn]
        out = []
        for kth in range(MOBA_TOPK):
            blk = idx_ref[(b * t_new + t) * LANE + h * MOBA_HEADS + kth]
            for o in range(PAGES_PER_BLOCK):
                page = pt_ref[b * n_pages + PAGES_PER_BLOCK * blk + o]
                slot = qn * per_query + kth * PAGES_PER_BLOCK + o
                out.append(pltpu.make_async_copy(ck_ref.at[layer, page, :, h, :], kbuf.at[slot], sems.at[0, qn]))
                out.append(pltpu.make_async_copy(cv_ref.at[layer, page, :, h, :], vbuf.at[slot], sems.at[1, qn]))
        return out

    copies = [page_copies(qn) for qn in range(len(queries))]
    for group in copies:
        for cp in group:
            cp.start()

    r_i = _iota((8, rows), 1)
    for qn, (t, h) in enumerate(queries):
        sl = slice(h * LANE, (h + 1) * LANE)
        q8 = jnp.broadcast_to(q_ref[t:t + 1, sl], (8, LANE)).astype(BF16)
        s_new = _dot_t(q8, kn_ref[:, sl].astype(BF16)) * scale
        s_new = jnp.where((r_i >= b * t_new) & (r_i <= b * t_new + t), s_new, NEG)
        for cp in copies[qn]:
            cp.wait()
        slots = range(qn * per_query, (qn + 1) * per_query)
        s_old = [_dot_t(q8, kbuf[slot].astype(BF16)) * scale for slot in slots]
        m = jnp.max(s_new, axis=-1, keepdims=True)
        for s in s_old:
            m = jnp.maximum(m, jnp.max(s, axis=-1, keepdims=True))
        p_new = jnp.exp(s_new - m)
        l = jnp.sum(p_new, axis=-1, keepdims=True)
        acc = _dot(p_new.astype(BF16), vn_ref[:, sl].astype(BF16))
        for s, slot in zip(s_old, slots):
            p = jnp.exp(s - m)
            l = l + jnp.sum(p, axis=-1, keepdims=True)
            acc = acc + _dot(p.astype(BF16), vbuf[slot].astype(BF16))
        o_ref[t:t + 1, sl] = (acc / l)[0:1, :]

    y_ref[...] = _rms(o_ref[...], ow_ref[...]).astype(y_ref.dtype)


def _sample_attn(q_rot, k_new, cols, ow, cache_k, cache_v, page_table_flat, idx_flat, layer, batch, t_new, n_pages):
    rows = batch * t_new
    G = GROUP_WIDTH
    n_slots = t_new * MOBA_HEADS * MOBA_TOPK * PAGES_PER_BLOCK
    return pl.pallas_call(
        functools.partial(_sample_attn_kernel, layer=layer, t_new=t_new, n_pages=n_pages),
        grid_spec=pltpu.PrefetchScalarGridSpec(
            num_scalar_prefetch=2,
            grid=(batch,),
            in_specs=[
                pl.BlockSpec((None, t_new, G), lambda b, pt, idx: (b, 0, 0)),
                pl.BlockSpec((rows, G), lambda b, pt, idx: (0, 0)),
                pl.BlockSpec((rows, G), lambda b, pt, idx: (0, C_MV // G)),
                pl.BlockSpec((1, G), lambda b, pt, idx: (0, 0)),
                pl.BlockSpec(memory_space=pl.ANY),
                pl.BlockSpec(memory_space=pl.ANY),
            ],
            out_specs=pl.BlockSpec((None, t_new, G), lambda b, pt, idx: (b, 0, 0)),
            scratch_shapes=[
                pltpu.VMEM((n_slots, PAGE_SIZE, MOBA_HEAD_DIM), F32),
                pltpu.VMEM((n_slots, PAGE_SIZE, MOBA_HEAD_DIM), F32),
                pltpu.SemaphoreType.DMA((2, t_new * MOBA_HEADS)),
                pltpu.VMEM((t_new, G), F32),
            ],
        ),
        out_shape=jax.ShapeDtypeStruct((batch, t_new, G), BF16),
        compiler_params=_params(1),
        name="moba_sample_attn",
    )(page_table_flat, idx_flat, q_rot.reshape(batch, t_new, G), k_new, cols, ow, cache_k, cache_v)


def _rope_tables(pos):
    half = MOBA_HEAD_DIM // 2
    freq = ROPE_THETA ** (-jnp.arange(half, dtype=F32) / half)
    ang = pos.astype(F32)[:, None] * freq[None, :]
    cos, sin = jnp.cos(ang), jnp.sin(ang)
    return jnp.concatenate([cos, cos], -1), jnp.concatenate([-sin, sin], -1)


def _pad_rows_front(x, rows):
    return jnp.pad(x, ((0, 0), (rows - x.shape[1], 0), (0, 0)))


def _pair_block_diag(s):
    B, H, N, _ = s.shape
    s = s.reshape(B, H // 2, 2, N, 1, N) * jnp.eye(2, dtype=s.dtype)[None, None, :, None, :, None]
    return s.reshape(B, H // 2, 2 * N, 2 * N)


def _pair_diag_blocks(s_bd):
    B, n_pairs, P, _ = s_bd.shape
    N = P // 2
    s = s_bd.reshape(B, n_pairs, 2, N, 2, N)
    return jnp.stack([s[:, :, 0, :, 0, :], s[:, :, 1, :, 1, :]], axis=2).reshape(B, 2 * n_pairs, N, N)


def _layer_weights(W, i):
    G = GROUP_WIDTH
    row = lambda v: v.reshape(1, -1)
    lane_pad = lambda v: jnp.pad(v, (0, LANE - v.shape[0])).reshape(1, LANE)
    zeros_r = jnp.zeros((RWKV_HEAD, G), F32)
    return dict(
        cw8=jnp.pad(W['ssd_conv_w'][i].T, ((0, 8 - SSD_CONV), (0, 0))),
        cb=row(W['ssd_conv_b'][i]),
        dtb=lane_pad(W['ssd_dt_bias'][i]),
        alog=lane_pad(W['ssd_a_log'][i]),
        dskip=row(jnp.repeat(W['ssd_d'][i], GROUP_WIDTH // SSD_HEADS)),
        ssd_nw=row(W['ssd_norm_w'][i]),
        qw=row(W['moba_q_norm_w'][i]), kw=row(W['moba_k_norm_w'][i]), ow=row(W['moba_out_norm_w'][i]),
        gvw=row(W['gmlp_v_norm_w'][i]), gow=row(W['gmlp_out_norm_w'][i]),
        ws=W['gmlp_w_s'][i], bs=W['gmlp_b_s'][i][:, :, None],
        mu=row(W['rwkv_mu'][i]), w0=row(W['rwkv_w0'][i]), a0=row(W['rwkv_a0'][i]),
        wup=jnp.concatenate([W['rwkv_w_up'][i], zeros_r], 0),
        aup=jnp.concatenate([zeros_r, W['rwkv_a_up'][i]], 0),
        gup=W['rwkv_g_up'][i],
        kk=row(W['rwkv_k_k'][i]), ka=row(W['rwkv_k_a'][i]), rk=row(W['rwkv_r_k'][i]),
        lnw=row(W['rwkv_ln_w'][i]), lnb=row(W['rwkv_ln_b'][i]),
    )


def _mixers(cols, B, T, Tp, lw, conv_prev, ssd_prev, rwkv_prev, shift_prev, t_valid):
    G = GROUP_WIDTH
    cols3 = cols.reshape(B, T, COLS)
    if Tp != T:
        cols3 = jnp.pad(cols3, ((0, 0), (0, Tp - T), (0, 0)))
    y_ssd, ssd_new = _ssd(cols3, _pad_rows_front(conv_prev, 8), ssd_prev.reshape(B, SSD_HEADS * SSD_HEADDIM, SSD_STATE),
                          lw['cw8'], lw['cb'], lw['dtb'], lw['alog'], lw['dskip'], lw['ssd_nw'], t_valid)
    y_gm, v_gm = _gmlp(cols3, lw['gvw'], lw['ws'], lw['bs'], lw['gow'])
    y_rw, rwkv_new = _rwkv(cols3, _pad_rows_front(shift_prev[:, None, :], 8), _pair_block_diag(rwkv_prev),
                           lw['mu'], lw['w0'], lw['wup'], lw['a0'], lw['aup'], lw['gup'], lw['kk'], lw['ka'],
                           lw['rk'], lw['lnw'], lw['lnb'], t_valid)
    crop = lambda y: y[:, :T].reshape(B * T, G)
    raw = cols.reshape(B, T, COLS)
    conv_new = raw[:, T - (SSD_CONV - 1):, C_X:C_X + SSD_CONV_DIM]
    shift_new = raw[:, T - 1, C_RR:C_RR + RWKV_COLS]
    states = (ssd_new.reshape(B, SSD_HEADS, SSD_HEADDIM, SSD_STATE), conv_new, _pair_diag_blocks(rwkv_new), shift_new)
    return crop(y_ssd), crop(y_gm), crop(y_rw), v_gm[:, :T], states


def kernel(x_prompt, x_sample, c_prompt, c_sample, cache_k, cache_v, page_table, state_ssd, state_ssd_conv, state_rwkv, state_rwkv_shift, norm_w, w_ada, b_ada, ffn_w1, ffn_w3, ffn_w2, w_in, w_out, ssd_conv_w, ssd_conv_b, ssd_dt_bias, ssd_a_log, ssd_d, ssd_norm_w, moba_q_norm_w, moba_k_norm_w, moba_out_norm_w, gmlp_v_norm_w, gmlp_w_s, gmlp_b_s, gmlp_out_norm_w, rwkv_mu, rwkv_w0, rwkv_w_up, rwkv_a0, rwkv_a_up, rwkv_g_up, rwkv_k_k, rwkv_k_a, rwkv_r_k, rwkv_ln_w, rwkv_ln_b):
    W = dict(ssd_conv_w=ssd_conv_w, ssd_conv_b=ssd_conv_b, ssd_dt_bias=ssd_dt_bias, ssd_a_log=ssd_a_log,
             ssd_d=ssd_d, ssd_norm_w=ssd_norm_w, moba_q_norm_w=moba_q_norm_w, moba_k_norm_w=moba_k_norm_w,
             moba_out_norm_w=moba_out_norm_w, gmlp_v_norm_w=gmlp_v_norm_w, gmlp_w_s=gmlp_w_s, gmlp_b_s=gmlp_b_s,
             gmlp_out_norm_w=gmlp_out_norm_w, rwkv_mu=rwkv_mu, rwkv_w0=rwkv_w0, rwkv_w_up=rwkv_w_up,
             rwkv_a0=rwkv_a0, rwkv_a_up=rwkv_a_up, rwkv_g_up=rwkv_g_up, rwkv_k_k=rwkv_k_k, rwkv_k_a=rwkv_k_a,
             rwkv_r_k=rwkv_r_k, rwkv_ln_w=rwkv_ln_w, rwkv_ln_b=rwkv_ln_b)
    Bp, Tq, D = x_prompt.shape
    Bs, Ts, _ = x_sample.shape
    n_pages = page_table.shape[1]
    assert n_pages * PAGE_SIZE == PAST_LEN and PAST_LEN % MOBA_BLOCK == 0
    assert PAST_LEN // MOBA_BLOCK >= MOBA_TOPK and Ts <= MOBA_BLOCK
    assert n_pages % (KMEAN_BLOCKS * PAGES_PER_BLOCK) == 0 and RWKV_CHUNK == RWKV_HEAD
    Rp, Rs = Bp * Tq, Bs * Ts

    w1_b, w3_b, w2_b = ffn_w1.astype(BF16), ffn_w3.astype(BF16), ffn_w2.astype(BF16)
    dt_lo = GROUP_WIDTH + SSD_CONV_DIM
    dt_hi = dt_lo + SSD_HEADS
    w_in_b = jnp.concatenate(
        [w_in[:, :, :dt_lo], w_in[:, :, dt_hi:], w_in[:, :, dt_lo:dt_hi],
         jnp.zeros((DEPTH, D, COLS - C_DT - SSD_HEADS), w_in.dtype)], axis=-1).astype(BF16)
    w_out_b = w_out.astype(BF16)
    nw4 = norm_w.reshape(DEPTH, 3, 1, D)
    b_ada3 = b_ada.reshape(DEPTH, 1, N_MOD * D)

    n_c = Bp + Bs
    c_all = jnp.pad(jnp.concatenate([c_prompt, c_sample], 0), ((0, -n_c % 8), (0, 0)))

    pos_p = jnp.arange(Tq, dtype=jnp.int32)
    pos_s = PAST_LEN + jnp.arange(Ts, dtype=jnp.int32)
    cos_p, sin_p = (jnp.tile(t, (Bp, 1)) for t in _rope_tables(pos_p))
    cos_s, sin_s = (jnp.tile(t, (Bs, 1)) for t in _rope_tables(pos_s))

    pt_flat = page_table.reshape(-1)

    zeros = lambda *s: jnp.zeros(s, F32)
    xp = x_prompt.reshape(Rp, D)
    xs = x_sample.reshape(Rs, D)
    TM = 512
    outs_p, outs_s = [], []
    for i in range(DEPTH):
        lw = _layer_weights(W, i)
        mod = _ada(c_all, w_ada, b_ada3, i).reshape(-1, N_MOD, D)
        mod_p = _Mod(mod[:Bp].reshape(Bp, N_MOD, 1, D), False, Tq, TM)
        mod_s = _Mod(jnp.repeat(mod[Bp:n_c], Ts, axis=0).transpose(1, 0, 2), True, Ts, Rs)

        xp = _ffn(xp, mod_p, 0, nw4, w1_b, w3_b, w2_b, i, 0, TM, 512)
        cols = _inproj(xp, mod_p, nw4, w_in_b, i, TM, 1536)
        q_rot, k_rot, k4, v4 = _moba_prep(cols, cos_p, sin_p, lw['qw'], lw['kw'], TM)
        y_ssd, y_gm, y_rw, _, st = _mixers(cols, Bp, Tq, Tq, lw, zeros(Bp, SSD_CONV - 1, SSD_CONV_DIM),
                                           zeros(Bp, SSD_HEADS, SSD_HEADDIM, SSD_STATE),
                                           zeros(Bp, RWKV_HEADS, RWKV_HEAD, RWKV_HEAD), zeros(Bp, RWKV_COLS), None)
        y_att = _moba_attn(q_rot.reshape(Bp, Tq, -1), k_rot.reshape(Bp, Tq, -1), cols.reshape(Bp, Tq, COLS),
                           lw['ow']).reshape(Rp, -1)
        xp = _outproj(xp, mod_p, (y_ssd, y_att, y_gm, y_rw), w_out_b, i, TM)
        xp = _ffn(xp, mod_p, 6, nw4, w1_b, w3_b, w2_b, i, 1, TM, 512)
        shp = (Bp, Tq, MOBA_HEADS, MOBA_HEAD_DIM)
        outs_p.append((k4.reshape(shp), v4.reshape(shp)) + st)

        xs = _ffn(xs, mod_s, 0, nw4, w1_b, w3_b, w2_b, i, 0, Rs, 512)
        cols = _inproj(xs, mod_s, nw4, w_in_b, i, Rs, 1536)
        q_rot, k_rot, k4, v4 = _moba_prep(cols, cos_s, sin_s, lw['qw'], lw['kw'], Rs)
        y_ssd, y_gm, y_rw, v_gm, st = _mixers(cols, Bs, Ts, SAMPLE_PAD, lw, state_ssd_conv[i], state_ssd[i],
                                              state_rwkv[i], state_rwkv_shift[i], Ts)
        kmean = _kmean_pages(cache_k, pt_flat, i, Bs, n_pages)
        idx = _select(q_rot, kmean, Bs, Ts)
        y_att = _sample_attn(q_rot, k_rot, cols, lw['ow'], cache_k, cache_v, pt_flat, idx.reshape(-1), i, Bs, Ts,
                             n_pages).reshape(Rs, GROUP_WIDTH)
        xs = _outproj(xs, mod_s, (y_ssd, y_att, y_gm, y_rw), w_out_b, i, Rs)
        xs = _ffn(xs, mod_s, 6, nw4, w1_b, w3_b, w2_b, i, 1, Rs, 512)
        shp = (Bs, Ts, MOBA_HEADS, MOBA_HEAD_DIM)
        outs_s.append((k4.reshape(shp), v4.reshape(shp)) + st + (v_gm,))

    k_p, v_p, ssd_p, conv_p, rwkv_p, shift_p = (jnp.stack(s) for s in zip(*outs_p))
    k_s, v_s, ssd_s, conv_s, rwkv_s, shift_s, gmlp_v_s = (jnp.stack(s) for s in zip(*outs_s))
    return (xp.reshape(Bp, Tq, D), xs.reshape(Bs, Ts, D), k_p, v_p, k_s, v_s, ssd_p, ssd_s, conv_p, conv_s,
            rwkv_p, rwkv_s, shift_p, shift_s, gmlp_v_s)
```

```python
import functools
import math

import jax
import jax.numpy as jnp
from jax import lax
from jax.experimental import pallas as pl
from jax.experimental.pallas import tpu as pltpu

F32 = jnp.float32
BF16 = jnp.bfloat16
HI = lax.Precision.HIGHEST

D_MODEL = 2048
DEPTH = 2
PAST_LEN = 16384
PAGE_SIZE = 128
GROUP_WIDTH = 512
SSD_HEADS = 8
SSD_HEADDIM = 64
SSD_STATE = 128
SSD_CONV = 4
SSD_CHUNK = 128
SSD_CONV_DIM = 1024
MOBA_HEADS = 4
MOBA_HEAD_DIM = 128
MOBA_BLOCK = 256
MOBA_TOPK = 3
ROPE_THETA = 10000.0
GMLP_CHUNK = 128
GMLP_HEADS = 4
RWKV_HEADS = 8
RWKV_HEAD = 64
RWKV_COLS = 1792
RWKV_CHUNK = 64
RWKV_SUB = 16
RWKV_SEQS_PER_STEP = 4
RWKV_DECAY_SCALE = 0.606531
RWKV_LN_EPS = 64e-5
D_FF = 5632
N_MOD = 9
FFN_RES = 0.5
EPS = 1e-6
NEG = -1e30

COLS = 6144
IN_COLS = 5896
C_Z, C_X, C_BC = 0, 512, 1024
C_MQ, C_MK, C_MV = 1536, 2048, 2560
C_GU, C_GV = 3072, 3584
C_RR, C_RK, C_RV, C_RL = 4096, 4608, 5120, 5632
C_DT = 5888

LANE = 128
SAMPLE_PAD = 128
VMEM_LIMIT = 56 * 1024 * 1024


def _params(n_axes, vmem=VMEM_LIMIT):
    return pltpu.CompilerParams(dimension_semantics=("arbitrary",) * n_axes, vmem_limit_bytes=vmem)


def _dot(a, b, precision=None):
    return jnp.dot(a, b, preferred_element_type=F32, precision=precision)


def _dot_t(a, b, precision=None):
    return lax.dot_general(a, b, (((1,), (1,)), ((), ())), preferred_element_type=F32, precision=precision)


def _dot_0(a, b, precision=None):
    return lax.dot_general(a, b, (((0,), (0,)), ((), ())), preferred_element_type=F32, precision=precision)


def _iota(shape, dim):
    return lax.broadcasted_iota(jnp.int32, shape, dim)


def _rms(x, w):
    return x * lax.rsqrt(jnp.mean(x * x, -1, keepdims=True) + EPS) * w


def _silu(x):
    return x * jax.nn.sigmoid(x)


def _ada_kernel(c_ref, w_ref, b_ref, o_ref):
    s = _silu(c_ref[...]).astype(BF16)
    o_ref[...] = _dot(s, w_ref[...].astype(BF16)) + b_ref[...]


def _ada(c_all, w_ada, b_ada3, layer):
    rows = c_all.shape[0]
    n_out = w_ada.shape[-1]
    tn = 1024
    return pl.pallas_call(
        _ada_kernel,
        grid=(n_out // tn,),
        in_specs=[
            pl.BlockSpec((rows, D_MODEL), lambda n: (0, 0)),
            pl.BlockSpec((None, D_MODEL, tn), lambda n: (layer, 0, n)),
            pl.BlockSpec((None, 1, tn), lambda n: (layer, 0, n)),
        ],
        out_specs=pl.BlockSpec((rows, tn), lambda n: (0, n)),
        out_shape=jax.ShapeDtypeStruct((rows, n_out), F32),
        compiler_params=_params(1),
        name="ada",
    )(c_all, w_ada, b_ada3)


class _Mod:
    def __init__(self, arr, per_row, rows_per_batch, tm):
        self.arr = arr
        self.per_row = per_row
        self.tiles_per_batch = None if per_row else rows_per_batch // tm
        self.tm = tm

    def spec(self, j):
        if self.per_row:
            return pl.BlockSpec((None, self.tm, D_MODEL), lambda r, *_: (j, r, 0))
        tpb = self.tiles_per_batch
        return pl.BlockSpec((None, None, 1, D_MODEL), lambda r, *_: (r // tpb, j, 0, 0))


def _ffn_kernel(x_ref, nw_ref, sh_ref, sc_ref, g_ref, w1_ref, w3_ref, w2_ref, o_ref, *rest, nf, emit_bf16):
    f = pl.program_id(1)
    h_ref, acc_ref = rest[-2:]

    @pl.when(f == 0)
    def _():
        xn = _rms(x_ref[...], nw_ref[...])
        h_ref[...] = (xn * (1 + sc_ref[...]) + sh_ref[...]).astype(BF16)
        acc_ref[...] = jnp.zeros_like(acc_ref)

    w1, w3, w2 = (w[...].astype(BF16) for w in (w1_ref, w3_ref, w2_ref))
    if emit_bf16:
        for dst, w in zip(rest[:3], (w1, w3, w2)):
            dst[...] = w
    h = h_ref[...]
    a = _dot(h, w1)
    b = _dot(h, w3)
    acc_ref[...] += _dot((_silu(a) * b).astype(BF16), w2)

    @pl.when(f == nf - 1)
    def _():
        o_ref[...] = x_ref[...] + FFN_RES * g_ref[...] * acc_ref[...]


def _ffn(x, mod, j0, nw4, w1, w3, w2, layer, slot, tm, tf):
    rows = x.shape[0]
    nf = D_FF // tf
    stacked = w1.ndim == 4
    if stacked:
        up = pl.BlockSpec((None, None, D_MODEL, tf), lambda r, f: (layer, slot, 0, f))
        down = pl.BlockSpec((None, None, tf, D_MODEL), lambda r, f: (layer, slot, f, 0))
    else:
        up = pl.BlockSpec((D_MODEL, tf), lambda r, f: (0, f))
        down = pl.BlockSpec((tf, D_MODEL), lambda r, f: (f, 0))
    out_specs = [pl.BlockSpec((tm, D_MODEL), lambda r, f: (r, 0))]
    out_shape = [jax.ShapeDtypeStruct((rows, D_MODEL), F32)]
    if stacked:
        assert rows == tm
        out_specs += [pl.BlockSpec((D_MODEL, tf), lambda r, f: (0, f))] * 2 + [pl.BlockSpec((tf, D_MODEL), lambda r, f: (f, 0))]
        out_shape += [jax.ShapeDtypeStruct((D_MODEL, D_FF), BF16)] * 2 + [jax.ShapeDtypeStruct((D_FF, D_MODEL), BF16)]
    outs = pl.pallas_call(
        functools.partial(_ffn_kernel, nf=nf, emit_bf16=stacked),
        grid=(rows // tm, nf),
        in_specs=[
            pl.BlockSpec((tm, D_MODEL), lambda r, f: (r, 0)),
            pl.BlockSpec((None, None, 1, D_MODEL), lambda r, f: (layer, 2 * slot, 0, 0)),
            mod.spec(j0), mod.spec(j0 + 1), mod.spec(j0 + 2),
            up, up, down,
        ],
        out_specs=out_specs,
        out_shape=out_shape,
        scratch_shapes=[pltpu.VMEM((tm, D_MODEL), BF16), pltpu.VMEM((tm, D_MODEL), F32)],
        compiler_params=_params(2),
        name="ffn",
    )(x, nw4, mod.arr, mod.arr, mod.arr, w1, w3, w2)
    return outs if stacked else outs[0]


def _inproj_kernel(x_ref, nw_ref, sh_ref, sc_ref, w_ref, o_ref, h_ref):
    @pl.when(pl.program_id(1) == 0)
    def _():
        xn = _rms(x_ref[...], nw_ref[...])
        h_ref[...] = (xn * (1 + sc_ref[...]) + sh_ref[...]).astype(BF16)

    o_ref[...] = _dot(h_ref[...], w_ref[...])


def _inproj(x, mod, nw4, w_in, layer, tm, tn):
    rows = x.shape[0]
    return pl.pallas_call(
        _inproj_kernel,
        grid=(rows // tm, COLS // tn),
        in_specs=[
            pl.BlockSpec((tm, D_MODEL), lambda r, n: (r, 0)),
            pl.BlockSpec((None, None, 1, D_MODEL), lambda r, n: (layer, 1, 0, 0)),
            mod.spec(3), mod.spec(4),
            pl.BlockSpec((None, D_MODEL, tn), lambda r, n: (layer, 0, n)),
        ],
        out_specs=pl.BlockSpec((tm, tn), lambda r, n: (r, n)),
        out_shape=jax.ShapeDtypeStruct((rows, COLS), F32),
        scratch_shapes=[pltpu.VMEM((tm, D_MODEL), BF16)],
        compiler_params=_params(2),
        name="inproj",
    )(x, nw4, mod.arr, mod.arr, w_in)


def _win_prep_kernel(w_ref, o_ref):
    head = C_BC + GROUP_WIDTH
    rows = w_ref.shape[0]
    o_ref[:, 0:head] = w_ref[:, 0:head].astype(BF16)
    o_ref[:, head:C_DT] = w_ref[:, head + SSD_HEADS:IN_COLS].astype(BF16)
    dt_block = w_ref[:, head:head + LANE]
    dt_block = jnp.where(_iota((rows, LANE), 1) < SSD_HEADS, dt_block, 0.0)
    o_ref[:, C_DT:C_DT + LANE] = dt_block.astype(BF16)
    o_ref[:, C_DT + LANE:COLS] = jnp.zeros((rows, COLS - C_DT - LANE), BF16)


def _win_prep(w_in):
    tr = 256
    return pl.pallas_call(
        _win_prep_kernel,
        grid=(DEPTH, D_MODEL // tr),
        in_specs=[pl.BlockSpec((None, tr, IN_COLS), lambda i, r: (i, r, 0))],
        out_specs=pl.BlockSpec((None, tr, COLS), lambda i, r: (i, r, 0)),
        out_shape=jax.ShapeDtypeStruct((DEPTH, D_MODEL, COLS), BF16),
        compiler_params=_params(2),
        name="w_in_prep",
    )(w_in)


def _outproj_kernel(x_ref, g_ref, y0_ref, y1_ref, y2_ref, y3_ref, w_ref, o_ref):
    G = GROUP_WIDTH
    acc = _dot(y0_ref[...], w_ref[0:G, :])
    acc += _dot(y1_ref[...], w_ref[G:2 * G, :])
    acc += _dot(y2_ref[...], w_ref[2 * G:3 * G, :])
    acc += _dot(y3_ref[...], w_ref[3 * G:4 * G, :])
    o_ref[...] = x_ref[...] + g_ref[...] * acc


def _outproj(x, mod, ys, w_out, layer, tm):
    rows = x.shape[0]
    yspec = pl.BlockSpec((tm, GROUP_WIDTH), lambda r: (r, 0))
    return pl.pallas_call(
        _outproj_kernel,
        grid=(rows // tm,),
        in_specs=[
            pl.BlockSpec((tm, D_MODEL), lambda r: (r, 0)),
            mod.spec(5), yspec, yspec, yspec, yspec,
            pl.BlockSpec((None, D_MODEL, D_MODEL), lambda r: (layer, 0, 0)),
        ],
        out_specs=pl.BlockSpec((tm, D_MODEL), lambda r: (r, 0)),
        out_shape=jax.ShapeDtypeStruct((rows, D_MODEL), F32),
        compiler_params=_params(1),
        name="outproj",
    )(x, mod.arr, *ys, w_out)


def _softplus(x):
    return jnp.maximum(x, 0.0) + jnp.log1p(jnp.exp(-jnp.abs(x)))


def _ssd_kernel(z_ref, x_ref, bc_ref, dt_ref, prev_ref, cw_ref, cb_ref, dtb_ref, alog_ref, dskip_ref, nw_ref,
                h0_ref, y_ref, hout_ref, ext_ref, st_ref, *, C, nc, t_valid):
    c = pl.program_id(1)
    G = GROUP_WIDTH

    @pl.when(c == 0)
    def _():
        ext_ref[0:8, :] = prev_ref[...]
        st_ref[...] = h0_ref[...]

    ext_ref[8:8 + C, 0:G] = x_ref[...]
    ext_ref[8:8 + C, G:2 * G] = bc_ref[...]
    conv = cb_ref[...] + ext_ref[5:5 + C, :] * cw_ref[0:1, :]
    for i in range(1, SSD_CONV):
        conv = conv + ext_ref[5 + i:5 + i + C, :] * cw_ref[i:i + 1, :]
    ext_ref[0:8, :] = ext_ref[C:C + 8, :]
    xbc = _silu(conv)
    xs = xbc[:, 0:G]

    dt = _softplus(dt_ref[...] + dtb_ref[...])
    if t_valid is not None:
        dt = jnp.where(c * C + _iota((C, LANE), 0) < t_valid, dt, 0.0)
    a = dt * (-jnp.exp(alog_ref[...]))
    tri = _iota((C, C), 0) >= _iota((C, C), 1)
    tri_b = tri.astype(BF16)
    a_hi, a_mid, a_lo = _split3(a)
    a_cs = _dg(tri_b, a_hi, _NN) + (_dg(tri_b, a_mid, _NN) + _dg(tri_b, a_lo, _NN))
    a_cs_t = a_cs.T
    lane_lo = _iota((C, LANE), 1) < SSD_HEADDIM
    row_lo = _iota((LANE, 1), 0) < SSD_HEADDIM
    col = lambda h: a_cs[:, h:h + 1]
    row = lambda h: a_cs_t[h:h + 1, :]
    last = lambda h: a_cs[C - 1:C, h:h + 1]
    by_head = lambda p, f: jnp.where(lane_lo, f(2 * p), f(2 * p + 1))
    decay_mat = lambda h: jnp.exp(jnp.where(tri, col(h) - row(h), -jnp.inf))

    n_groups = 2
    heads_per_group = SSD_HEADS // n_groups
    bm_s = [_split2(xbc[:, G + g * SSD_STATE:G + (g + 1) * SSD_STATE]) for g in range(n_groups)]
    cm_s = [_split2(xbc[:, G + (n_groups + g) * SSD_STATE:G + (n_groups + g + 1) * SSD_STATE]) for g in range(n_groups)]
    scores = [_dot3(cm_s[g], bm_s[g], _NT) for g in range(n_groups)]
    group = lambda p: 2 * p // heads_per_group
    each = lambda f: [f(p) for p in range(SSD_HEADS // 2)]
    m0_s = each(lambda p: _split2(scores[group(p)] * decay_mat(2 * p)))
    m1_s = each(lambda p: _split2(scores[group(p)] * decay_mat(2 * p + 1)))
    xs_p = each(lambda p: xs[:, p * LANE:(p + 1) * LANE])
    xdt = each(lambda p: xs_p[p] * by_head(p, lambda h: dt[:, h:h + 1]))
    xdt_s = each(lambda p: _split2(xdt[p]))
    y_diag = each(lambda p: jnp.where(lane_lo, _dot3(m0_s[p], xdt_s[p]), _dot3(m1_s[p], xdt_s[p])))
    st = each(lambda p: st_ref[p * LANE:(p + 1) * LANE, :])
    y_off = each(lambda p: _dot3(cm_s[group(p)], _split2(st[p]), _NT) * by_head(p, lambda h: jnp.exp(col(h))))
    decay = each(lambda p: by_head(p, lambda h: jnp.exp(last(h) - col(h))))
    new = each(lambda p: _dot3(_split2(xdt[p] * decay[p]), bm_s[group(p)], _TN))
    for p in range(SSD_HEADS // 2):
        keep = jnp.where(row_lo, jnp.exp(last(2 * p)), jnp.exp(last(2 * p + 1)))
        st_ref[p * LANE:(p + 1) * LANE, :] = st[p] * keep + new[p]
    ys = each(lambda p: y_diag[p] + y_off[p] + xs_p[p] * dskip_ref[:, p * LANE:(p + 1) * LANE])

    y = jnp.concatenate(ys, axis=1) * _silu(z_ref[...])
    y_ref[...] = _rms(y, nw_ref[...]).astype(y_ref.dtype)

    @pl.when(c == nc - 1)
    def _():
        hout_ref[...] = st_ref[...]


def _ssd(cols3, prev8, h0, cw8, cb, dtb, alog, dskip, nw, t_valid):
    B, T, _ = cols3.shape
    C = SSD_CHUNK
    nc = T // C
    G = GROUP_WIDTH
    vec = lambda n: pl.BlockSpec((1, n), lambda b, c: (0, 0))
    return pl.pallas_call(
        functools.partial(_ssd_kernel, C=C, nc=nc, t_valid=t_valid),
        grid=(B, nc),
        in_specs=[
            pl.BlockSpec((None, C, G), lambda b, c: (b, c, C_Z // G)),
            pl.BlockSpec((None, C, G), lambda b, c: (b, c, C_X // G)),
            pl.BlockSpec((None, C, G), lambda b, c: (b, c, C_BC // G)),
            pl.BlockSpec((None, C, LANE), lambda b, c: (b, c, C_DT // LANE)),
            pl.BlockSpec((None, 8, SSD_CONV_DIM), lambda b, c: (b, 0, 0)),
            pl.BlockSpec((8, SSD_CONV_DIM), lambda b, c: (0, 0)),
            vec(SSD_CONV_DIM), vec(LANE), vec(LANE), vec(G), vec(G),
            pl.BlockSpec((None, SSD_HEADS * SSD_HEADDIM, SSD_STATE), lambda b, c: (b, 0, 0)),
        ],
        out_specs=[
            pl.BlockSpec((None, C, G), lambda b, c: (b, c, 0)),
            pl.BlockSpec((None, SSD_HEADS * SSD_HEADDIM, SSD_STATE), lambda b, c: (b, 0, 0)),
        ],
        out_shape=[
            jax.ShapeDtypeStruct((B, T, G), BF16),
            jax.ShapeDtypeStruct((B, SSD_HEADS * SSD_HEADDIM, SSD_STATE), F32),
        ],
        scratch_shapes=[pltpu.VMEM((C + 8, SSD_CONV_DIM), F32), pltpu.VMEM((SSD_HEADS * SSD_HEADDIM, SSD_STATE), F32)],
        compiler_params=_params(2),
        name="ssd",
    )(cols3, cols3, cols3, cols3, prev8, cw8, cb, dtb, alog, dskip, nw, h0)


def _gmlp_kernel(u_ref, v_ref, vw_ref, ws_ref, bs_ref, ow_ref, y_ref, vout_ref, *, C):
    u = jax.nn.gelu(u_ref[...])
    v = _rms(jax.nn.gelu(v_ref[...]), vw_ref[...])
    vout_ref[...] = v
    tri = _iota((C, C), 0) >= _iota((C, C), 1)
    mixed = []
    for h in range(GMLP_HEADS):
        ws = jnp.where(tri, ws_ref[h], 0.0)
        mixed.append(_dot3(_split2(ws), _split2(v[:, h * LANE:(h + 1) * LANE])) + bs_ref[h])
    y = u * jnp.concatenate(mixed, axis=1)
    y_ref[...] = _rms(y, ow_ref[...]).astype(y_ref.dtype)


def _gmlp(cols3, vw, ws, bs, ow):
    B, T, _ = cols3.shape
    C = GMLP_CHUNK
    G = GROUP_WIDTH
    vec = pl.BlockSpec((1, G), lambda b, c: (0, 0))
    return pl.pallas_call(
        functools.partial(_gmlp_kernel, C=C),
        grid=(B, T // C),
        in_specs=[
            pl.BlockSpec((None, C, G), lambda b, c: (b, c, C_GU // G)),
            pl.BlockSpec((None, C, G), lambda b, c: (b, c, C_GV // G)),
            vec,
            pl.BlockSpec((GMLP_HEADS, C, C), lambda b, c: (0, 0, 0)),
            pl.BlockSpec((GMLP_HEADS, C, 1), lambda b, c: (0, 0, 0)),
            vec,
        ],
        out_specs=[pl.BlockSpec((None, C, G), lambda b, c: (b, c, 0))] * 2,
        out_shape=[jax.ShapeDtypeStruct((B, T, G), BF16), jax.ShapeDtypeStruct((B, T, G), F32)],
        compiler_params=_params(2),
        name="gmlp",
    )(cols3, cols3, vw, ws, bs, ow)


_NN = ((1,), (0,))
_NT = ((1,), (1,))
_TN = ((0,), (0,))


def _split2(x):
    hi = x.astype(BF16)
    return hi, (x - hi.astype(F32)).astype(BF16)


def _split3(x):
    hi = x.astype(BF16)
    rest = x - hi.astype(F32)
    mid = rest.astype(BF16)
    return hi, mid, (rest - mid.astype(F32)).astype(BF16)


def _dg(a, b, dims):
    return lax.dot_general(a, b, (dims, ((), ())), preferred_element_type=F32)


def _dot3(a, b, dims=_NN):
    (ah, al), (bh, bl) = a, b
    return _dg(ah, bh, dims) + (_dg(ah, bl, dims) + _dg(al, bh, dims))


def _block_diag(pieces, same_head):
    return tuple(jnp.where(same_head, jnp.concatenate([x, x], axis=0), jnp.zeros((), x.dtype)) for x in pieces)


def _rwkv_kernel(r_ref, k_ref, v_ref, lo_ref, prev_ref, mu_ref, w0_ref, wup_ref, a0_ref, aup_ref, gup_ref,
                 kk_ref, ka_ref, rk_ref, lnw_ref, lnb_ref, s0_ref, y_ref, sout_ref, ext_ref, st_ref,
                 *, C, nc, t_valid):
    c = pl.program_id(1)
    G = GROUP_WIDTH
    N = RWKV_HEAD
    P = 2 * N

    n_batch = r_ref.shape[0]
    n_pairs = G // P
    sls = [slice(p * P, (p + 1) * P) for p in range(n_pairs)]
    same_head_n = (_iota((P, P), 0) < N) == (_iota((P, P), 1) < N)
    ones_bd = same_head_n.astype(BF16)
    tri_cc = (_iota((C, C), 0) >= _iota((C, C), 1)).astype(BF16)

    @pl.when(c == 0)
    def _():
        ext_ref[:, 0:8, :] = prev_ref[...]
        st_ref[...] = s0_ref[...]

    def head_sum(x):
        def pair_sum(p):
            hi, lo = _split2(x[:, sls[p]])
            return _dg(hi, ones_bd, _NN) + _dg(lo, ones_bd, _NN)
        return jnp.concatenate([pair_sum(p) for p in range(n_pairs)], axis=1)

    def chunk_inputs(n):
        ext_ref[n, 8:8 + C, 0:G] = r_ref[n]
        ext_ref[n, 8:8 + C, G:2 * G] = k_ref[n]
        ext_ref[n, 8:8 + C, 2 * G:3 * G] = v_ref[n]
        ext_ref[n, 8:8 + C, 3 * G:RWKV_COLS] = lo_ref[n]
        cur = ext_ref[n, 8:8 + C, :]
        prev = ext_ref[n, 7:7 + C, :]
        xs = cur + (prev - cur) * mu_ref[...]
        ext_ref[n, 0:8, :] = ext_ref[n, C:C + 8, :]

        r, k, v = xs[:, 0:G], xs[:, G:2 * G], xs[:, 2 * G:3 * G]
        la = xs[:, 3 * G:3 * G + P]
        gl = xs[:, 3 * G + P:RWKV_COLS]
        w_log = -RWKV_DECAY_SCALE * jax.nn.sigmoid(
            w0_ref[...] + _dot3(_split2(jnp.tanh(la)), _split2(wup_ref[...])))
        a = jax.nn.sigmoid(a0_ref[...] + _dot3(_split2(la), _split2(aup_ref[...])))
        g = _dot3(_split2(jax.nn.sigmoid(gl)), _split2(gup_ref[...]))

        kk = k * kk_ref[...]
        kk = kk * lax.rsqrt(jnp.maximum(head_sum(kk * kk), 1e-12))
        k2 = k * (1 + (a - 1) * ka_ref[...])
        if t_valid is not None:
            ok = c * C + _iota((C, G), 0) < t_valid
            w_log = jnp.where(ok, w_log, 0.0)
            kk = jnp.where(ok, kk, 0.0)
            k2 = jnp.where(ok, k2, 0.0)
        b = kk * a

        w_hi, w_mid, w_lo = _split3(w_log)
        cl = _dg(tri_cc, w_hi, _NN) + (_dg(tri_cc, w_mid, _NN) + _dg(tri_cc, w_lo, _NN))
        cl_last = cl[C - 1:C, :]
        einv = jnp.exp(-cl)
        e_c = jnp.exp(cl_last - cl)
        return dict(r=r, v=v, k2=k2, g=g, kkp=kk * jnp.exp(cl - w_log), rp=r * jnp.exp(cl), bi=b * einv,
                    ki=k2 * einv, bt=b * e_c, kt=k2 * e_c, p_c=jnp.exp(cl_last))

    seqs = [chunk_inputs(n) for n in range(n_batch)]
    units = [(n, p) for n in range(n_batch) for p in range(n_pairs)]
    each = lambda f: [f(u) for u in range(len(units))]
    tile = lambda name: each(lambda u: seqs[units[u][0]][name][:, sls[units[u][1]]])
    kkp, rp, bi, ki, bt, kt, v_p = (tile(name) for name in ("kkp", "rp", "bi", "ki", "bt", "kt", "v"))

    t_i = _iota((C, P), 0)
    s_i = _iota((C, P), 1) % C
    strict = s_i < t_i
    incl = s_i <= t_i
    diag_blk = (s_i // RWKV_SUB) == (t_i // RWKV_SUB)
    eye = (s_i == t_i).astype(F32)
    same_head = (_iota((P, P), 0) < C) == (_iota((P, P), 1) < C)

    memo = {}

    def sp(x):
        if id(x) not in memo:
            memo[id(x)] = (x, _split2(x))
        return memo[id(x)][1]

    def bd(x):
        if ("bd", id(x)) not in memo:
            memo["bd", id(x)] = (x, _block_diag(sp(x), same_head))
        return memo["bd", id(x)][1]

    def mm(xs, ys):
        return each(lambda p: _dot3(sp(xs[p]), bd(ys[p])))

    kkp_s = each(lambda p: _split2(kkp[p]))
    rp_s = each(lambda p: _split2(rp[p]))
    lhs_s = each(lambda p: tuple(jnp.concatenate([x, y], axis=0) for x, y in zip(kkp_s[p], rp_s[p])))
    ab = each(lambda p: _dot3(lhs_s[p], _block_diag(_split2(bi[p]), same_head_n), _NT))
    ak = each(lambda p: _dot3(lhs_s[p], _block_diag(_split2(ki[p]), same_head_n), _NT))
    a_m = each(lambda p: jnp.where(strict, ab[p][0:C], 0.0))
    b_k = each(lambda p: jnp.where(strict, ak[p][0:C], 0.0))
    r_b = each(lambda p: jnp.where(incl, ab[p][C:2 * C], 0.0))
    r_k = each(lambda p: jnp.where(incl, ak[p][C:2 * C], 0.0))

    n_pow = each(lambda p: jnp.where(diag_blk, -a_m[p], 0.0))
    a_o = each(lambda p: jnp.where(diag_blk, 0.0, a_m[p]))
    t_d = each(lambda p: eye + n_pow[p])
    for _ in range(int(math.log2(RWKV_SUB)) - 1):
        n_pow = mm(n_pow, n_pow)
        step = mm(t_d, n_pow)
        t_d = each(lambda p: t_d[p] + step[p])
    m1 = mm(t_d, a_o)
    m2 = mm(m1, m1)
    im = each(lambda p: eye - m1[p])
    im_m2 = mm(im, m2)
    t_full = mm(each(lambda p: im[p] + im_m2[p]), t_d)

    st = each(lambda u: st_ref[units[u]])
    st_s = each(lambda p: _split2(st[p]))
    bkv = mm(b_k, v_p)
    rhs = each(lambda p: _dot3(kkp_s[p], st_s[p], _NT) + bkv[p])
    u = mm(t_full, rhs)
    rkv = mm(r_k, v_p)
    rbu = mm(r_b, u)
    ys = each(lambda p: _dot3(rp_s[p], st_s[p], _NT) + rkv[p] - rbu[p])
    upd = each(lambda p: _dot3(_split2(jnp.concatenate([v_p[p], -u[p]], axis=0)),
                               _split2(jnp.concatenate([kt[p], bt[p]], axis=0)), _TN))
    for u_i, (n, p) in enumerate(units):
        st_ref[n, p] = st[u_i] * seqs[n]["p_c"][:, sls[p]] + jnp.where(same_head_n, upd[u_i], 0.0)

    for n, seq in enumerate(seqs):
        y = jnp.concatenate(ys[n * n_pairs:(n + 1) * n_pairs], axis=1)
        mean = head_sum(y) * (1.0 / N)
        d = y - mean
        var = head_sum(d * d) * (1.0 / N)
        yn = d * lax.rsqrt(var + RWKV_LN_EPS) * lnw_ref[...] + lnb_ref[...]
        bonus = head_sum(seq["r"] * seq["k2"] * rk_ref[...]) * seq["v"]
        y_ref[n] = ((yn + bonus) * seq["g"]).astype(y_ref.dtype)

    @pl.when(c == nc - 1)
    def _():
        sout_ref[...] = st_ref[...]


def _rwkv(cols3, prev8, s0_bd, mu, w0, wup, a0, aup, gup, kk, ka, rk, lnw, lnb, t_valid):
    B, T, _ = cols3.shape
    C = RWKV_CHUNK
    nc = T // C
    G = GROUP_WIDTH
    P = 2 * RWKV_HEAD
    n_pairs = RWKV_HEADS // 2
    S = RWKV_SEQS_PER_STEP
    vec = lambda n: pl.BlockSpec((1, n), lambda b, c: (0, 0))
    mat = lambda m, n: pl.BlockSpec((m, n), lambda b, c: (0, 0))
    return pl.pallas_call(
        functools.partial(_rwkv_kernel, C=C, nc=nc, t_valid=t_valid),
        grid=(B // S, nc),
        in_specs=[
            pl.BlockSpec((S, C, G), lambda b, c: (b, c, C_RR // G)),
            pl.BlockSpec((S, C, G), lambda b, c: (b, c, C_RK // G)),
            pl.BlockSpec((S, C, G), lambda b, c: (b, c, C_RV // G)),
            pl.BlockSpec((S, C, 2 * P), lambda b, c: (b, c, C_RL // (2 * P))),
            pl.BlockSpec((S, 8, RWKV_COLS), lambda b, c: (b, 0, 0)),
            vec(RWKV_COLS), vec(G), mat(P, G), vec(G), mat(P, G), mat(P, G),
            vec(G), vec(G), vec(G), vec(G), vec(G),
            pl.BlockSpec((S, n_pairs, P, P), lambda b, c: (b, 0, 0, 0)),
        ],
        out_specs=[
            pl.BlockSpec((S, C, G), lambda b, c: (b, c, 0)),
            pl.BlockSpec((S, n_pairs, P, P), lambda b, c: (b, 0, 0, 0)),
        ],
        out_shape=[
            jax.ShapeDtypeStruct((B, T, G), BF16),
            jax.ShapeDtypeStruct((B, n_pairs, P, P), F32),
        ],
        scratch_shapes=[pltpu.VMEM((S, C + 8, RWKV_COLS), F32), pltpu.VMEM((S, n_pairs, P, P), F32)],
        compiler_params=_params(2),
        name="rwkv",
    )(cols3, cols3, cols3, cols3, prev8, mu, w0, wup, a0, aup, gup, kk, ka, rk, lnw, lnb, s0_bd)


def _moba_prep_kernel(q_ref, k_ref, v_ref, cos_ref, sin_ref, qw_ref, kw_ref, qo_ref, ko_ref, k4_ref, v4_ref):
    cos, sin = cos_ref[...], sin_ref[...]

    def rotate(x, w_ref):
        xn = _rms(x, w_ref[...])
        return xn * cos + pltpu.roll(xn, MOBA_HEAD_DIM // 2, 1) * sin

    for h in range(MOBA_HEADS):
        sl = slice(h * LANE, (h + 1) * LANE)
        qo_ref[:, sl] = rotate(q_ref[:, sl], qw_ref)
        k_h = rotate(k_ref[:, sl], kw_ref)
        ko_ref[:, sl] = k_h
        k4_ref[:, h, :] = k_h
        v4_ref[:, h, :] = v_ref[:, sl]


def _moba_prep(cols, cos, sin, qw, kw, tm):
    rows = cols.shape[0]
    G = GROUP_WIDTH
    tab = pl.BlockSpec((tm, LANE), lambda r: (r, 0))
    vec = pl.BlockSpec((1, LANE), lambda r: (0, 0))
    flat = pl.BlockSpec((tm, G), lambda r: (r, 0))
    heads = pl.BlockSpec((tm, MOBA_HEADS, MOBA_HEAD_DIM), lambda r: (r, 0, 0))
    return pl.pallas_call(
        _moba_prep_kernel,
        grid=(rows // tm,),
        in_specs=[
            pl.BlockSpec((tm, G), lambda r: (r, C_MQ // G)),
            pl.BlockSpec((tm, G), lambda r: (r, C_MK // G)),
            pl.BlockSpec((tm, G), lambda r: (r, C_MV // G)),
            tab, tab, vec, vec,
        ],
        out_specs=[flat, flat, heads, heads],
        out_shape=[jax.ShapeDtypeStruct((rows, G), F32)] * 2
        + [jax.ShapeDtypeStruct((rows, MOBA_HEADS, MOBA_HEAD_DIM), F32)] * 2,
        compiler_params=_params(1),
        name="moba_prep",
    )(cols, cols, cols, cos, sin, qw, kw)


def _top3(gate, lane_f):
    sel = jnp.zeros(gate.shape, jnp.bool_)
    g = gate
    big = float(gate.shape[-1])
    idxs = []
    for _ in range(MOBA_TOPK):
        m = jnp.max(g, axis=-1, keepdims=True)
        idx = jnp.min(jnp.where(g == m, lane_f, big), axis=-1, keepdims=True)
        pick = lane_f == idx
        sel = sel | pick
        g = jnp.where(pick, -jnp.inf, g)
        idxs.append(idx)
    return sel, idxs


def _moba_attn_kernel(q_ref, k_ref, v_ref, ow_ref, y_ref, km_ref, kb_ref, vt_ref, sel_ref, o_ref, *, nb):
    qi = pl.program_id(1)
    BLK = MOBA_BLOCK
    scale = MOBA_HEAD_DIM ** -0.5

    @pl.when(qi == 0)
    def _():
        for j in range(nb):
            rows = slice(j * BLK, (j + 1) * BLK)
            k_blk = k_ref[rows, :]
            km_ref[j:j + 1, :] = jnp.mean(k_blk, axis=0, keepdims=True)
            kb_ref[rows, :] = k_blk.astype(BF16)
            vt_ref[:, rows] = v_ref[rows, :].T.astype(BF16)

    blk_i = _iota((nb, BLK), 0)
    blk_f = blk_i.astype(F32)
    past = blk_i < qi
    causal = _iota((BLK, BLK), 0) <= _iota((BLK, BLK), 1)
    own = pl.multiple_of(qi * BLK, BLK)

    sls = [slice(h * LANE, (h + 1) * LANE) for h in range(MOBA_HEADS)]
    each = lambda f: [f(h) for h in range(MOBA_HEADS)]
    q_t = each(lambda h: q_ref[:, sls[h]].T)
    gate = each(lambda h: jnp.where(past, _dot3(_split2(km_ref[:, sls[h]]), _split2(q_t[h])), NEG))
    sel = each(lambda h: jnp.zeros((nb, BLK), jnp.bool_))
    for _ in range(MOBA_TOPK):
        best = each(lambda h: jnp.max(gate[h], axis=0, keepdims=True))
        idx = each(lambda h: jnp.min(jnp.where(gate[h] == best[h], blk_f, float(nb)), axis=0, keepdims=True))
        sel = each(lambda h: sel[h] | (blk_f == idx[h]))
        gate = each(lambda h: jnp.where(blk_f == idx[h], -jnp.inf, gate[h]))
    for h in range(MOBA_HEADS):
        sel_ref[h] = (sel[h] & past).astype(F32)
    qb_t = each(lambda h: q_t[h].astype(BF16))

    def block_scores(start, h):
        return _dot(kb_ref[pl.ds(start, BLK), sls[h]], qb_t[h]) * scale

    s0 = each(lambda h: jnp.where(causal, block_scores(own, h), NEG))
    m0 = each(lambda h: jnp.max(s0[h], axis=0, keepdims=True))
    p0 = each(lambda h: jnp.exp(s0[h] - m0[h]))
    l0 = each(lambda h: jnp.sum(p0[h], axis=0, keepdims=True))
    acc0 = each(lambda h: _dot(vt_ref[sls[h], pl.ds(own, BLK)], p0[h].astype(BF16)))

    def body(j, carry):
        m_i, l_i, acc = carry
        start = pl.multiple_of(j * BLK, BLK)
        s = each(lambda h: jnp.where(sel_ref[h, pl.ds(j, 1), :] > 0.0, block_scores(start, h), NEG))
        m_n = each(lambda h: jnp.maximum(m_i[h], jnp.max(s[h], axis=0, keepdims=True)))
        alpha = each(lambda h: jnp.exp(m_i[h] - m_n[h]))
        p = each(lambda h: jnp.exp(s[h] - m_n[h]))
        l_n = each(lambda h: alpha[h] * l_i[h] + jnp.sum(p[h], axis=0, keepdims=True))
        acc_n = each(lambda h: alpha[h] * acc[h] + _dot(vt_ref[sls[h], pl.ds(start, BLK)], p[h].astype(BF16)))
        return tuple(m_n), tuple(l_n), tuple(acc_n)

    _, l_f, acc_f = lax.fori_loop(0, qi, body, (tuple(m0), tuple(l0), tuple(acc0)))
    for h in range(MOBA_HEADS):
        o_ref[:, sls[h]] = (acc_f[h] / l_f[h]).T

    y_ref[...] = _rms(o_ref[...], ow_ref[...]).astype(y_ref.dtype)


def _moba_attn(q3, k3, cols3, ow):
    B, T, G = q3.shape
    BLK = MOBA_BLOCK
    nb = T // BLK
    return pl.pallas_call(
        functools.partial(_moba_attn_kernel, nb=nb),
        grid=(B, nb),
        in_specs=[
            pl.BlockSpec((None, BLK, G), lambda b, i: (b, i, 0)),
            pl.BlockSpec((None, T, G), lambda b, i: (b, 0, 0)),
            pl.BlockSpec((None, T, G), lambda b, i: (b, 0, C_MV // G)),
            pl.BlockSpec((1, G), lambda b, i: (0, 0)),
        ],
        out_specs=pl.BlockSpec((None, BLK, G), lambda b, i: (b, i, 0)),
        out_shape=jax.ShapeDtypeStruct((B, T, G), BF16),
        scratch_shapes=[pltpu.VMEM((nb, G), F32), pltpu.VMEM((T, G), BF16), pltpu.VMEM((G, T), BF16),
                        pltpu.VMEM((MOBA_HEADS, nb, BLK), F32), pltpu.VMEM((BLK, G), F32)],
        compiler_params=_params(2),
        name="moba_attn",
    )(q3, k3, cols3, ow)


PAGES_PER_BLOCK = MOBA_BLOCK // PAGE_SIZE
KMEAN_BLOCKS = 8


def _kmean_kernel(pt_ref, *refs):
    page_refs, o_ref = refs[:-1], refs[-1]
    for j in range(KMEAN_BLOCKS):
        s = jnp.sum(page_refs[PAGES_PER_BLOCK * j][...], axis=0)
        for o in range(1, PAGES_PER_BLOCK):
            s = s + jnp.sum(page_refs[PAGES_PER_BLOCK * j + o][...], axis=0)
        o_ref[j] = s * (1.0 / MOBA_BLOCK)


def _kmean_pages(cache_k, page_table_flat, layer, batch, n_pages):
    nbk = n_pages // PAGES_PER_BLOCK
    per_step = KMEAN_BLOCKS * PAGES_PER_BLOCK

    def page(o):
        return pl.BlockSpec((None, None, PAGE_SIZE, MOBA_HEADS, MOBA_HEAD_DIM),
                            lambda b, j, pt: (layer, pt[b * n_pages + j * per_step + o], 0, 0, 0))

    return pl.pallas_call(
        _kmean_kernel,
        grid_spec=pltpu.PrefetchScalarGridSpec(
            num_scalar_prefetch=1,
            grid=(batch, nbk // KMEAN_BLOCKS),
            in_specs=[page(o) for o in range(per_step)],
            out_specs=pl.BlockSpec((None, KMEAN_BLOCKS, MOBA_HEADS, MOBA_HEAD_DIM), lambda b, j, pt: (b, j, 0, 0)),
        ),
        out_shape=jax.ShapeDtypeStruct((batch, nbk, MOBA_HEADS, MOBA_HEAD_DIM), F32),
        compiler_params=_params(2),
        name="moba_kmean_pages",
    )(page_table_flat, *([cache_k] * per_step))


def _select_kernel(q_ref, km_ref, idx_ref, *, batch, t_new, nbk):
    rows = batch * t_new
    lane_i = _iota((rows, LANE), 1)
    lane_f = lane_i.astype(F32)
    row_b = _iota((rows, LANE), 0) // t_new
    out = jnp.zeros((rows, LANE), F32)
    pad = jnp.zeros((LANE - nbk, LANE), F32)
    for h in range(MOBA_HEADS):
        sl = slice(h * LANE, (h + 1) * LANE)
        q = q_ref[:, sl]
        gate = jnp.zeros((rows, LANE), F32)
        for b in range(batch):
            km = jnp.concatenate([km_ref[b, :, h, :], pad], axis=0)
            gate = jnp.where(row_b == b, _dot_t(q, km, HI), gate)
        gate = jnp.where(lane_i < nbk, gate, NEG)
        _, idxs = _top3(gate, lane_f)
        for kth, idx in enumerate(idxs):
            out = jnp.where(lane_i == h * MOBA_HEADS + kth, idx, out)
    idx_ref[...] = out.astype(jnp.int32)


def _select(q_rot, kmean, batch, t_new):
    rows = batch * t_new
    nbk = kmean.shape[1]
    G = GROUP_WIDTH
    return pl.pallas_call(
        functools.partial(_select_kernel, batch=batch, t_new=t_new, nbk=nbk),
        grid=(1,),
        in_specs=[
            pl.BlockSpec((rows, G), lambda i: (0, 0)),
            pl.BlockSpec((batch, nbk, MOBA_HEADS, MOBA_HEAD_DIM), lambda i: (0, 0, 0, 0)),
        ],
        out_specs=pl.BlockSpec((rows, LANE), lambda i: (0, 0)),
        out_shape=jax.ShapeDtypeStruct((rows, LANE), jnp.int32),
        compiler_params=_params(1),
        name="moba_select",
    )(q_rot, kmean)


def _sample_attn_kernel(pt_ref, idx_ref, q_ref, kn_ref, vn_ref, ow_ref, ck_ref, cv_ref, y_ref,
                        kbuf, vbuf, sems, o_ref, *, layer, t_new, n_pages):
    b = pl.program_id(0)
    rows = kn_ref.shape[0]
    scale = MOBA_HEAD_DIM ** -0.5
    per_query = PAGES_PER_BLOCK * MOBA_TOPK
    queries = [(t, h) for t in range(t_new) for h in range(MOBA_HEADS)]

    def page_copies(qn):
        t, h = queries[qn]
        out = []
        for kth in range(MOBA_TOPK):
            blk = idx_ref[(b * t_new + t) * LANE + h * MOBA_HEADS + kth]
            for o in range(PAGES_PER_BLOCK):
                page = pt_ref[b * n_pages + PAGES_PER_BLOCK * blk + o]
                slot = qn * per_query + kth * PAGES_PER_BLOCK + o
                out.append(pltpu.make_async_copy(ck_ref.at[layer, page, :, h, :], kbuf.at[slot], sems.at[0, qn]))
                out.append(pltpu.make_async_copy(cv_ref.at[layer, page, :, h, :], vbuf.at[slot], sems.at[1, qn]))
        return out

    copies = [page_copies(qn) for qn in range(len(queries))]
    for group in copies:
        for cp in group:
            cp.start()

    r_i = _iota((8, rows), 1)
    for qn, (t, h) in enumerate(queries):
        sl = slice(h * LANE, (h + 1) * LANE)
        q8 = jnp.broadcast_to(q_ref[t:t + 1, sl], (8, LANE)).astype(BF16)
        s_new = _dot_t(q8, kn_ref[:, sl].astype(BF16)) * scale
        s_new = jnp.where((r_i >= b * t_new) & (r_i <= b * t_new + t), s_new, NEG)
        for cp in copies[qn]:
            cp.wait()
        slots = range(qn * per_query, (qn + 1) * per_query)
        s_old = [_dot_t(q8, kbuf[slot].astype(BF16)) * scale for slot in slots]
        m = jnp.max(s_new, axis=-1, keepdims=True)
        for s in s_old:
            m = jnp.maximum(m, jnp.max(s, axis=-1, keepdims=True))
        p_new = jnp.exp(s_new - m)
        l = jnp.sum(p_new, axis=-1, keepdims=True)
        acc = _dot(p_new.astype(BF16), vn_ref[:, sl].astype(BF16))
        for s, slot in zip(s_old, slots):
            p = jnp.exp(s - m)
            l = l + jnp.sum(p, axis=-1, keepdims=True)
            acc = acc + _dot(p.astype(BF16), vbuf[slot].astype(BF16))
        o_ref[t:t + 1, sl] = (acc / l)[0:1, :]

    y_ref[...] = _rms(o_ref[...], ow_ref[...]).astype(y_ref.dtype)


def _sample_attn(q_rot, k_new, cols, ow, cache_k, cache_v, page_table_flat, idx_flat, layer, batch, t_new, n_pages):
    rows = batch * t_new
    G = GROUP_WIDTH
    n_slots = t_new * MOBA_HEADS * MOBA_TOPK * PAGES_PER_BLOCK
    return pl.pallas_call(
        functools.partial(_sample_attn_kernel, layer=layer, t_new=t_new, n_pages=n_pages),
        grid_spec=pltpu.PrefetchScalarGridSpec(
            num_scalar_prefetch=2,
            grid=(batch,),
            in_specs=[
                pl.BlockSpec((None, t_new, G), lambda b, pt, idx: (b, 0, 0)),
                pl.BlockSpec((rows, G), lambda b, pt, idx: (0, 0)),
                pl.BlockSpec((rows, G), lambda b, pt, idx: (0, C_MV // G)),
                pl.BlockSpec((1, G), lambda b, pt, idx: (0, 0)),
                pl.BlockSpec(memory_space=pl.ANY),
                pl.BlockSpec(memory_space=pl.ANY),
            ],
            out_specs=pl.BlockSpec((None, t_new, G), lambda b, pt, idx: (b, 0, 0)),
            scratch_shapes=[
                pltpu.VMEM((n_slots, PAGE_SIZE, MOBA_HEAD_DIM), F32),
                pltpu.VMEM((n_slots, PAGE_SIZE, MOBA_HEAD_DIM), F32),
                pltpu.SemaphoreType.DMA((2, t_new * MOBA_HEADS)),
                pltpu.VMEM((t_new, G), F32),
            ],
        ),
        out_shape=jax.ShapeDtypeStruct((batch, t_new, G), BF16),
        compiler_params=_params(1),
        name="moba_sample_attn",
    )(page_table_flat, idx_flat, q_rot.reshape(batch, t_new, G), k_new, cols, ow, cache_k, cache_v)


def _rope_tables(pos):
    half = MOBA_HEAD_DIM // 2
    freq = ROPE_THETA ** (-jnp.arange(half, dtype=F32) / half)
    ang = pos.astype(F32)[:, None] * freq[None, :]
    cos, sin = jnp.cos(ang), jnp.sin(ang)
    return jnp.concatenate([cos, cos], -1), jnp.concatenate([-sin, sin], -1)


def _pad_rows_front(x, rows):
    return jnp.pad(x, ((0, 0), (rows - x.shape[1], 0), (0, 0)))


def _pair_block_diag(s):
    B, H, N, _ = s.shape
    s = s.reshape(B, H // 2, 2, N, 1, N) * jnp.eye(2, dtype=s.dtype)[None, None, :, None, :, None]
    return s.reshape(B, H // 2, 2 * N, 2 * N)


def _pair_diag_blocks(s_bd):
    B, n_pairs, P, _ = s_bd.shape
    N = P // 2
    s = s_bd.reshape(B, n_pairs, 2, N, 2, N)
    return jnp.stack([s[:, :, 0, :, 0, :], s[:, :, 1, :, 1, :]], axis=2).reshape(B, 2 * n_pairs, N, N)


def _layer_weights(W, i):
    G = GROUP_WIDTH
    row = lambda v: v.reshape(1, -1)
    lane_pad = lambda v: jnp.pad(v, (0, LANE - v.shape[0])).reshape(1, LANE)
    zeros_r = jnp.zeros((RWKV_HEAD, G), F32)
    return dict(
        cw8=jnp.pad(W['ssd_conv_w'][i].T, ((0, 8 - SSD_CONV), (0, 0))),
        cb=row(W['ssd_conv_b'][i]),
        dtb=lane_pad(W['ssd_dt_bias'][i]),
        alog=lane_pad(W['ssd_a_log'][i]),
        dskip=row(jnp.repeat(W['ssd_d'][i], GROUP_WIDTH // SSD_HEADS)),
        ssd_nw=row(W['ssd_norm_w'][i]),
        qw=row(W['moba_q_norm_w'][i]), kw=row(W['moba_k_norm_w'][i]), ow=row(W['moba_out_norm_w'][i]),
        gvw=row(W['gmlp_v_norm_w'][i]), gow=row(W['gmlp_out_norm_w'][i]),
        ws=W['gmlp_w_s'][i], bs=W['gmlp_b_s'][i][:, :, None],
        mu=row(W['rwkv_mu'][i]), w0=row(W['rwkv_w0'][i]), a0=row(W['rwkv_a0'][i]),
        wup=jnp.concatenate([W['rwkv_w_up'][i], zeros_r], 0),
        aup=jnp.concatenate([zeros_r, W['rwkv_a_up'][i]], 0),
        gup=W['rwkv_g_up'][i],
        kk=row(W['rwkv_k_k'][i]), ka=row(W['rwkv_k_a'][i]), rk=row(W['rwkv_r_k'][i]),
        lnw=row(W['rwkv_ln_w'][i]), lnb=row(W['rwkv_ln_b'][i]),
    )


def _mixers(cols, B, T, Tp, lw, conv_prev, ssd_prev, rwkv_prev, shift_prev, t_valid):
    G = GROUP_WIDTH
    cols3 = cols.reshape(B, T, COLS)
    if Tp != T:
        cols3 = jnp.pad(cols3, ((0, 0), (0, Tp - T), (0, 0)))
    y_ssd, ssd_new = _ssd(cols3, _pad_rows_front(conv_prev, 8), ssd_prev.reshape(B, SSD_HEADS * SSD_HEADDIM, SSD_STATE),
                          lw['cw8'], lw['cb'], lw['dtb'], lw['alog'], lw['dskip'], lw['ssd_nw'], t_valid)
    y_gm, v_gm = _gmlp(cols3, lw['gvw'], lw['ws'], lw['bs'], lw['gow'])
    y_rw, rwkv_new = _rwkv(cols3, _pad_rows_front(shift_prev[:, None, :], 8), _pair_block_diag(rwkv_prev),
                           lw['mu'], lw['w0'], lw['wup'], lw['a0'], lw['aup'], lw['gup'], lw['kk'], lw['ka'],
                           lw['rk'], lw['lnw'], lw['lnb'], t_valid)
    crop = lambda y: y[:, :T].reshape(B * T, G)
    raw = cols.reshape(B, T, COLS)
    conv_new = raw[:, T - (SSD_CONV - 1):, C_X:C_X + SSD_CONV_DIM]
    shift_new = raw[:, T - 1, C_RR:C_RR + RWKV_COLS]
    states = (ssd_new.reshape(B, SSD_HEADS, SSD_HEADDIM, SSD_STATE), conv_new, _pair_diag_blocks(rwkv_new), shift_new)
    return crop(y_ssd), crop(y_gm), crop(y_rw), v_gm[:, :T], states


def kernel(x_prompt, x_sample, c_prompt, c_sample, cache_k, cache_v, page_table, state_ssd, state_ssd_conv, state_rwkv, state_rwkv_shift, norm_w, w_ada, b_ada, ffn_w1, ffn_w3, ffn_w2, w_in, w_out, ssd_conv_w, ssd_conv_b, ssd_dt_bias, ssd_a_log, ssd_d, ssd_norm_w, moba_q_norm_w, moba_k_norm_w, moba_out_norm_w, gmlp_v_norm_w, gmlp_w_s, gmlp_b_s, gmlp_out_norm_w, rwkv_mu, rwkv_w0, rwkv_w_up, rwkv_a0, rwkv_a_up, rwkv_g_up, rwkv_k_k, rwkv_k_a, rwkv_r_k, rwkv_ln_w, rwkv_ln_b):
    W = dict(ssd_conv_w=ssd_conv_w, ssd_conv_b=ssd_conv_b, ssd_dt_bias=ssd_dt_bias, ssd_a_log=ssd_a_log,
             ssd_d=ssd_d, ssd_norm_w=ssd_norm_w, moba_q_norm_w=moba_q_norm_w, moba_k_norm_w=moba_k_norm_w,
             moba_out_norm_w=moba_out_norm_w, gmlp_v_norm_w=gmlp_v_norm_w, gmlp_w_s=gmlp_w_s, gmlp_b_s=gmlp_b_s,
             gmlp_out_norm_w=gmlp_out_norm_w, rwkv_mu=rwkv_mu, rwkv_w0=rwkv_w0, rwkv_w_up=rwkv_w_up,
             rwkv_a0=rwkv_a0, rwkv_a_up=rwkv_a_up, rwkv_g_up=rwkv_g_up, rwkv_k_k=rwkv_k_k, rwkv_k_a=rwkv_k_a,
             rwkv_r_k=rwkv_r_k, rwkv_ln_w=rwkv_ln_w, rwkv_ln_b=rwkv_ln_b)
    Bp, Tq, D = x_prompt.shape
    Bs, Ts, _ = x_sample.shape
    n_pages = page_table.shape[1]
    assert n_pages * PAGE_SIZE == PAST_LEN and PAST_LEN % MOBA_BLOCK == 0
    assert PAST_LEN // MOBA_BLOCK >= MOBA_TOPK and Ts <= MOBA_BLOCK
    assert n_pages % (KMEAN_BLOCKS * PAGES_PER_BLOCK) == 0 and RWKV_CHUNK == RWKV_HEAD
    Rp, Rs = Bp * Tq, Bs * Ts

    assert w_in.shape == (DEPTH, D, IN_COLS)
    w_in_b = _win_prep(w_in)
    w_out_b = w_out.astype(BF16)
    nw4 = norm_w.reshape(DEPTH, 3, 1, D)
    b_ada3 = b_ada.reshape(DEPTH, 1, N_MOD * D)

    n_c = Bp + Bs
    c_all = jnp.pad(jnp.concatenate([c_prompt, c_sample], 0), ((0, -n_c % 8), (0, 0)))

    pos_p = jnp.arange(Tq, dtype=jnp.int32)
    pos_s = PAST_LEN + jnp.arange(Ts, dtype=jnp.int32)
    cos_p, sin_p = (jnp.tile(t, (Bp, 1)) for t in _rope_tables(pos_p))
    cos_s, sin_s = (jnp.tile(t, (Bs, 1)) for t in _rope_tables(pos_s))

    pt_flat = page_table.reshape(-1)

    zeros = lambda *s: jnp.zeros(s, F32)
    xp = x_prompt.reshape(Rp, D)
    xs = x_sample.reshape(Rs, D)
    TM = 512
    outs_p, outs_s = [], []
    for i in range(DEPTH):
        lw = _layer_weights(W, i)
        mod = _ada(c_all, w_ada, b_ada3, i).reshape(-1, N_MOD, D)
        mod_p = _Mod(mod[:Bp].reshape(Bp, N_MOD, 1, D), False, Tq, TM)
        mod_s = _Mod(jnp.repeat(mod[Bp:n_c], Ts, axis=0).transpose(1, 0, 2), True, Ts, Rs)

        xs, *ffn_a = _ffn(xs, mod_s, 0, nw4, ffn_w1, ffn_w3, ffn_w2, i, 0, Rs, 512)
        cols = _inproj(xs, mod_s, nw4, w_in_b, i, Rs, 1536)
        q_rot, k_rot, k4, v4 = _moba_prep(cols, cos_s, sin_s, lw['qw'], lw['kw'], Rs)
        y_ssd, y_gm, y_rw, v_gm, st = _mixers(cols, Bs, Ts, SAMPLE_PAD, lw, state_ssd_conv[i], state_ssd[i],
                                              state_rwkv[i], state_rwkv_shift[i], Ts)
        kmean = _kmean_pages(cache_k, pt_flat, i, Bs, n_pages)
        idx = _select(q_rot, kmean, Bs, Ts)
        y_att = _sample_attn(q_rot, k_rot, cols, lw['ow'], cache_k, cache_v, pt_flat, idx.reshape(-1), i, Bs, Ts,
                             n_pages).reshape(Rs, GROUP_WIDTH)
        xs = _outproj(xs, mod_s, (y_ssd, y_att, y_gm, y_rw), w_out_b, i, Rs)
        xs, *ffn_b = _ffn(xs, mod_s, 6, nw4, ffn_w1, ffn_w3, ffn_w2, i, 1, Rs, 512)
        shp = (Bs, Ts, MOBA_HEADS, MOBA_HEAD_DIM)
        outs_s.append((k4.reshape(shp), v4.reshape(shp)) + st + (v_gm,))

        xp = _ffn(xp, mod_p, 0, nw4, *ffn_a, i, 0, TM, 512)
        cols = _inproj(xp, mod_p, nw4, w_in_b, i, TM, 1536)
        q_rot, k_rot, k4, v4 = _moba_prep(cols, cos_p, sin_p, lw['qw'], lw['kw'], TM)
        y_ssd, y_gm, y_rw, _, st = _mixers(cols, Bp, Tq, Tq, lw, zeros(Bp, SSD_CONV - 1, SSD_CONV_DIM),
                                           zeros(Bp, SSD_HEADS, SSD_HEADDIM, SSD_STATE),
                                           zeros(Bp, RWKV_HEADS, RWKV_HEAD, RWKV_HEAD), zeros(Bp, RWKV_COLS), None)
        y_att = _moba_attn(q_rot.reshape(Bp, Tq, -1), k_rot.reshape(Bp, Tq, -1), cols.reshape(Bp, Tq, COLS),
                           lw['ow']).reshape(Rp, -1)
        xp = _outproj(xp, mod_p, (y_ssd, y_att, y_gm, y_rw), w_out_b, i, TM)
        xp = _ffn(xp, mod_p, 6, nw4, *ffn_b, i, 1, TM, 512)
        shp = (Bp, Tq, MOBA_HEADS, MOBA_HEAD_DIM)
        outs_p.append((k4.reshape(shp), v4.reshape(shp)) + st)

    k_p, v_p, ssd_p, conv_p, rwkv_p, shift_p = (jnp.stack(s) for s in zip(*outs_p))
    k_s, v_s, ssd_s, conv_s, rwkv_s, shift_s, gmlp_v_s = (jnp.stack(s) for s in zip(*outs_s))
    return (xp.reshape(Bp, Tq, D), xs.reshape(Bs, Ts, D), k_p, v_p, k_s, v_s, ssd_p, ssd_s, conv_p, conv_s,
            rwkv_p, rwkv_s, shift_p, shift_s, gmlp_v_s)
```

```python
import functools
import math

import jax
import jax.numpy as jnp
from jax import lax
from jax.experimental import pallas as pl
from jax.experimental.pallas import tpu as pltpu

F32 = jnp.float32
BF16 = jnp.bfloat16
HI = lax.Precision.HIGHEST

D_MODEL = 2048
DEPTH = 2
PAST_LEN = 16384
PAGE_SIZE = 128
GROUP_WIDTH = 512
SSD_HEADS = 8
SSD_HEADDIM = 64
SSD_STATE = 128
SSD_CONV = 4
SSD_CHUNK = 128
SSD_CONV_DIM = 1024
MOBA_HEADS = 4
MOBA_HEAD_DIM = 128
MOBA_BLOCK = 256
MOBA_TOPK = 3
ROPE_THETA = 10000.0
GMLP_CHUNK = 128
GMLP_HEADS = 4
RWKV_HEADS = 8
RWKV_HEAD = 64
RWKV_COLS = 1792
RWKV_CHUNK = 64
RWKV_SUB = 16
RWKV_SEQS_PER_STEP = 4
RWKV_DECAY_SCALE = 0.606531
RWKV_LN_EPS = 64e-5
D_FF = 5632
N_MOD = 9
FFN_RES = 0.5
EPS = 1e-6
NEG = -1e30

COLS = 6144
IN_COLS = 5896
C_Z, C_X, C_BC = 0, 512, 1024
C_MQ, C_MK, C_MV = 1536, 2048, 2560
C_GU, C_GV = 3072, 3584
C_RR, C_RK, C_RV, C_RL = 4096, 4608, 5120, 5632
C_DT = 5888

LANE = 128
SAMPLE_PAD = 128
VMEM_LIMIT = 56 * 1024 * 1024


def _params(n_axes, vmem=VMEM_LIMIT):
    return pltpu.CompilerParams(dimension_semantics=("arbitrary",) * n_axes, vmem_limit_bytes=vmem)


def _dot(a, b, precision=None):
    return jnp.dot(a, b, preferred_element_type=F32, precision=precision)


def _dot_t(a, b, precision=None):
    return lax.dot_general(a, b, (((1,), (1,)), ((), ())), preferred_element_type=F32, precision=precision)


def _dot_0(a, b, precision=None):
    return lax.dot_general(a, b, (((0,), (0,)), ((), ())), preferred_element_type=F32, precision=precision)


def _iota(shape, dim):
    return lax.broadcasted_iota(jnp.int32, shape, dim)


def _rms(x, w):
    return x * lax.rsqrt(jnp.mean(x * x, -1, keepdims=True) + EPS) * w


def _silu(x):
    return x * jax.nn.sigmoid(x)


def _ada_kernel(c_ref, w_ref, b_ref, o_ref):
    s = _silu(c_ref[...]).astype(BF16)
    o_ref[...] = _dot(s, w_ref[...].astype(BF16)) + b_ref[...]


def _ada(c_all, w_ada, b_ada3, layer):
    rows = c_all.shape[0]
    n_out = w_ada.shape[-1]
    tn = 1024
    return pl.pallas_call(
        _ada_kernel,
        grid=(n_out // tn,),
        in_specs=[
            pl.BlockSpec((rows, D_MODEL), lambda n: (0, 0)),
            pl.BlockSpec((None, D_MODEL, tn), lambda n: (layer, 0, n)),
            pl.BlockSpec((None, 1, tn), lambda n: (layer, 0, n)),
        ],
        out_specs=pl.BlockSpec((rows, tn), lambda n: (0, n)),
        out_shape=jax.ShapeDtypeStruct((rows, n_out), F32),
        compiler_params=_params(1),
        name="ada",
    )(c_all, w_ada, b_ada3)


class _Mod:
    def __init__(self, arr, per_row, rows_per_batch, tm):
        self.arr = arr
        self.per_row = per_row
        self.tiles_per_batch = None if per_row else rows_per_batch // tm
        self.tm = tm

    def spec(self, j):
        if self.per_row:
            return pl.BlockSpec((None, self.tm, D_MODEL), lambda r, *_: (j, r, 0))
        tpb = self.tiles_per_batch
        return pl.BlockSpec((None, None, 1, D_MODEL), lambda r, *_: (r // tpb, j, 0, 0))


def _ffn_kernel(x_ref, nw_ref, sh_ref, sc_ref, g_ref, w1_ref, w3_ref, w2_ref, o_ref, *rest, nf, emit_bf16):
    f = pl.program_id(1)
    h_ref, acc_ref = rest[-2:]

    @pl.when(f == 0)
    def _():
        xn = _rms(x_ref[...], nw_ref[...])
        h_ref[...] = (xn * (1 + sc_ref[...]) + sh_ref[...]).astype(BF16)
        acc_ref[...] = jnp.zeros_like(acc_ref)

    w1, w3, w2 = (w[...].astype(BF16) for w in (w1_ref, w3_ref, w2_ref))
    if emit_bf16:
        for dst, w in zip(rest[:3], (w1, w3, w2)):
            dst[...] = w
    h = h_ref[...]
    a = _dot(h, w1)
    b = _dot(h, w3)
    acc_ref[...] += _dot((_silu(a) * b).astype(BF16), w2)

    @pl.when(f == nf - 1)
    def _():
        o_ref[...] = x_ref[...] + FFN_RES * g_ref[...] * acc_ref[...]


def _ffn(x, mod, j0, nw4, w1, w3, w2, layer, slot, tm, tf):
    rows = x.shape[0]
    nf = D_FF // tf
    stacked = w1.ndim == 4
    if stacked:
        up = pl.BlockSpec((None, None, D_MODEL, tf), lambda r, f: (layer, slot, 0, f))
        down = pl.BlockSpec((None, None, tf, D_MODEL), lambda r, f: (layer, slot, f, 0))
    else:
        up = pl.BlockSpec((D_MODEL, tf), lambda r, f: (0, f))
        down = pl.BlockSpec((tf, D_MODEL), lambda r, f: (f, 0))
    out_specs = [pl.BlockSpec((tm, D_MODEL), lambda r, f: (r, 0))]
    out_shape = [jax.ShapeDtypeStruct((rows, D_MODEL), F32)]
    if stacked:
        assert rows == tm
        out_specs += [pl.BlockSpec((D_MODEL, tf), lambda r, f: (0, f))] * 2 + [pl.BlockSpec((tf, D_MODEL), lambda r, f: (f, 0))]
        out_shape += [jax.ShapeDtypeStruct((D_MODEL, D_FF), BF16)] * 2 + [jax.ShapeDtypeStruct((D_FF, D_MODEL), BF16)]
    outs = pl.pallas_call(
        functools.partial(_ffn_kernel, nf=nf, emit_bf16=stacked),
        grid=(rows // tm, nf),
        in_specs=[
            pl.BlockSpec((tm, D_MODEL), lambda r, f: (r, 0)),
            pl.BlockSpec((None, None, 1, D_MODEL), lambda r, f: (layer, 2 * slot, 0, 0)),
            mod.spec(j0), mod.spec(j0 + 1), mod.spec(j0 + 2),
            up, up, down,
        ],
        out_specs=out_specs,
        out_shape=out_shape,
        scratch_shapes=[pltpu.VMEM((tm, D_MODEL), BF16), pltpu.VMEM((tm, D_MODEL), F32)],
        compiler_params=_params(2),
        name="ffn",
    )(x, nw4, mod.arr, mod.arr, mod.arr, w1, w3, w2)
    return outs if stacked else outs[0]


def _inproj_kernel(x_ref, nw_ref, sh_ref, sc_ref, w_ref, o_ref, h_ref):
    @pl.when(pl.program_id(1) == 0)
    def _():
        xn = _rms(x_ref[...], nw_ref[...])
        h_ref[...] = (xn * (1 + sc_ref[...]) + sh_ref[...]).astype(BF16)

    o_ref[...] = _dot_t(h_ref[...], w_ref[...])


def _inproj(x, mod, nw4, w_in_t, layer, tm, tn):
    rows = x.shape[0]
    return pl.pallas_call(
        _inproj_kernel,
        grid=(rows // tm, COLS // tn),
        in_specs=[
            pl.BlockSpec((tm, D_MODEL), lambda r, n: (r, 0)),
            pl.BlockSpec((None, None, 1, D_MODEL), lambda r, n: (layer, 1, 0, 0)),
            mod.spec(3), mod.spec(4),
            pl.BlockSpec((None, tn, D_MODEL), lambda r, n: (layer, n, 0)),
        ],
        out_specs=pl.BlockSpec((tm, tn), lambda r, n: (r, n)),
        out_shape=jax.ShapeDtypeStruct((rows, COLS), F32),
        scratch_shapes=[pltpu.VMEM((tm, D_MODEL), BF16)],
        compiler_params=_params(2),
        name="inproj",
    )(x, nw4, mod.arr, mod.arr, w_in_t)


WIN_TILE = 512


def _win_prep_kernel(w_ref, dt_ref, o_ref, *, n_tiles, tail):
    j = pl.program_id(1)

    @pl.when(j < n_tiles - 1)
    def _():
        o_ref[...] = w_ref[0].astype(BF16)

    @pl.when(j == n_tiles - 1)
    def _():
        o_ref[0:tail, :] = w_ref[0, WIN_TILE - tail:WIN_TILE, :].astype(BF16)
        rest = jnp.concatenate([dt_ref[0], jnp.zeros((WIN_TILE - tail - SSD_HEADS, D_MODEL), F32)], axis=0)
        o_ref[tail:WIN_TILE, :] = rest.astype(BF16)


def _win_prep(w_in_t):
    head = C_BC + GROUP_WIDTH
    n_tiles = COLS // WIN_TILE
    tail = C_DT % WIN_TILE
    assert head % WIN_TILE == 0 and SSD_HEADS == 8 and 0 < tail < WIN_TILE

    def src_row(j):
        shifted = jnp.minimum(j * WIN_TILE + SSD_HEADS, IN_COLS - WIN_TILE)
        return pl.multiple_of(jnp.where(j * WIN_TILE < head, j * WIN_TILE, shifted), SSD_HEADS)

    return pl.pallas_call(
        functools.partial(_win_prep_kernel, n_tiles=n_tiles, tail=tail),
        grid=(DEPTH, n_tiles),
        in_specs=[
            pl.BlockSpec((pl.Element(1), pl.Element(WIN_TILE), pl.Element(D_MODEL)), lambda i, j: (i, src_row(j), 0)),
            pl.BlockSpec((pl.Element(1), pl.Element(SSD_HEADS), pl.Element(D_MODEL)), lambda i, j: (i, head, 0)),
        ],
        out_specs=pl.BlockSpec((None, WIN_TILE, D_MODEL), lambda i, j: (i, j, 0)),
        out_shape=jax.ShapeDtypeStruct((DEPTH, COLS, D_MODEL), BF16),
        compiler_params=_params(2),
        name="w_in_prep",
    )(w_in_t, w_in_t)


def _outproj_kernel(x_ref, g_ref, y0_ref, y1_ref, y2_ref, y3_ref, w_ref, o_ref):
    G = GROUP_WIDTH
    acc = _dot(y0_ref[...], w_ref[0:G, :])
    acc += _dot(y1_ref[...], w_ref[G:2 * G, :])
    acc += _dot(y2_ref[...], w_ref[2 * G:3 * G, :])
    acc += _dot(y3_ref[...], w_ref[3 * G:4 * G, :])
    o_ref[...] = x_ref[...] + g_ref[...] * acc


def _outproj(x, mod, ys, w_out, layer, tm):
    rows = x.shape[0]
    yspec = pl.BlockSpec((tm, GROUP_WIDTH), lambda r: (r, 0))
    return pl.pallas_call(
        _outproj_kernel,
        grid=(rows // tm,),
        in_specs=[
            pl.BlockSpec((tm, D_MODEL), lambda r: (r, 0)),
            mod.spec(5), yspec, yspec, yspec, yspec,
            pl.BlockSpec((None, D_MODEL, D_MODEL), lambda r: (layer, 0, 0)),
        ],
        out_specs=pl.BlockSpec((tm, D_MODEL), lambda r: (r, 0)),
        out_shape=jax.ShapeDtypeStruct((rows, D_MODEL), F32),
        compiler_params=_params(1),
        name="outproj",
    )(x, mod.arr, *ys, w_out)


def _softplus(x):
    return jnp.maximum(x, 0.0) + jnp.log1p(jnp.exp(-jnp.abs(x)))


def _ssd_kernel(z_ref, x_ref, bc_ref, dt_ref, prev_ref, cw_ref, cb_ref, dtb_ref, alog_ref, dskip_ref, nw_ref,
                h0_ref, y_ref, hout_ref, ext_ref, st_ref, *, C, nc, t_valid):
    c = pl.program_id(1)
    G = GROUP_WIDTH

    @pl.when(c == 0)
    def _():
        ext_ref[0:8, :] = prev_ref[...]
        st_ref[...] = h0_ref[...]

    ext_ref[8:8 + C, 0:G] = x_ref[...]
    ext_ref[8:8 + C, G:2 * G] = bc_ref[...]
    conv = cb_ref[...] + ext_ref[5:5 + C, :] * cw_ref[0:1, :]
    for i in range(1, SSD_CONV):
        conv = conv + ext_ref[5 + i:5 + i + C, :] * cw_ref[i:i + 1, :]
    ext_ref[0:8, :] = ext_ref[C:C + 8, :]
    xbc = _silu(conv)
    xs = xbc[:, 0:G]

    dt = _softplus(dt_ref[...] + dtb_ref[...])
    if t_valid is not None:
        dt = jnp.where(c * C + _iota((C, LANE), 0) < t_valid, dt, 0.0)
    a = dt * (-jnp.exp(alog_ref[...]))
    tri = _iota((C, C), 0) >= _iota((C, C), 1)
    tri_b = tri.astype(BF16)
    a_hi, a_mid, a_lo = _split3(a)
    a_cs = _dg(tri_b, a_hi, _NN) + (_dg(tri_b, a_mid, _NN) + _dg(tri_b, a_lo, _NN))
    a_cs_t = a_cs.T
    lane_lo = _iota((C, LANE), 1) < SSD_HEADDIM
    row_lo = _iota((LANE, 1), 0) < SSD_HEADDIM
    col = lambda h: a_cs[:, h:h + 1]
    row = lambda h: a_cs_t[h:h + 1, :]
    last = lambda h: a_cs[C - 1:C, h:h + 1]
    by_head = lambda p, f: jnp.where(lane_lo, f(2 * p), f(2 * p + 1))
    decay_mat = lambda h: jnp.exp(jnp.where(tri, col(h) - row(h), -jnp.inf))

    n_groups = 2
    heads_per_group = SSD_HEADS // n_groups
    bm_s = [_split2(xbc[:, G + g * SSD_STATE:G + (g + 1) * SSD_STATE]) for g in range(n_groups)]
    cm_s = [_split2(xbc[:, G + (n_groups + g) * SSD_STATE:G + (n_groups + g + 1) * SSD_STATE]) for g in range(n_groups)]
    scores = [_dot3(cm_s[g], bm_s[g], _NT) for g in range(n_groups)]
    group = lambda p: 2 * p // heads_per_group
    each = lambda f: [f(p) for p in range(SSD_HEADS // 2)]
    m0_s = each(lambda p: _split2(scores[group(p)] * decay_mat(2 * p)))
    m1_s = each(lambda p: _split2(scores[group(p)] * decay_mat(2 * p + 1)))
    xs_p = each(lambda p: xs[:, p * LANE:(p + 1) * LANE])
    xdt = each(lambda p: xs_p[p] * by_head(p, lambda h: dt[:, h:h + 1]))
    xdt_s = each(lambda p: _split2(xdt[p]))
    y_diag = each(lambda p: jnp.where(lane_lo, _dot3(m0_s[p], xdt_s[p]), _dot3(m1_s[p], xdt_s[p])))
    st = each(lambda p: st_ref[p * LANE:(p + 1) * LANE, :])
    y_off = each(lambda p: _dot3(cm_s[group(p)], _split2(st[p]), _NT) * by_head(p, lambda h: jnp.exp(col(h))))
    decay = each(lambda p: by_head(p, lambda h: jnp.exp(last(h) - col(h))))
    new = each(lambda p: _dot3(_split2(xdt[p] * decay[p]), bm_s[group(p)], _TN))
    for p in range(SSD_HEADS // 2):
        keep = jnp.where(row_lo, jnp.exp(last(2 * p)), jnp.exp(last(2 * p + 1)))
        st_ref[p * LANE:(p + 1) * LANE, :] = st[p] * keep + new[p]
    ys = each(lambda p: y_diag[p] + y_off[p] + xs_p[p] * dskip_ref[:, p * LANE:(p + 1) * LANE])

    y = jnp.concatenate(ys, axis=1) * _silu(z_ref[...])
    y_ref[...] = _rms(y, nw_ref[...]).astype(y_ref.dtype)

    @pl.when(c == nc - 1)
    def _():
        hout_ref[...] = st_ref[...]


def _ssd(cols3, prev8, h0, cw8, cb, dtb, alog, dskip, nw, t_valid):
    B, T, _ = cols3.shape
    C = SSD_CHUNK
    nc = T // C
    G = GROUP_WIDTH
    vec = lambda n: pl.BlockSpec((1, n), lambda b, c: (0, 0))
    return pl.pallas_call(
        functools.partial(_ssd_kernel, C=C, nc=nc, t_valid=t_valid),
        grid=(B, nc),
        in_specs=[
            pl.BlockSpec((None, C, G), lambda b, c: (b, c, C_Z // G)),
            pl.BlockSpec((None, C, G), lambda b, c: (b, c, C_X // G)),
            pl.BlockSpec((None, C, G), lambda b, c: (b, c, C_BC // G)),
            pl.BlockSpec((None, C, LANE), lambda b, c: (b, c, C_DT // LANE)),
            pl.BlockSpec((None, 8, SSD_CONV_DIM), lambda b, c: (b, 0, 0)),
            pl.BlockSpec((8, SSD_CONV_DIM), lambda b, c: (0, 0)),
            vec(SSD_CONV_DIM), vec(LANE), vec(LANE), vec(G), vec(G),
            pl.BlockSpec((None, SSD_HEADS * SSD_HEADDIM, SSD_STATE), lambda b, c: (b, 0, 0)),
        ],
        out_specs=[
            pl.BlockSpec((None, C, G), lambda b, c: (b, c, 0)),
            pl.BlockSpec((None, SSD_HEADS * SSD_HEADDIM, SSD_STATE), lambda b, c: (b, 0, 0)),
        ],
        out_shape=[
            jax.ShapeDtypeStruct((B, T, G), BF16),
            jax.ShapeDtypeStruct((B, SSD_HEADS * SSD_HEADDIM, SSD_STATE), F32),
        ],
        scratch_shapes=[pltpu.VMEM((C + 8, SSD_CONV_DIM), F32), pltpu.VMEM((SSD_HEADS * SSD_HEADDIM, SSD_STATE), F32)],
        compiler_params=_params(2),
        name="ssd",
    )(cols3, cols3, cols3, cols3, prev8, cw8, cb, dtb, alog, dskip, nw, h0)


def _gmlp_kernel(u_ref, v_ref, vw_ref, ws_ref, bs_ref, ow_ref, y_ref, vout_ref, *, C):
    u = jax.nn.gelu(u_ref[...])
    v = _rms(jax.nn.gelu(v_ref[...]), vw_ref[...])
    vout_ref[...] = v
    tri = _iota((C, C), 0) >= _iota((C, C), 1)
    mixed = []
    for h in range(GMLP_HEADS):
        ws = jnp.where(tri, ws_ref[h], 0.0)
        mixed.append(_dot3(_split2(ws), _split2(v[:, h * LANE:(h + 1) * LANE])) + bs_ref[h])
    y = u * jnp.concatenate(mixed, axis=1)
    y_ref[...] = _rms(y, ow_ref[...]).astype(y_ref.dtype)


def _gmlp(cols3, vw, ws, bs, ow):
    B, T, _ = cols3.shape
    C = GMLP_CHUNK
    G = GROUP_WIDTH
    vec = pl.BlockSpec((1, G), lambda b, c: (0, 0))
    return pl.pallas_call(
        functools.partial(_gmlp_kernel, C=C),
        grid=(B, T // C),
        in_specs=[
            pl.BlockSpec((None, C, G), lambda b, c: (b, c, C_GU // G)),
            pl.BlockSpec((None, C, G), lambda b, c: (b, c, C_GV // G)),
            vec,
            pl.BlockSpec((GMLP_HEADS, C, C), lambda b, c: (0, 0, 0)),
            pl.BlockSpec((GMLP_HEADS, C, 1), lambda b, c: (0, 0, 0)),
            vec,
        ],
        out_specs=[pl.BlockSpec((None, C, G), lambda b, c: (b, c, 0))] * 2,
        out_shape=[jax.ShapeDtypeStruct((B, T, G), BF16), jax.ShapeDtypeStruct((B, T, G), F32)],
        compiler_params=_params(2),
        name="gmlp",
    )(cols3, cols3, vw, ws, bs, ow)


_NN = ((1,), (0,))
_NT = ((1,), (1,))
_TN = ((0,), (0,))


def _split2(x):
    hi = x.astype(BF16)
    return hi, (x - hi.astype(F32)).astype(BF16)


def _split3(x):
    hi = x.astype(BF16)
    rest = x - hi.astype(F32)
    mid = rest.astype(BF16)
    return hi, mid, (rest - mid.astype(F32)).astype(BF16)


def _dg(a, b, dims):
    return lax.dot_general(a, b, (dims, ((), ())), preferred_element_type=F32)


def _dot3(a, b, dims=_NN):
    (ah, al), (bh, bl) = a, b
    return _dg(ah, bh, dims) + (_dg(ah, bl, dims) + _dg(al, bh, dims))


def _block_diag(pieces, same_head):
    return tuple(jnp.where(same_head, jnp.concatenate([x, x], axis=0), jnp.zeros((), x.dtype)) for x in pieces)


def _rwkv_kernel(r_ref, k_ref, v_ref, lo_ref, prev_ref, mu_ref, w0_ref, wup_ref, a0_ref, aup_ref, gup_ref,
                 kk_ref, ka_ref, rk_ref, lnw_ref, lnb_ref, s0_ref, y_ref, sout_ref, ext_ref, st_ref,
                 *, C, nc, t_valid):
    c = pl.program_id(1)
    G = GROUP_WIDTH
    N = RWKV_HEAD
    P = 2 * N

    n_batch = r_ref.shape[0]
    n_pairs = G // P
    sls = [slice(p * P, (p + 1) * P) for p in range(n_pairs)]
    same_head_n = (_iota((P, P), 0) < N) == (_iota((P, P), 1) < N)
    ones_bd = same_head_n.astype(BF16)
    tri_cc = (_iota((C, C), 0) >= _iota((C, C), 1)).astype(BF16)

    @pl.when(c == 0)
    def _():
        ext_ref[:, 0:8, :] = prev_ref[...]
        st_ref[...] = s0_ref[...]

    def head_sum(x):
        def pair_sum(p):
            hi, lo = _split2(x[:, sls[p]])
            return _dg(hi, ones_bd, _NN) + _dg(lo, ones_bd, _NN)
        return jnp.concatenate([pair_sum(p) for p in range(n_pairs)], axis=1)

    def token_shift(n):
        ext_ref[n, 8:8 + C, 0:G] = r_ref[n]
        ext_ref[n, 8:8 + C, G:2 * G] = k_ref[n]
        ext_ref[n, 8:8 + C, 2 * G:3 * G] = v_ref[n]
        ext_ref[n, 8:8 + C, 3 * G:RWKV_COLS] = lo_ref[n]
        cur = ext_ref[n, 8:8 + C, :]
        prev = ext_ref[n, 7:7 + C, :]
        xs = cur + (prev - cur) * mu_ref[...]
        ext_ref[n, 0:8, :] = ext_ref[n, C:C + 8, :]
        return xs

    shifted = [token_shift(n) for n in range(n_batch)]
    la_all = jnp.concatenate([xs[:, 3 * G:3 * G + P] for xs in shifted], axis=0)
    gl_all = jnp.concatenate([xs[:, 3 * G + P:RWKV_COLS] for xs in shifted], axis=0)
    w_log_all = -RWKV_DECAY_SCALE * jax.nn.sigmoid(
        w0_ref[...] + _dot3(_split2(jnp.tanh(la_all)), _split2(wup_ref[...])))
    a_all = jax.nn.sigmoid(a0_ref[...] + _dot3(_split2(la_all), _split2(aup_ref[...])))
    g_all = _dot3(_split2(jax.nn.sigmoid(gl_all)), _split2(gup_ref[...]))

    def chunk_inputs(n):
        xs = shifted[n]
        r, k, v = xs[:, 0:G], xs[:, G:2 * G], xs[:, 2 * G:3 * G]
        w_log, a, g = (t[n * C:(n + 1) * C] for t in (w_log_all, a_all, g_all))

        kk = k * kk_ref[...]
        kk = kk * lax.rsqrt(jnp.maximum(head_sum(kk * kk), 1e-12))
        k2 = k * (1 + (a - 1) * ka_ref[...])
        if t_valid is not None:
            ok = c * C + _iota((C, G), 0) < t_valid
            w_log = jnp.where(ok, w_log, 0.0)
            kk = jnp.where(ok, kk, 0.0)
            k2 = jnp.where(ok, k2, 0.0)
        b = kk * a

        w_hi, w_mid, w_lo = _split3(w_log)
        cl = _dg(tri_cc, w_hi, _NN) + (_dg(tri_cc, w_mid, _NN) + _dg(tri_cc, w_lo, _NN))
        cl_last = cl[C - 1:C, :]
        einv = jnp.exp(-cl)
        e_c = jnp.exp(cl_last - cl)
        return dict(r=r, v=v, k2=k2, g=g, kkp=kk * jnp.exp(cl - w_log), rp=r * jnp.exp(cl), bi=b * einv,
                    ki=k2 * einv, bt=b * e_c, kt=k2 * e_c, p_c=jnp.exp(cl_last))

    seqs = [chunk_inputs(n) for n in range(n_batch)]
    units = [(n, p) for n in range(n_batch) for p in range(n_pairs)]
    each = lambda f: [f(u) for u in range(len(units))]
    tile = lambda name: each(lambda u: seqs[units[u][0]][name][:, sls[units[u][1]]])
    kkp, rp, bi, ki, bt, kt, v_p = (tile(name) for name in ("kkp", "rp", "bi", "ki", "bt", "kt", "v"))

    t_i = _iota((C, P), 0)
    s_i = _iota((C, P), 1) % C
    strict = s_i < t_i
    incl = s_i <= t_i
    diag_blk = (s_i // RWKV_SUB) == (t_i // RWKV_SUB)
    eye = (s_i == t_i).astype(F32)
    same_head = (_iota((P, P), 0) < C) == (_iota((P, P), 1) < C)

    memo = {}

    def sp(x):
        if id(x) not in memo:
            memo[id(x)] = (x, _split2(x))
        return memo[id(x)][1]

    def bd(x):
        if ("bd", id(x)) not in memo:
            memo["bd", id(x)] = (x, _block_diag(sp(x), same_head))
        return memo["bd", id(x)][1]

    def mm(xs, ys):
        return each(lambda p: _dot3(sp(xs[p]), bd(ys[p])))

    kkp_s = each(lambda p: _split2(kkp[p]))
    rp_s = each(lambda p: _split2(rp[p]))
    lhs_s = each(lambda p: tuple(jnp.concatenate([x, y], axis=0) for x, y in zip(kkp_s[p], rp_s[p])))
    ab = each(lambda p: _dot3(lhs_s[p], _block_diag(_split2(bi[p]), same_head_n), _NT))
    ak = each(lambda p: _dot3(lhs_s[p], _block_diag(_split2(ki[p]), same_head_n), _NT))
    a_m = each(lambda p: jnp.where(strict, ab[p][0:C], 0.0))
    b_k = each(lambda p: jnp.where(strict, ak[p][0:C], 0.0))
    r_b = each(lambda p: jnp.where(incl, ab[p][C:2 * C], 0.0))
    r_k = each(lambda p: jnp.where(incl, ak[p][C:2 * C], 0.0))

    n_pow = each(lambda p: jnp.where(diag_blk, -a_m[p], 0.0))
    a_o = each(lambda p: jnp.where(diag_blk, 0.0, a_m[p]))
    t_d = each(lambda p: eye + n_pow[p])
    for _ in range(int(math.log2(RWKV_SUB)) - 1):
        n_pow = mm(n_pow, n_pow)
        step = mm(t_d, n_pow)
        t_d = each(lambda p: t_d[p] + step[p])
    m1 = mm(t_d, a_o)
    m2 = mm(m1, m1)
    im = each(lambda p: eye - m1[p])
    im_m2 = mm(im, m2)
    t_full = mm(each(lambda p: im[p] + im_m2[p]), t_d)

    st = each(lambda u: st_ref[units[u]])
    st_s = each(lambda p: _split2(st[p]))
    bkv = mm(b_k, v_p)
    rhs = each(lambda p: _dot3(kkp_s[p], st_s[p], _NT) + bkv[p])
    u = mm(t_full, rhs)
    rkv = mm(r_k, v_p)
    rbu = mm(r_b, u)
    ys = each(lambda p: _dot3(rp_s[p], st_s[p], _NT) + rkv[p] - rbu[p])
    upd = each(lambda p: _dot3(_split2(jnp.concatenate([v_p[p], -u[p]], axis=0)),
                               _split2(jnp.concatenate([kt[p], bt[p]], axis=0)), _TN))
    for u_i, (n, p) in enumerate(units):
        st_ref[n, p] = st[u_i] * seqs[n]["p_c"][:, sls[p]] + jnp.where(same_head_n, upd[u_i], 0.0)

    for n, seq in enumerate(seqs):
        y = jnp.concatenate(ys[n * n_pairs:(n + 1) * n_pairs], axis=1)
        mean = head_sum(y) * (1.0 / N)
        d = y - mean
        var = head_sum(d * d) * (1.0 / N)
        yn = d * lax.rsqrt(var + RWKV_LN_EPS) * lnw_ref[...] + lnb_ref[...]
        bonus = head_sum(seq["r"] * seq["k2"] * rk_ref[...]) * seq["v"]
        y_ref[n] = ((yn + bonus) * seq["g"]).astype(y_ref.dtype)

    @pl.when(c == nc - 1)
    def _():
        sout_ref[...] = st_ref[...]


def _rwkv(cols3, prev8, s0_bd, mu, w0, wup, a0, aup, gup, kk, ka, rk, lnw, lnb, t_valid):
    B, T, _ = cols3.shape
    C = RWKV_CHUNK
    nc = T // C
    G = GROUP_WIDTH
    P = 2 * RWKV_HEAD
    n_pairs = RWKV_HEADS // 2
    S = RWKV_SEQS_PER_STEP
    vec = lambda n: pl.BlockSpec((1, n), lambda b, c: (0, 0))
    mat = lambda m, n: pl.BlockSpec((m, n), lambda b, c: (0, 0))
    return pl.pallas_call(
        functools.partial(_rwkv_kernel, C=C, nc=nc, t_valid=t_valid),
        grid=(B // S, nc),
        in_specs=[
            pl.BlockSpec((S, C, G), lambda b, c: (b, c, C_RR // G)),
            pl.BlockSpec((S, C, G), lambda b, c: (b, c, C_RK // G)),
            pl.BlockSpec((S, C, G), lambda b, c: (b, c, C_RV // G)),
            pl.BlockSpec((S, C, 2 * P), lambda b, c: (b, c, C_RL // (2 * P))),
            pl.BlockSpec((S, 8, RWKV_COLS), lambda b, c: (b, 0, 0)),
            vec(RWKV_COLS), vec(G), mat(P, G), vec(G), mat(P, G), mat(P, G),
            vec(G), vec(G), vec(G), vec(G), vec(G),
            pl.BlockSpec((S, n_pairs, P, P), lambda b, c: (b, 0, 0, 0)),
        ],
        out_specs=[
            pl.BlockSpec((S, C, G), lambda b, c: (b, c, 0)),
            pl.BlockSpec((S, n_pairs, P, P), lambda b, c: (b, 0, 0, 0)),
        ],
        out_shape=[
            jax.ShapeDtypeStruct((B, T, G), BF16),
            jax.ShapeDtypeStruct((B, n_pairs, P, P), F32),
        ],
        scratch_shapes=[pltpu.VMEM((S, C + 8, RWKV_COLS), F32), pltpu.VMEM((S, n_pairs, P, P), F32)],
        compiler_params=_params(2),
        name="rwkv",
    )(cols3, cols3, cols3, cols3, prev8, mu, w0, wup, a0, aup, gup, kk, ka, rk, lnw, lnb, s0_bd)


def _moba_prep_kernel(q_ref, k_ref, v_ref, cos_ref, sin_ref, qw_ref, kw_ref, qo_ref, ko_ref, k4_ref, v4_ref):
    cos, sin = cos_ref[...], sin_ref[...]

    def rotate(x, w_ref):
        xn = _rms(x, w_ref[...])
        return xn * cos + pltpu.roll(xn, MOBA_HEAD_DIM // 2, 1) * sin

    for h in range(MOBA_HEADS):
        sl = slice(h * LANE, (h + 1) * LANE)
        qo_ref[:, sl] = rotate(q_ref[:, sl], qw_ref)
        k_h = rotate(k_ref[:, sl], kw_ref)
        ko_ref[:, sl] = k_h
        k4_ref[:, h, :] = k_h
        v4_ref[:, h, :] = v_ref[:, sl]


def _moba_prep(cols, cos, sin, qw, kw, tm):
    rows = cols.shape[0]
    G = GROUP_WIDTH
    tab = pl.BlockSpec((tm, LANE), lambda r: (r, 0))
    vec = pl.BlockSpec((1, LANE), lambda r: (0, 0))
    flat = pl.BlockSpec((tm, G), lambda r: (r, 0))
    heads = pl.BlockSpec((tm, MOBA_HEADS, MOBA_HEAD_DIM), lambda r: (r, 0, 0))
    return pl.pallas_call(
        _moba_prep_kernel,
        grid=(rows // tm,),
        in_specs=[
            pl.BlockSpec((tm, G), lambda r: (r, C_MQ // G)),
            pl.BlockSpec((tm, G), lambda r: (r, C_MK // G)),
            pl.BlockSpec((tm, G), lambda r: (r, C_MV // G)),
            tab, tab, vec, vec,
        ],
        out_specs=[flat, flat, heads, heads],
        out_shape=[jax.ShapeDtypeStruct((rows, G), F32)] * 2
        + [jax.ShapeDtypeStruct((rows, MOBA_HEADS, MOBA_HEAD_DIM), F32)] * 2,
        compiler_params=_params(1),
        name="moba_prep",
    )(cols, cols, cols, cos, sin, qw, kw)


def _top3(gate, lane_f):
    sel = jnp.zeros(gate.shape, jnp.bool_)
    g = gate
    big = float(gate.shape[-1])
    idxs = []
    for _ in range(MOBA_TOPK):
        m = jnp.max(g, axis=-1, keepdims=True)
        idx = jnp.min(jnp.where(g == m, lane_f, big), axis=-1, keepdims=True)
        pick = lane_f == idx
        sel = sel | pick
        g = jnp.where(pick, -jnp.inf, g)
        idxs.append(idx)
    return sel, idxs


def _moba_attn_kernel(q_ref, k_ref, v_ref, ow_ref, y_ref, km_ref, kb_ref, vt_ref, sel_ref, o_ref, *, nb):
    qi = pl.program_id(1)
    BLK = MOBA_BLOCK
    scale = MOBA_HEAD_DIM ** -0.5

    @pl.when(qi == 0)
    def _():
        for j in range(nb):
            rows = slice(j * BLK, (j + 1) * BLK)
            k_blk = k_ref[rows, :]
            km_ref[j:j + 1, :] = jnp.mean(k_blk, axis=0, keepdims=True)
            kb_ref[rows, :] = k_blk.astype(BF16)
            vt_ref[:, rows] = v_ref[rows, :].T.astype(BF16)

    blk_i = _iota((nb, BLK), 0)
    blk_f = blk_i.astype(F32)
    past = blk_i < qi
    causal = _iota((BLK, BLK), 0) <= _iota((BLK, BLK), 1)
    own = pl.multiple_of(qi * BLK, BLK)

    sls = [slice(h * LANE, (h + 1) * LANE) for h in range(MOBA_HEADS)]
    each = lambda f: [f(h) for h in range(MOBA_HEADS)]
    q_t = each(lambda h: q_ref[:, sls[h]].T)
    gate = each(lambda h: jnp.where(past, _dot3(_split2(km_ref[:, sls[h]]), _split2(q_t[h])), NEG))
    sel = each(lambda h: jnp.zeros((nb, BLK), jnp.bool_))
    for _ in range(MOBA_TOPK):
        best = each(lambda h: jnp.max(gate[h], axis=0, keepdims=True))
        idx = each(lambda h: jnp.min(jnp.where(gate[h] == best[h], blk_f, float(nb)), axis=0, keepdims=True))
        sel = each(lambda h: sel[h] | (blk_f == idx[h]))
        gate = each(lambda h: jnp.where(blk_f == idx[h], -jnp.inf, gate[h]))
    for h in range(MOBA_HEADS):
        sel_ref[h] = (sel[h] & past).astype(F32)
    qb_t = each(lambda h: q_t[h].astype(BF16))

    def block_scores(start, h):
        return _dot(kb_ref[pl.ds(start, BLK), sls[h]], qb_t[h]) * scale

    s0 = each(lambda h: jnp.where(causal, block_scores(own, h), NEG))
    m0 = each(lambda h: jnp.max(s0[h], axis=0, keepdims=True))
    p0 = each(lambda h: jnp.exp(s0[h] - m0[h]))
    l0 = each(lambda h: jnp.sum(p0[h], axis=0, keepdims=True))
    acc0 = each(lambda h: _dot(vt_ref[sls[h], pl.ds(own, BLK)], p0[h].astype(BF16)))

    def body(j, carry):
        m_i, l_i, acc = carry
        start = pl.multiple_of(j * BLK, BLK)
        s = each(lambda h: jnp.where(sel_ref[h, pl.ds(j, 1), :] > 0.0, block_scores(start, h), NEG))
        m_n = each(lambda h: jnp.maximum(m_i[h], jnp.max(s[h], axis=0, keepdims=True)))
        alpha = each(lambda h: jnp.exp(m_i[h] - m_n[h]))
        p = each(lambda h: jnp.exp(s[h] - m_n[h]))
        l_n = each(lambda h: alpha[h] * l_i[h] + jnp.sum(p[h], axis=0, keepdims=True))
        acc_n = each(lambda h: alpha[h] * acc[h] + _dot(vt_ref[sls[h], pl.ds(start, BLK)], p[h].astype(BF16)))
        return tuple(m_n), tuple(l_n), tuple(acc_n)

    _, l_f, acc_f = lax.fori_loop(0, qi, body, (tuple(m0), tuple(l0), tuple(acc0)))
    for h in range(MOBA_HEADS):
        o_ref[:, sls[h]] = (acc_f[h] / l_f[h]).T

    y_ref[...] = _rms(o_ref[...], ow_ref[...]).astype(y_ref.dtype)


def _moba_attn(q3, k3, cols3, ow):
    B, T, G = q3.shape
    BLK = MOBA_BLOCK
    nb = T // BLK
    return pl.pallas_call(
        functools.partial(_moba_attn_kernel, nb=nb),
        grid=(B, nb),
        in_specs=[
            pl.BlockSpec((None, BLK, G), lambda b, i: (b, i, 0)),
            pl.BlockSpec((None, T, G), lambda b, i: (b, 0, 0)),
            pl.BlockSpec((None, T, G), lambda b, i: (b, 0, C_MV // G)),
            pl.BlockSpec((1, G), lambda b, i: (0, 0)),
        ],
        out_specs=pl.BlockSpec((None, BLK, G), lambda b, i: (b, i, 0)),
        out_shape=jax.ShapeDtypeStruct((B, T, G), BF16),
        scratch_shapes=[pltpu.VMEM((nb, G), F32), pltpu.VMEM((T, G), BF16), pltpu.VMEM((G, T), BF16),
                        pltpu.VMEM((MOBA_HEADS, nb, BLK), F32), pltpu.VMEM((BLK, G), F32)],
        compiler_params=_params(2),
        name="moba_attn",
    )(q3, k3, cols3, ow)


PAGES_PER_BLOCK = MOBA_BLOCK // PAGE_SIZE
KMEAN_BLOCKS = 8


def _kmean_kernel(pt_ref, *refs):
    page_refs, o_ref = refs[:-1], refs[-1]
    for j in range(KMEAN_BLOCKS):
        s = jnp.sum(page_refs[PAGES_PER_BLOCK * j][...], axis=0)
        for o in range(1, PAGES_PER_BLOCK):
            s = s + jnp.sum(page_refs[PAGES_PER_BLOCK * j + o][...], axis=0)
        o_ref[j] = s * (1.0 / MOBA_BLOCK)


def _kmean_pages(cache_k, page_table_flat, layer, batch, n_pages):
    nbk = n_pages // PAGES_PER_BLOCK
    per_step = KMEAN_BLOCKS * PAGES_PER_BLOCK

    def page(o):
        return pl.BlockSpec((None, None, PAGE_SIZE, MOBA_HEADS, MOBA_HEAD_DIM),
                            lambda b, j, pt: (layer, pt[b * n_pages + j * per_step + o], 0, 0, 0))

    return pl.pallas_call(
        _kmean_kernel,
        grid_spec=pltpu.PrefetchScalarGridSpec(
            num_scalar_prefetch=1,
            grid=(batch, nbk // KMEAN_BLOCKS),
            in_specs=[page(o) for o in range(per_step)],
            out_specs=pl.BlockSpec((None, KMEAN_BLOCKS, MOBA_HEADS, MOBA_HEAD_DIM), lambda b, j, pt: (b, j, 0, 0)),
        ),
        out_shape=jax.ShapeDtypeStruct((batch, nbk, MOBA_HEADS, MOBA_HEAD_DIM), F32),
        compiler_params=_params(2),
        name="moba_kmean_pages",
    )(page_table_flat, *([cache_k] * per_step))


def _select_kernel(q_ref, km_ref, idx_ref, *, batch, t_new, nbk):
    rows = batch * t_new
    lane_i = _iota((rows, LANE), 1)
    lane_f = lane_i.astype(F32)
    row_b = _iota((rows, LANE), 0) // t_new
    out = jnp.zeros((rows, LANE), F32)
    pad = jnp.zeros((LANE - nbk, LANE), F32)
    for h in range(MOBA_HEADS):
        sl = slice(h * LANE, (h + 1) * LANE)
        q = q_ref[:, sl]
        gate = jnp.zeros((rows, LANE), F32)
        for b in range(batch):
            km = jnp.concatenate([km_ref[b, :, h, :], pad], axis=0)
            gate = jnp.where(row_b == b, _dot_t(q, km, HI), gate)
        gate = jnp.where(lane_i < nbk, gate, NEG)
        _, idxs = _top3(gate, lane_f)
        for kth, idx in enumerate(idxs):
            out = jnp.where(lane_i == h * MOBA_HEADS + kth, idx, out)
    idx_ref[...] = out.astype(jnp.int32)


def _select(q_rot, kmean, batch, t_new):
    rows = batch * t_new
    nbk = kmean.shape[1]
    G = GROUP_WIDTH
    return pl.pallas_call(
        functools.partial(_select_kernel, batch=batch, t_new=t_new, nbk=nbk),
        grid=(1,),
        in_specs=[
            pl.BlockSpec((rows, G), lambda i: (0, 0)),
            pl.BlockSpec((batch, nbk, MOBA_HEADS, MOBA_HEAD_DIM), lambda i: (0, 0, 0, 0)),
        ],
        out_specs=pl.BlockSpec((rows, LANE), lambda i: (0, 0)),
        out_shape=jax.ShapeDtypeStruct((rows, LANE), jnp.int32),
        compiler_params=_params(1),
        name="moba_select",
    )(q_rot, kmean)


def _sample_attn_kernel(pt_ref, idx_ref, q_ref, kn_ref, vn_ref, ow_ref, ck_ref, cv_ref, y_ref,
                        kbuf, vbuf, sems, o_ref, *, layer, t_new, n_pages):
    b = pl.program_id(0)
    rows = kn_ref.shape[0]
    scale = MOBA_HEAD_DIM ** -0.5
    per_query = PAGES_PER_BLOCK * MOBA_TOPK
    queries = [(t, h) for t in range(t_new) for h in range(MOBA_HEADS)]

    def page_copies(qn):
        t, h = queries[qn]
        out = []
        for kth in range(MOBA_TOPK):
            blk = idx_ref[(b * t_new + t) * LANE + h * MOBA_HEADS + kth]
            for o in range(PAGES_PER_BLOCK):
                page = pt_ref[b * n_pages + PAGES_PER_BLOCK * blk + o]
                slot = qn * per_query + kth * PAGES_PER_BLOCK + o
                out.append(pltpu.make_async_copy(ck_ref.at[layer, page, :, h, :], kbuf.at[slot], sems.at[0, qn]))
                out.append(pltpu.make_async_copy(cv_ref.at[layer, page, :, h, :], vbuf.at[slot], sems.at[1, qn]))
        return out

    copies = [page_copies(qn) for qn in range(len(queries))]
    for group in copies:
        for cp in group:
            cp.start()

    r_i = _iota((8, rows), 1)
    for qn, (t, h) in enumerate(queries):
        sl = slice(h * LANE, (h + 1) * LANE)
        q8 = jnp.broadcast_to(q_ref[t:t + 1, sl], (8, LANE)).astype(BF16)
        s_new = _dot_t(q8, kn_ref[:, sl].astype(BF16)) * scale
        s_new = jnp.where((r_i >= b * t_new) & (r_i <= b * t_new + t), s_new, NEG)
        for cp in copies[qn]:
            cp.wait()
        slots = range(qn * per_query, (qn + 1) * per_query)
        s_old = [_dot_t(q8, kbuf[slot].astype(BF16)) * scale for slot in slots]
        m = jnp.max(s_new, axis=-1, keepdims=True)
        for s in s_old:
            m = jnp.maximum(m, jnp.max(s, axis=-1, keepdims=True))
        p_new = jnp.exp(s_new - m)
        l = jnp.sum(p_new, axis=-1, keepdims=True)
        acc = _dot(p_new.astype(BF16), vn_ref[:, sl].astype(BF16))
        for s, slot in zip(s_old, slots):
            p = jnp.exp(s - m)
            l = l + jnp.sum(p, axis=-1, keepdims=True)
            acc = acc + _dot(p.astype(BF16), vbuf[slot].astype(BF16))
        o_ref[t:t + 1, sl] = (acc / l)[0:1, :]

    y_ref[...] = _rms(o_ref[...], ow_ref[...]).astype(y_ref.dtype)


def _sample_attn(q_rot, k_new, cols, ow, cache_k, cache_v, page_table_flat, idx_flat, layer, batch, t_new, n_pages):
    rows = batch * t_new
    G = GROUP_WIDTH
    n_slots = t_new * MOBA_HEADS * MOBA_TOPK * PAGES_PER_BLOCK
    return pl.pallas_call(
        functools.partial(_sample_attn_kernel, layer=layer, t_new=t_new, n_pages=n_pages),
        grid_spec=pltpu.PrefetchScalarGridSpec(
            num_scalar_prefetch=2,
            grid=(batch,),
            in_specs=[
                pl.BlockSpec((None, t_new, G), lambda b, pt, idx: (b, 0, 0)),
                pl.BlockSpec((rows, G), lambda b, pt, idx: (0, 0)),
                pl.BlockSpec((rows, G), lambda b, pt, idx: (0, C_MV // G)),
                pl.BlockSpec((1, G), lambda b, pt, idx: (0, 0)),
                pl.BlockSpec(memory_space=pl.ANY),
                pl.BlockSpec(memory_space=pl.ANY),
            ],
            out_specs=pl.BlockSpec((None, t_new, G), lambda b, pt, idx: (b, 0, 0)),
            scratch_shapes=[
                pltpu.VMEM((n_slots, PAGE_SIZE, MOBA_HEAD_DIM), F32),
                pltpu.VMEM((n_slots, PAGE_SIZE, MOBA_HEAD_DIM), F32),
                pltpu.SemaphoreType.DMA((2, t_new * MOBA_HEADS)),
                pltpu.VMEM((t_new, G), F32),
            ],
        ),
        out_shape=jax.ShapeDtypeStruct((batch, t_new, G), BF16),
        compiler_params=_params(1),
        name="moba_sample_attn",
    )(page_table_flat, idx_flat, q_rot.reshape(batch, t_new, G), k_new, cols, ow, cache_k, cache_v)


def _rope_tables(pos):
    half = MOBA_HEAD_DIM // 2
    freq = ROPE_THETA ** (-jnp.arange(half, dtype=F32) / half)
    ang = pos.astype(F32)[:, None] * freq[None, :]
    cos, sin = jnp.cos(ang), jnp.sin(ang)
    return jnp.concatenate([cos, cos], -1), jnp.concatenate([-sin, sin], -1)


def _pad_rows_front(x, rows):
    return jnp.pad(x, ((0, 0), (rows - x.shape[1], 0), (0, 0)))


def _pair_block_diag(s):
    B, H, N, _ = s.shape
    s = s.reshape(B, H // 2, 2, N, 1, N) * jnp.eye(2, dtype=s.dtype)[None, None, :, None, :, None]
    return s.reshape(B, H // 2, 2 * N, 2 * N)


def _pair_diag_blocks(s_bd):
    B, n_pairs, P, _ = s_bd.shape
    N = P // 2
    s = s_bd.reshape(B, n_pairs, 2, N, 2, N)
    return jnp.stack([s[:, :, 0, :, 0, :], s[:, :, 1, :, 1, :]], axis=2).reshape(B, 2 * n_pairs, N, N)


def _layer_weights(W, i):
    G = GROUP_WIDTH
    row = lambda v: v.reshape(1, -1)
    lane_pad = lambda v: jnp.pad(v, (0, LANE - v.shape[0])).reshape(1, LANE)
    zeros_r = jnp.zeros((RWKV_HEAD, G), F32)
    return dict(
        cw8=jnp.pad(W['ssd_conv_w'][i].T, ((0, 8 - SSD_CONV), (0, 0))),
        cb=row(W['ssd_conv_b'][i]),
        dtb=lane_pad(W['ssd_dt_bias'][i]),
        alog=lane_pad(W['ssd_a_log'][i]),
        dskip=row(jnp.repeat(W['ssd_d'][i], GROUP_WIDTH // SSD_HEADS)),
        ssd_nw=row(W['ssd_norm_w'][i]),
        qw=row(W['moba_q_norm_w'][i]), kw=row(W['moba_k_norm_w'][i]), ow=row(W['moba_out_norm_w'][i]),
        gvw=row(W['gmlp_v_norm_w'][i]), gow=row(W['gmlp_out_norm_w'][i]),
        ws=W['gmlp_w_s'][i], bs=W['gmlp_b_s'][i][:, :, None],
        mu=row(W['rwkv_mu'][i]), w0=row(W['rwkv_w0'][i]), a0=row(W['rwkv_a0'][i]),
        wup=jnp.concatenate([W['rwkv_w_up'][i], zeros_r], 0),
        aup=jnp.concatenate([zeros_r, W['rwkv_a_up'][i]], 0),
        gup=W['rwkv_g_up'][i],
        kk=row(W['rwkv_k_k'][i]), ka=row(W['rwkv_k_a'][i]), rk=row(W['rwkv_r_k'][i]),
        lnw=row(W['rwkv_ln_w'][i]), lnb=row(W['rwkv_ln_b'][i]),
    )


def _mixers(cols, B, T, Tp, lw, conv_prev, ssd_prev, rwkv_prev, shift_prev, t_valid):
    G = GROUP_WIDTH
    cols3 = cols.reshape(B, T, COLS)
    if Tp != T:
        cols3 = jnp.pad(cols3, ((0, 0), (0, Tp - T), (0, 0)))
    y_ssd, ssd_new = _ssd(cols3, _pad_rows_front(conv_prev, 8), ssd_prev.reshape(B, SSD_HEADS * SSD_HEADDIM, SSD_STATE),
                          lw['cw8'], lw['cb'], lw['dtb'], lw['alog'], lw['dskip'], lw['ssd_nw'], t_valid)
    y_gm, v_gm = _gmlp(cols3, lw['gvw'], lw['ws'], lw['bs'], lw['gow'])
    y_rw, rwkv_new = _rwkv(cols3, _pad_rows_front(shift_prev[:, None, :], 8), _pair_block_diag(rwkv_prev),
                           lw['mu'], lw['w0'], lw['wup'], lw['a0'], lw['aup'], lw['gup'], lw['kk'], lw['ka'],
                           lw['rk'], lw['lnw'], lw['lnb'], t_valid)
    crop = lambda y: y[:, :T].reshape(B * T, G)
    raw = cols.reshape(B, T, COLS)
    conv_new = raw[:, T - (SSD_CONV - 1):, C_X:C_X + SSD_CONV_DIM]
    shift_new = raw[:, T - 1, C_RR:C_RR + RWKV_COLS]
    states = (ssd_new.reshape(B, SSD_HEADS, SSD_HEADDIM, SSD_STATE), conv_new, _pair_diag_blocks(rwkv_new), shift_new)
    return crop(y_ssd), crop(y_gm), crop(y_rw), v_gm[:, :T], states


def kernel(x_prompt, x_sample, c_prompt, c_sample, cache_k, cache_v, page_table, state_ssd, state_ssd_conv, state_rwkv, state_rwkv_shift, norm_w, w_ada, b_ada, ffn_w1, ffn_w3, ffn_w2, w_in, w_out, ssd_conv_w, ssd_conv_b, ssd_dt_bias, ssd_a_log, ssd_d, ssd_norm_w, moba_q_norm_w, moba_k_norm_w, moba_out_norm_w, gmlp_v_norm_w, gmlp_w_s, gmlp_b_s, gmlp_out_norm_w, rwkv_mu, rwkv_w0, rwkv_w_up, rwkv_a0, rwkv_a_up, rwkv_g_up, rwkv_k_k, rwkv_k_a, rwkv_r_k, rwkv_ln_w, rwkv_ln_b):
    W = dict(ssd_conv_w=ssd_conv_w, ssd_conv_b=ssd_conv_b, ssd_dt_bias=ssd_dt_bias, ssd_a_log=ssd_a_log,
             ssd_d=ssd_d, ssd_norm_w=ssd_norm_w, moba_q_norm_w=moba_q_norm_w, moba_k_norm_w=moba_k_norm_w,
             moba_out_norm_w=moba_out_norm_w, gmlp_v_norm_w=gmlp_v_norm_w, gmlp_w_s=gmlp_w_s, gmlp_b_s=gmlp_b_s,
             gmlp_out_norm_w=gmlp_out_norm_w, rwkv_mu=rwkv_mu, rwkv_w0=rwkv_w0, rwkv_w_up=rwkv_w_up,
             rwkv_a0=rwkv_a0, rwkv_a_up=rwkv_a_up, rwkv_g_up=rwkv_g_up, rwkv_k_k=rwkv_k_k, rwkv_k_a=rwkv_k_a,
             rwkv_r_k=rwkv_r_k, rwkv_ln_w=rwkv_ln_w, rwkv_ln_b=rwkv_ln_b)
    Bp, Tq, D = x_prompt.shape
    Bs, Ts, _ = x_sample.shape
    n_pages = page_table.shape[1]
    assert n_pages * PAGE_SIZE == PAST_LEN and PAST_LEN % MOBA_BLOCK == 0
    assert PAST_LEN // MOBA_BLOCK >= MOBA_TOPK and Ts <= MOBA_BLOCK
    assert n_pages % (KMEAN_BLOCKS * PAGES_PER_BLOCK) == 0 and RWKV_CHUNK == RWKV_HEAD
    Rp, Rs = Bp * Tq, Bs * Ts

    assert w_in.shape == (DEPTH, D, IN_COLS)
    w_in_b = _win_prep(jnp.swapaxes(w_in, 1, 2))
    w_out_b = w_out.astype(BF16)
    nw4 = norm_w.reshape(DEPTH, 3, 1, D)
    b_ada3 = b_ada.reshape(DEPTH, 1, N_MOD * D)

    n_c = Bp + Bs
    c_all = jnp.pad(jnp.concatenate([c_prompt, c_sample], 0), ((0, -n_c % 8), (0, 0)))

    pos_p = jnp.arange(Tq, dtype=jnp.int32)
    pos_s = PAST_LEN + jnp.arange(Ts, dtype=jnp.int32)
    cos_p, sin_p = (jnp.tile(t, (Bp, 1)) for t in _rope_tables(pos_p))
    cos_s, sin_s = (jnp.tile(t, (Bs, 1)) for t in _rope_tables(pos_s))

    pt_flat = page_table.reshape(-1)

    zeros = lambda *s: jnp.zeros(s, F32)
    xp = x_prompt.reshape(Rp, D)
    xs = x_sample.reshape(Rs, D)
    TM = 512
    outs_p, outs_s = [], []
    for i in range(DEPTH):
        lw = _layer_weights(W, i)
        mod = _ada(c_all, w_ada, b_ada3, i).reshape(-1, N_MOD, D)
        mod_p = _Mod(mod[:Bp].reshape(Bp, N_MOD, 1, D), False, Tq, TM)
        mod_s = _Mod(jnp.repeat(mod[Bp:n_c], Ts, axis=0).transpose(1, 0, 2), True, Ts, Rs)

        xs, *ffn_a = _ffn(xs, mod_s, 0, nw4, ffn_w1, ffn_w3, ffn_w2, i, 0, Rs, 512)
        cols = _inproj(xs, mod_s, nw4, w_in_b, i, Rs, 1536)
        q_rot, k_rot, k4, v4 = _moba_prep(cols, cos_s, sin_s, lw['qw'], lw['kw'], Rs)
        y_ssd, y_gm, y_rw, v_gm, st = _mixers(cols, Bs, Ts, SAMPLE_PAD, lw, state_ssd_conv[i], state_ssd[i],
                                              state_rwkv[i], state_rwkv_shift[i], Ts)
        kmean = _kmean_pages(cache_k, pt_flat, i, Bs, n_pages)
        idx = _select(q_rot, kmean, Bs, Ts)
        y_att = _sample_attn(q_rot, k_rot, cols, lw['ow'], cache_k, cache_v, pt_flat, idx.reshape(-1), i, Bs, Ts,
                             n_pages).reshape(Rs, GROUP_WIDTH)
        xs = _outproj(xs, mod_s, (y_ssd, y_att, y_gm, y_rw), w_out_b, i, Rs)
        xs, *ffn_b = _ffn(xs, mod_s, 6, nw4, ffn_w1, ffn_w3, ffn_w2, i, 1, Rs, 512)
        shp = (Bs, Ts, MOBA_HEADS, MOBA_HEAD_DIM)
        outs_s.append((k4.reshape(shp), v4.reshape(shp)) + st + (v_gm,))

        xp = _ffn(xp, mod_p, 0, nw4, *ffn_a, i, 0, TM, 512)
        cols = _inproj(xp, mod_p, nw4, w_in_b, i, TM, 1536)
        q_rot, k_rot, k4, v4 = _moba_prep(cols, cos_p, sin_p, lw['qw'], lw['kw'], TM)
        y_ssd, y_gm, y_rw, _, st = _mixers(cols, Bp, Tq, Tq, lw, zeros(Bp, SSD_CONV - 1, SSD_CONV_DIM),
                                           zeros(Bp, SSD_HEADS, SSD_HEADDIM, SSD_STATE),
                                           zeros(Bp, RWKV_HEADS, RWKV_HEAD, RWKV_HEAD), zeros(Bp, RWKV_COLS), None)
        y_att = _moba_attn(q_rot.reshape(Bp, Tq, -1), k_rot.reshape(Bp, Tq, -1), cols.reshape(Bp, Tq, COLS),
                           lw['ow']).reshape(Rp, -1)
        xp = _outproj(xp, mod_p, (y_ssd, y_att, y_gm, y_rw), w_out_b, i, TM)
        xp = _ffn(xp, mod_p, 6, nw4, *ffn_b, i, 1, TM, 512)
        shp = (Bp, Tq, MOBA_HEADS, MOBA_HEAD_DIM)
        outs_p.append((k4.reshape(shp), v4.reshape(shp)) + st)

    k_p, v_p, ssd_p, conv_p, rwkv_p, shift_p = (jnp.stack(s) for s in zip(*outs_p))
    k_s, v_s, ssd_s, conv_s, rwkv_s, shift_s, gmlp_v_s = (jnp.stack(s) for s in zip(*outs_s))
    return (xp.reshape(Bp, Tq, D), xs.reshape(Bs, Ts, D), k_p, v_p, k_s, v_s, ssd_p, ssd_s, conv_p, conv_s,
            rwkv_p, rwkv_s, shift_p, shift_s, gmlp_v_s)
```

```python
import functools
import math

import jax
import jax.numpy as jnp
from jax import lax
from jax.experimental import pallas as pl
from jax.experimental.pallas import tpu as pltpu

F32 = jnp.float32
BF16 = jnp.bfloat16
HI = lax.Precision.HIGHEST

D_MODEL = 2048
DEPTH = 2
PAST_LEN = 16384
PAGE_SIZE = 128
GROUP_WIDTH = 512
SSD_HEADS = 8
SSD_HEADDIM = 64
SSD_STATE = 128
SSD_CONV = 4
SSD_CHUNK = 128
SSD_CONV_DIM = 1024
MOBA_HEADS = 4
MOBA_HEAD_DIM = 128
MOBA_BLOCK = 256
MOBA_TOPK = 3
ROPE_THETA = 10000.0
GMLP_CHUNK = 128
GMLP_HEADS = 4
GMLP_CHUNKS_PER_STEP = 4
RWKV_HEADS = 8
RWKV_HEAD = 64
RWKV_COLS = 1792
RWKV_CHUNK = 64
RWKV_SUB = 16
RWKV_SEQS_PER_STEP = 4
RWKV_DECAY_SCALE = 0.606531
RWKV_LN_EPS = 64e-5
D_FF = 5632
N_MOD = 9
FFN_RES = 0.5
EPS = 1e-6
NEG = -1e30

COLS = 6144
IN_COLS = 5896
C_Z, C_X, C_BC = 0, 512, 1024
C_MQ, C_MK, C_MV = 1536, 2048, 2560
C_GU, C_GV = 3072, 3584
C_RR, C_RK, C_RV, C_RL = 4096, 4608, 5120, 5632
C_DT = 5888

LANE = 128
SAMPLE_PAD = 128
VMEM_LIMIT = 56 * 1024 * 1024


def _params(n_axes, vmem=VMEM_LIMIT):
    return pltpu.CompilerParams(dimension_semantics=("arbitrary",) * n_axes, vmem_limit_bytes=vmem)


def _dot(a, b, precision=None):
    return jnp.dot(a, b, preferred_element_type=F32, precision=precision)


def _dot_t(a, b, precision=None):
    return lax.dot_general(a, b, (((1,), (1,)), ((), ())), preferred_element_type=F32, precision=precision)


def _dot_0(a, b, precision=None):
    return lax.dot_general(a, b, (((0,), (0,)), ((), ())), preferred_element_type=F32, precision=precision)


def _iota(shape, dim):
    return lax.broadcasted_iota(jnp.int32, shape, dim)


def _rms(x, w):
    return x * lax.rsqrt(jnp.mean(x * x, -1, keepdims=True) + EPS) * w


def _silu(x):
    return x * jax.nn.sigmoid(x)


def _ada_kernel(c_ref, w_ref, b_ref, o_ref):
    s = _silu(c_ref[...]).astype(BF16)
    o_ref[...] = _dot(s, w_ref[...].astype(BF16)) + b_ref[...]


def _ada(c_all, w_ada, b_ada3, layer):
    rows = c_all.shape[0]
    n_out = w_ada.shape[-1]
    tn = 1024
    return pl.pallas_call(
        _ada_kernel,
        grid=(n_out // tn,),
        in_specs=[
            pl.BlockSpec((rows, D_MODEL), lambda n: (0, 0)),
            pl.BlockSpec((None, D_MODEL, tn), lambda n: (layer, 0, n)),
            pl.BlockSpec((None, 1, tn), lambda n: (layer, 0, n)),
        ],
        out_specs=pl.BlockSpec((rows, tn), lambda n: (0, n)),
        out_shape=jax.ShapeDtypeStruct((rows, n_out), F32),
        compiler_params=_params(1),
        name="ada",
    )(c_all, w_ada, b_ada3)


class _Mod:
    def __init__(self, arr, per_row, rows_per_batch, tm):
        self.arr = arr
        self.per_row = per_row
        self.tiles_per_batch = None if per_row else rows_per_batch // tm
        self.tm = tm

    def spec(self, j):
        if self.per_row:
            return pl.BlockSpec((None, self.tm, D_MODEL), lambda r, *_: (j, r, 0))
        tpb = self.tiles_per_batch
        return pl.BlockSpec((None, None, 1, D_MODEL), lambda r, *_: (r // tpb, j, 0, 0))


def _ffn_kernel(x_ref, nw_ref, sh_ref, sc_ref, g_ref, w1_ref, w3_ref, w2_ref, o_ref, *rest, nf, emit_bf16):
    f = pl.program_id(1)
    h_ref, acc_ref = rest[-2:]

    @pl.when(f == 0)
    def _():
        xn = _rms(x_ref[...], nw_ref[...])
        h_ref[...] = (xn * (1 + sc_ref[...]) + sh_ref[...]).astype(BF16)
        acc_ref[...] = jnp.zeros_like(acc_ref)

    w1, w3, w2 = (w[...].astype(BF16) for w in (w1_ref, w3_ref, w2_ref))
    if emit_bf16:
        for dst, w in zip(rest[:3], (w1, w3, w2)):
            dst[...] = w
    h = h_ref[...]
    a = _dot(h, w1)
    b = _dot(h, w3)
    acc_ref[...] += _dot((_silu(a) * b).astype(BF16), w2)

    @pl.when(f == nf - 1)
    def _():
        o_ref[...] = x_ref[...] + FFN_RES * g_ref[...] * acc_ref[...]


def _ffn(x, mod, j0, nw4, w1, w3, w2, layer, slot, tm, tf):
    rows = x.shape[0]
    nf = D_FF // tf
    stacked = w1.ndim == 4
    if stacked:
        up = pl.BlockSpec((None, None, D_MODEL, tf), lambda r, f: (layer, slot, 0, f))
        down = pl.BlockSpec((None, None, tf, D_MODEL), lambda r, f: (layer, slot, f, 0))
    else:
        up = pl.BlockSpec((D_MODEL, tf), lambda r, f: (0, f))
        down = pl.BlockSpec((tf, D_MODEL), lambda r, f: (f, 0))
    out_specs = [pl.BlockSpec((tm, D_MODEL), lambda r, f: (r, 0))]
    out_shape = [jax.ShapeDtypeStruct((rows, D_MODEL), F32)]
    if stacked:
        assert rows == tm
        out_specs += [pl.BlockSpec((D_MODEL, tf), lambda r, f: (0, f))] * 2 + [pl.BlockSpec((tf, D_MODEL), lambda r, f: (f, 0))]
        out_shape += [jax.ShapeDtypeStruct((D_MODEL, D_FF), BF16)] * 2 + [jax.ShapeDtypeStruct((D_FF, D_MODEL), BF16)]
    outs = pl.pallas_call(
        functools.partial(_ffn_kernel, nf=nf, emit_bf16=stacked),
        grid=(rows // tm, nf),
        in_specs=[
            pl.BlockSpec((tm, D_MODEL), lambda r, f: (r, 0)),
            pl.BlockSpec((None, None, 1, D_MODEL), lambda r, f: (layer, 2 * slot, 0, 0)),
            mod.spec(j0), mod.spec(j0 + 1), mod.spec(j0 + 2),
            up, up, down,
        ],
        out_specs=out_specs,
        out_shape=out_shape,
        scratch_shapes=[pltpu.VMEM((tm, D_MODEL), BF16), pltpu.VMEM((tm, D_MODEL), F32)],
        compiler_params=_params(2),
        name="ffn",
    )(x, nw4, mod.arr, mod.arr, mod.arr, w1, w3, w2)
    return outs if stacked else outs[0]


def _inproj_kernel(x_ref, nw_ref, sh_ref, sc_ref, w_ref, o_ref, h_ref):
    @pl.when(pl.program_id(1) == 0)
    def _():
        xn = _rms(x_ref[...], nw_ref[...])
        h_ref[...] = (xn * (1 + sc_ref[...]) + sh_ref[...]).astype(BF16)

    o_ref[...] = _dot_t(h_ref[...], w_ref[...])


def _inproj(x, mod, nw4, w_in_t, layer, tm, tn):
    rows = x.shape[0]
    return pl.pallas_call(
        _inproj_kernel,
        grid=(rows // tm, COLS // tn),
        in_specs=[
            pl.BlockSpec((tm, D_MODEL), lambda r, n: (r, 0)),
            pl.BlockSpec((None, None, 1, D_MODEL), lambda r, n: (layer, 1, 0, 0)),
            mod.spec(3), mod.spec(4),
            pl.BlockSpec((None, tn, D_MODEL), lambda r, n: (layer, n, 0)),
        ],
        out_specs=pl.BlockSpec((tm, tn), lambda r, n: (r, n)),
        out_shape=jax.ShapeDtypeStruct((rows, COLS), F32),
        scratch_shapes=[pltpu.VMEM((tm, D_MODEL), BF16)],
        compiler_params=_params(2),
        name="inproj",
    )(x, nw4, mod.arr, mod.arr, w_in_t)


WIN_TILE = 512


def _win_prep_kernel(w_ref, dt_ref, o_ref, *, n_tiles, tail):
    j = pl.program_id(1)

    @pl.when(j < n_tiles - 1)
    def _():
        o_ref[...] = w_ref[0].astype(BF16)

    @pl.when(j == n_tiles - 1)
    def _():
        o_ref[0:tail, :] = w_ref[0, WIN_TILE - tail:WIN_TILE, :].astype(BF16)
        rest = jnp.concatenate([dt_ref[0], jnp.zeros((WIN_TILE - tail - SSD_HEADS, D_MODEL), F32)], axis=0)
        o_ref[tail:WIN_TILE, :] = rest.astype(BF16)


def _win_prep(w_in_t):
    head = C_BC + GROUP_WIDTH
    n_tiles = COLS // WIN_TILE
    tail = C_DT % WIN_TILE
    assert head % WIN_TILE == 0 and SSD_HEADS == 8 and 0 < tail < WIN_TILE

    def src_row(j):
        shifted = jnp.minimum(j * WIN_TILE + SSD_HEADS, IN_COLS - WIN_TILE)
        return pl.multiple_of(jnp.where(j * WIN_TILE < head, j * WIN_TILE, shifted), SSD_HEADS)

    return pl.pallas_call(
        functools.partial(_win_prep_kernel, n_tiles=n_tiles, tail=tail),
        grid=(DEPTH, n_tiles),
        in_specs=[
            pl.BlockSpec((pl.Element(1), pl.Element(WIN_TILE), pl.Element(D_MODEL)), lambda i, j: (i, src_row(j), 0)),
            pl.BlockSpec((pl.Element(1), pl.Element(SSD_HEADS), pl.Element(D_MODEL)), lambda i, j: (i, head, 0)),
        ],
        out_specs=pl.BlockSpec((None, WIN_TILE, D_MODEL), lambda i, j: (i, j, 0)),
        out_shape=jax.ShapeDtypeStruct((DEPTH, COLS, D_MODEL), BF16),
        compiler_params=_params(2),
        name="w_in_prep",
    )(w_in_t, w_in_t)


def _outproj_kernel(x_ref, g_ref, y0_ref, y1_ref, y2_ref, y3_ref, w_ref, o_ref):
    G = GROUP_WIDTH
    acc = _dot(y0_ref[...], w_ref[0:G, :])
    acc += _dot(y1_ref[...], w_ref[G:2 * G, :])
    acc += _dot(y2_ref[...], w_ref[2 * G:3 * G, :])
    acc += _dot(y3_ref[...], w_ref[3 * G:4 * G, :])
    o_ref[...] = x_ref[...] + g_ref[...] * acc


def _outproj(x, mod, ys, w_out, layer, tm):
    rows = x.shape[0]
    yspec = pl.BlockSpec((tm, GROUP_WIDTH), lambda r: (r, 0))
    return pl.pallas_call(
        _outproj_kernel,
        grid=(rows // tm,),
        in_specs=[
            pl.BlockSpec((tm, D_MODEL), lambda r: (r, 0)),
            mod.spec(5), yspec, yspec, yspec, yspec,
            pl.BlockSpec((None, D_MODEL, D_MODEL), lambda r: (layer, 0, 0)),
        ],
        out_specs=pl.BlockSpec((tm, D_MODEL), lambda r: (r, 0)),
        out_shape=jax.ShapeDtypeStruct((rows, D_MODEL), F32),
        compiler_params=_params(1),
        name="outproj",
    )(x, mod.arr, *ys, w_out)


def _softplus(x):
    return jnp.maximum(x, 0.0) + jnp.log1p(jnp.exp(-jnp.abs(x)))


def _ssd_kernel(z_ref, x_ref, bc_ref, dt_ref, prev_ref, cw_ref, cb_ref, dtb_ref, alog_ref, dskip_ref, nw_ref,
                h0_ref, y_ref, hout_ref, ext_ref, st_ref, *, C, nc, t_valid):
    c = pl.program_id(1)
    G = GROUP_WIDTH

    @pl.when(c == 0)
    def _():
        ext_ref[0:8, :] = prev_ref[...]
        st_ref[...] = h0_ref[...]

    ext_ref[8:8 + C, 0:G] = x_ref[...]
    ext_ref[8:8 + C, G:2 * G] = bc_ref[...]
    conv = cb_ref[...] + ext_ref[5:5 + C, :] * cw_ref[0:1, :]
    for i in range(1, SSD_CONV):
        conv = conv + ext_ref[5 + i:5 + i + C, :] * cw_ref[i:i + 1, :]
    ext_ref[0:8, :] = ext_ref[C:C + 8, :]
    xbc = _silu(conv)
    xs = xbc[:, 0:G]

    dt = _softplus(dt_ref[...] + dtb_ref[...])
    if t_valid is not None:
        dt = jnp.where(c * C + _iota((C, LANE), 0) < t_valid, dt, 0.0)
    a = dt * (-jnp.exp(alog_ref[...]))
    tri = _iota((C, C), 0) >= _iota((C, C), 1)
    tri_b = tri.astype(BF16)
    a_hi, a_mid, a_lo = _split3(a)
    a_cs = _dg(tri_b, a_hi, _NN) + (_dg(tri_b, a_mid, _NN) + _dg(tri_b, a_lo, _NN))
    a_cs_t = a_cs.T
    lane_lo = _iota((C, LANE), 1) < SSD_HEADDIM
    row_lo = _iota((LANE, 1), 0) < SSD_HEADDIM
    col = lambda h: a_cs[:, h:h + 1]
    row = lambda h: a_cs_t[h:h + 1, :]
    last = lambda h: a_cs[C - 1:C, h:h + 1]
    by_head = lambda p, f: jnp.where(lane_lo, f(2 * p), f(2 * p + 1))
    decay_mat = lambda h: jnp.exp(jnp.where(tri, col(h) - row(h), -jnp.inf))

    n_groups = 2
    heads_per_group = SSD_HEADS // n_groups
    bm_s = [_split2(xbc[:, G + g * SSD_STATE:G + (g + 1) * SSD_STATE]) for g in range(n_groups)]
    cm_s = [_split2(xbc[:, G + (n_groups + g) * SSD_STATE:G + (n_groups + g + 1) * SSD_STATE]) for g in range(n_groups)]
    scores = [_dot3(cm_s[g], bm_s[g], _NT) for g in range(n_groups)]
    group = lambda p: 2 * p // heads_per_group
    each = lambda f: [f(p) for p in range(SSD_HEADS // 2)]
    m0_s = each(lambda p: _split2(scores[group(p)] * decay_mat(2 * p)))
    m1_s = each(lambda p: _split2(scores[group(p)] * decay_mat(2 * p + 1)))
    xs_p = each(lambda p: xs[:, p * LANE:(p + 1) * LANE])
    xdt = each(lambda p: xs_p[p] * by_head(p, lambda h: dt[:, h:h + 1]))
    xdt_s = each(lambda p: _split2(xdt[p]))
    y_diag = each(lambda p: jnp.where(lane_lo, _dot3(m0_s[p], xdt_s[p]), _dot3(m1_s[p], xdt_s[p])))
    st = each(lambda p: st_ref[p * LANE:(p + 1) * LANE, :])
    y_off = each(lambda p: _dot3(cm_s[group(p)], _split2(st[p]), _NT) * by_head(p, lambda h: jnp.exp(col(h))))
    decay = each(lambda p: by_head(p, lambda h: jnp.exp(last(h) - col(h))))
    new = each(lambda p: _dot3(_split2(xdt[p] * decay[p]), bm_s[group(p)], _TN))
    for p in range(SSD_HEADS // 2):
        keep = jnp.where(row_lo, jnp.exp(last(2 * p)), jnp.exp(last(2 * p + 1)))
        st_ref[p * LANE:(p + 1) * LANE, :] = st[p] * keep + new[p]
    ys = each(lambda p: y_diag[p] + y_off[p] + xs_p[p] * dskip_ref[:, p * LANE:(p + 1) * LANE])

    y = jnp.concatenate(ys, axis=1) * _silu(z_ref[...])
    y_ref[...] = _rms(y, nw_ref[...]).astype(y_ref.dtype)

    @pl.when(c == nc - 1)
    def _():
        hout_ref[...] = st_ref[...]


def _ssd(cols3, prev8, h0, cw8, cb, dtb, alog, dskip, nw, t_valid):
    B, T, _ = cols3.shape
    C = SSD_CHUNK
    nc = T // C
    G = GROUP_WIDTH
    vec = lambda n: pl.BlockSpec((1, n), lambda b, c: (0, 0))
    return pl.pallas_call(
        functools.partial(_ssd_kernel, C=C, nc=nc, t_valid=t_valid),
        grid=(B, nc),
        in_specs=[
            pl.BlockSpec((None, C, G), lambda b, c: (b, c, C_Z // G)),
            pl.BlockSpec((None, C, G), lambda b, c: (b, c, C_X // G)),
            pl.BlockSpec((None, C, G), lambda b, c: (b, c, C_BC // G)),
            pl.BlockSpec((None, C, LANE), lambda b, c: (b, c, C_DT // LANE)),
            pl.BlockSpec((None, 8, SSD_CONV_DIM), lambda b, c: (b, 0, 0)),
            pl.BlockSpec((8, SSD_CONV_DIM), lambda b, c: (0, 0)),
            vec(SSD_CONV_DIM), vec(LANE), vec(LANE), vec(G), vec(G),
            pl.BlockSpec((None, SSD_HEADS * SSD_HEADDIM, SSD_STATE), lambda b, c: (b, 0, 0)),
        ],
        out_specs=[
            pl.BlockSpec((None, C, G), lambda b, c: (b, c, 0)),
            pl.BlockSpec((None, SSD_HEADS * SSD_HEADDIM, SSD_STATE), lambda b, c: (b, 0, 0)),
        ],
        out_shape=[
            jax.ShapeDtypeStruct((B, T, G), BF16),
            jax.ShapeDtypeStruct((B, SSD_HEADS * SSD_HEADDIM, SSD_STATE), F32),
        ],
        scratch_shapes=[pltpu.VMEM((C + 8, SSD_CONV_DIM), F32), pltpu.VMEM((SSD_HEADS * SSD_HEADDIM, SSD_STATE), F32)],
        compiler_params=_params(2),
        name="ssd",
    )(cols3, cols3, cols3, cols3, prev8, cw8, cb, dtb, alog, dskip, nw, h0)


def _gmlp_kernel(u_ref, v_ref, vw_ref, ws_ref, bs_ref, ow_ref, y_ref, vout_ref, *, C, n_sub):
    u = jax.nn.gelu(u_ref[...])
    v = _rms(jax.nn.gelu(v_ref[...]), vw_ref[...])
    vout_ref[...] = v
    tri = _iota((C, C), 0) >= _iota((C, C), 1)
    mixed = [[None] * GMLP_HEADS for _ in range(n_sub)]
    for h in range(GMLP_HEADS):
        ws = jnp.where(tri, ws_ref[h], 0.0)
        v_h = jnp.concatenate([v[s * C:(s + 1) * C, h * LANE:(h + 1) * LANE] for s in range(n_sub)], axis=1)
        m_h = _dot3(_split2(ws), _split2(v_h)) + bs_ref[h]
        for s in range(n_sub):
            mixed[s][h] = m_h[:, s * LANE:(s + 1) * LANE]
    y = u * jnp.concatenate([jnp.concatenate(row, axis=1) for row in mixed], axis=0)
    y_ref[...] = _rms(y, ow_ref[...]).astype(y_ref.dtype)


def _gmlp(cols3, vw, ws, bs, ow):
    B, T, _ = cols3.shape
    C = GMLP_CHUNK
    G = GROUP_WIDTH
    vec = pl.BlockSpec((1, G), lambda b, c: (0, 0))
    n_sub = math.gcd(T // C, GMLP_CHUNKS_PER_STEP)
    rows = n_sub * C
    return pl.pallas_call(
        functools.partial(_gmlp_kernel, C=C, n_sub=n_sub),
        grid=(B, T // rows),
        in_specs=[
            pl.BlockSpec((None, rows, G), lambda b, c: (b, c, C_GU // G)),
            pl.BlockSpec((None, rows, G), lambda b, c: (b, c, C_GV // G)),
            vec,
            pl.BlockSpec((GMLP_HEADS, C, C), lambda b, c: (0, 0, 0)),
            pl.BlockSpec((GMLP_HEADS, C, 1), lambda b, c: (0, 0, 0)),
            vec,
        ],
        out_specs=[pl.BlockSpec((None, rows, G), lambda b, c: (b, c, 0))] * 2,
        out_shape=[jax.ShapeDtypeStruct((B, T, G), BF16), jax.ShapeDtypeStruct((B, T, G), F32)],
        compiler_params=_params(2),
        name="gmlp",
    )(cols3, cols3, vw, ws, bs, ow)


_NN = ((1,), (0,))
_NT = ((1,), (1,))
_TN = ((0,), (0,))


def _split2(x):
    hi = x.astype(BF16)
    return hi, (x - hi.astype(F32)).astype(BF16)


def _split3(x):
    hi = x.astype(BF16)
    rest = x - hi.astype(F32)
    mid = rest.astype(BF16)
    return hi, mid, (rest - mid.astype(F32)).astype(BF16)


def _dg(a, b, dims):
    return lax.dot_general(a, b, (dims, ((), ())), preferred_element_type=F32)


def _dot3(a, b, dims=_NN):
    (ah, al), (bh, bl) = a, b
    return _dg(ah, bh, dims) + (_dg(ah, bl, dims) + _dg(al, bh, dims))


def _block_diag(pieces, same_head):
    return tuple(jnp.where(same_head, jnp.concatenate([x, x], axis=0), jnp.zeros((), x.dtype)) for x in pieces)


def _rwkv_kernel(r_ref, k_ref, v_ref, lo_ref, prev_ref, mu_ref, w0_ref, wup_ref, a0_ref, aup_ref, gup_ref,
                 kk_ref, ka_ref, rk_ref, lnw_ref, lnb_ref, s0_ref, y_ref, sout_ref, ext_ref, st_ref,
                 *, C, nc, t_valid):
    c = pl.program_id(1)
    G = GROUP_WIDTH
    N = RWKV_HEAD
    P = 2 * N

    n_batch = r_ref.shape[0]
    n_pairs = G // P
    sls = [slice(p * P, (p + 1) * P) for p in range(n_pairs)]
    same_head_n = (_iota((P, P), 0) < N) == (_iota((P, P), 1) < N)
    ones_bd = same_head_n.astype(BF16)
    tri_cc = (_iota((C, C), 0) >= _iota((C, C), 1)).astype(BF16)

    @pl.when(c == 0)
    def _():
        ext_ref[:, 0:8, :] = prev_ref[...]
        st_ref[...] = s0_ref[...]

    def head_sum(x):
        def pair_sum(p):
            hi, lo = _split2(x[:, sls[p]])
            return _dg(hi, ones_bd, _NN) + _dg(lo, ones_bd, _NN)
        return jnp.concatenate([pair_sum(p) for p in range(n_pairs)], axis=1)

    def token_shift(n):
        ext_ref[n, 8:8 + C, 0:G] = r_ref[n]
        ext_ref[n, 8:8 + C, G:2 * G] = k_ref[n]
        ext_ref[n, 8:8 + C, 2 * G:3 * G] = v_ref[n]
        ext_ref[n, 8:8 + C, 3 * G:RWKV_COLS] = lo_ref[n]
        cur = ext_ref[n, 8:8 + C, :]
        prev = ext_ref[n, 7:7 + C, :]
        xs = cur + (prev - cur) * mu_ref[...]
        ext_ref[n, 0:8, :] = ext_ref[n, C:C + 8, :]
        return xs

    shifted = [token_shift(n) for n in range(n_batch)]
    la_all = jnp.concatenate([xs[:, 3 * G:3 * G + P] for xs in shifted], axis=0)
    gl_all = jnp.concatenate([xs[:, 3 * G + P:RWKV_COLS] for xs in shifted], axis=0)
    w_log_all = -RWKV_DECAY_SCALE * jax.nn.sigmoid(
        w0_ref[...] + _dot3(_split2(jnp.tanh(la_all)), _split2(wup_ref[...])))
    a_all = jax.nn.sigmoid(a0_ref[...] + _dot3(_split2(la_all), _split2(aup_ref[...])))
    g_all = _dot3(_split2(jax.nn.sigmoid(gl_all)), _split2(gup_ref[...]))

    def chunk_inputs(n):
        xs = shifted[n]
        r, k, v = xs[:, 0:G], xs[:, G:2 * G], xs[:, 2 * G:3 * G]
        w_log, a, g = (t[n * C:(n + 1) * C] for t in (w_log_all, a_all, g_all))

        kk = k * kk_ref[...]
        kk = kk * lax.rsqrt(jnp.maximum(head_sum(kk * kk), 1e-12))
        k2 = k * (1 + (a - 1) * ka_ref[...])
        if t_valid is not None:
            ok = c * C + _iota((C, G), 0) < t_valid
            w_log = jnp.where(ok, w_log, 0.0)
            kk = jnp.where(ok, kk, 0.0)
            k2 = jnp.where(ok, k2, 0.0)
        b = kk * a

        w_hi, w_mid, w_lo = _split3(w_log)
        cl = _dg(tri_cc, w_hi, _NN) + (_dg(tri_cc, w_mid, _NN) + _dg(tri_cc, w_lo, _NN))
        cl_last = cl[C - 1:C, :]
        einv = jnp.exp(-cl)
        e_c = jnp.exp(cl_last - cl)
        return dict(r=r, v=v, k2=k2, g=g, kkp=kk * jnp.exp(cl - w_log), rp=r * jnp.exp(cl), bi=b * einv,
                    ki=k2 * einv, bt=b * e_c, kt=k2 * e_c, p_c=jnp.exp(cl_last))

    seqs = [chunk_inputs(n) for n in range(n_batch)]
    units = [(n, p) for n in range(n_batch) for p in range(n_pairs)]
    each = lambda f: [f(u) for u in range(len(units))]
    tile = lambda name: each(lambda u: seqs[units[u][0]][name][:, sls[units[u][1]]])
    kkp, rp, bi, ki, bt, kt, v_p = (tile(name) for name in ("kkp", "rp", "bi", "ki", "bt", "kt", "v"))

    t_i = _iota((C, P), 0)
    s_i = _iota((C, P), 1) % C
    strict = s_i < t_i
    incl = s_i <= t_i
    diag_blk = (s_i // RWKV_SUB) == (t_i // RWKV_SUB)
    eye = (s_i == t_i).astype(F32)
    same_head = (_iota((P, P), 0) < C) == (_iota((P, P), 1) < C)

    memo = {}

    def sp(x):
        if id(x) not in memo:
            memo[id(x)] = (x, _split2(x))
        return memo[id(x)][1]

    def bd(x):
        if ("bd", id(x)) not in memo:
            memo["bd", id(x)] = (x, _block_diag(sp(x), same_head))
        return memo["bd", id(x)][1]

    def mm(xs, ys):
        return each(lambda p: _dot3(sp(xs[p]), bd(ys[p])))

    kkp_s = each(lambda p: _split2(kkp[p]))
    rp_s = each(lambda p: _split2(rp[p]))
    lhs_s = each(lambda p: tuple(jnp.concatenate([x, y], axis=0) for x, y in zip(kkp_s[p], rp_s[p])))
    ab = each(lambda p: _dot3(lhs_s[p], _block_diag(_split2(bi[p]), same_head_n), _NT))
    ak = each(lambda p: _dot3(lhs_s[p], _block_diag(_split2(ki[p]), same_head_n), _NT))
    a_m = each(lambda p: jnp.where(strict, ab[p][0:C], 0.0))
    b_k = each(lambda p: jnp.where(strict, ak[p][0:C], 0.0))
    r_b = each(lambda p: jnp.where(incl, ab[p][C:2 * C], 0.0))
    r_k = each(lambda p: jnp.where(incl, ak[p][C:2 * C], 0.0))

    n_pow = each(lambda p: jnp.where(diag_blk, -a_m[p], 0.0))
    a_o = each(lambda p: jnp.where(diag_blk, 0.0, a_m[p]))
    t_d = each(lambda p: eye + n_pow[p])
    for _ in range(int(math.log2(RWKV_SUB)) - 1):
        n_pow = mm(n_pow, n_pow)
        step = mm(t_d, n_pow)
        t_d = each(lambda p: t_d[p] + step[p])
    m1 = mm(t_d, a_o)
    m2 = mm(m1, m1)
    im = each(lambda p: eye - m1[p])
    im_m2 = mm(im, m2)
    t_full = mm(each(lambda p: im[p] + im_m2[p]), t_d)

    st = each(lambda u: st_ref[units[u]])
    st_s = each(lambda p: _split2(st[p]))
    bkv = mm(b_k, v_p)
    rhs = each(lambda p: _dot3(kkp_s[p], st_s[p], _NT) + bkv[p])
    u = mm(t_full, rhs)
    rkv = mm(r_k, v_p)
    rbu = mm(r_b, u)
    ys = each(lambda p: _dot3(rp_s[p], st_s[p], _NT) + rkv[p] - rbu[p])
    upd = each(lambda p: _dot3(_split2(jnp.concatenate([v_p[p], -u[p]], axis=0)),
                               _split2(jnp.concatenate([kt[p], bt[p]], axis=0)), _TN))
    for u_i, (n, p) in enumerate(units):
        st_ref[n, p] = st[u_i] * seqs[n]["p_c"][:, sls[p]] + jnp.where(same_head_n, upd[u_i], 0.0)

    for n, seq in enumerate(seqs):
        y = jnp.concatenate(ys[n * n_pairs:(n + 1) * n_pairs], axis=1)
        mean = head_sum(y) * (1.0 / N)
        d = y - mean
        var = head_sum(d * d) * (1.0 / N)
        yn = d * lax.rsqrt(var + RWKV_LN_EPS) * lnw_ref[...] + lnb_ref[...]
        bonus = head_sum(seq["r"] * seq["k2"] * rk_ref[...]) * seq["v"]
        y_ref[n] = ((yn + bonus) * seq["g"]).astype(y_ref.dtype)

    @pl.when(c == nc - 1)
    def _():
        sout_ref[...] = st_ref[...]


def _rwkv(cols3, prev8, s0_bd, mu, w0, wup, a0, aup, gup, kk, ka, rk, lnw, lnb, t_valid):
    B, T, _ = cols3.shape
    C = RWKV_CHUNK
    nc = T // C if t_valid is None else -(-t_valid // C)
    G = GROUP_WIDTH
    P = 2 * RWKV_HEAD
    n_pairs = RWKV_HEADS // 2
    S = RWKV_SEQS_PER_STEP
    vec = lambda n: pl.BlockSpec((1, n), lambda b, c: (0, 0))
    mat = lambda m, n: pl.BlockSpec((m, n), lambda b, c: (0, 0))
    return pl.pallas_call(
        functools.partial(_rwkv_kernel, C=C, nc=nc, t_valid=t_valid),
        grid=(B // S, nc),
        in_specs=[
            pl.BlockSpec((S, C, G), lambda b, c: (b, c, C_RR // G)),
            pl.BlockSpec((S, C, G), lambda b, c: (b, c, C_RK // G)),
            pl.BlockSpec((S, C, G), lambda b, c: (b, c, C_RV // G)),
            pl.BlockSpec((S, C, 2 * P), lambda b, c: (b, c, C_RL // (2 * P))),
            pl.BlockSpec((S, 8, RWKV_COLS), lambda b, c: (b, 0, 0)),
            vec(RWKV_COLS), vec(G), mat(P, G), vec(G), mat(P, G), mat(P, G),
            vec(G), vec(G), vec(G), vec(G), vec(G),
            pl.BlockSpec((S, n_pairs, P, P), lambda b, c: (b, 0, 0, 0)),
        ],
        out_specs=[
            pl.BlockSpec((S, C, G), lambda b, c: (b, c, 0)),
            pl.BlockSpec((S, n_pairs, P, P), lambda b, c: (b, 0, 0, 0)),
        ],
        out_shape=[
            jax.ShapeDtypeStruct((B, nc * C, G), BF16),
            jax.ShapeDtypeStruct((B, n_pairs, P, P), F32),
        ],
        scratch_shapes=[pltpu.VMEM((S, C + 8, RWKV_COLS), F32), pltpu.VMEM((S, n_pairs, P, P), F32)],
        compiler_params=_params(2),
        name="rwkv",
    )(cols3, cols3, cols3, cols3, prev8, mu, w0, wup, a0, aup, gup, kk, ka, rk, lnw, lnb, s0_bd)


def _moba_prep_kernel(q_ref, k_ref, v_ref, cos_ref, sin_ref, qw_ref, kw_ref, qo_ref, ko_ref, k4_ref, v4_ref):
    cos, sin = cos_ref[...], sin_ref[...]

    def rotate(x, w_ref):
        xn = _rms(x, w_ref[...])
        return xn * cos + pltpu.roll(xn, MOBA_HEAD_DIM // 2, 1) * sin

    for h in range(MOBA_HEADS):
        sl = slice(h * LANE, (h + 1) * LANE)
        qo_ref[:, sl] = rotate(q_ref[:, sl], qw_ref)
        k_h = rotate(k_ref[:, sl], kw_ref)
        ko_ref[:, sl] = k_h
        k4_ref[:, h, :] = k_h
        v4_ref[:, h, :] = v_ref[:, sl]


def _moba_prep(cols, cos, sin, qw, kw, tm):
    rows = cols.shape[0]
    G = GROUP_WIDTH
    tab = pl.BlockSpec((tm, LANE), lambda r: (r, 0))
    vec = pl.BlockSpec((1, LANE), lambda r: (0, 0))
    flat = pl.BlockSpec((tm, G), lambda r: (r, 0))
    heads = pl.BlockSpec((tm, MOBA_HEADS, MOBA_HEAD_DIM), lambda r: (r, 0, 0))
    return pl.pallas_call(
        _moba_prep_kernel,
        grid=(rows // tm,),
        in_specs=[
            pl.BlockSpec((tm, G), lambda r: (r, C_MQ // G)),
            pl.BlockSpec((tm, G), lambda r: (r, C_MK // G)),
            pl.BlockSpec((tm, G), lambda r: (r, C_MV // G)),
            tab, tab, vec, vec,
        ],
        out_specs=[flat, flat, heads, heads],
        out_shape=[jax.ShapeDtypeStruct((rows, G), F32)] * 2
        + [jax.ShapeDtypeStruct((rows, MOBA_HEADS, MOBA_HEAD_DIM), F32)] * 2,
        compiler_params=_params(1),
        name="moba_prep",
    )(cols, cols, cols, cos, sin, qw, kw)


def _top3(gate, lane_f):
    sel = jnp.zeros(gate.shape, jnp.bool_)
    g = gate
    big = float(gate.shape[-1])
    idxs = []
    for _ in range(MOBA_TOPK):
        m = jnp.max(g, axis=-1, keepdims=True)
        idx = jnp.min(jnp.where(g == m, lane_f, big), axis=-1, keepdims=True)
        pick = lane_f == idx
        sel = sel | pick
        g = jnp.where(pick, -jnp.inf, g)
        idxs.append(idx)
    return sel, idxs


def _moba_attn_kernel(q_ref, k_ref, v_ref, ow_ref, y_ref, km_ref, kb_ref, vt_ref, sel_ref, o_ref, *, nb):
    qi = pl.program_id(1)
    BLK = MOBA_BLOCK
    scale = MOBA_HEAD_DIM ** -0.5

    @pl.when(qi == 0)
    def _():
        for j in range(nb):
            rows = slice(j * BLK, (j + 1) * BLK)
            k_blk = k_ref[rows, :]
            km_ref[j:j + 1, :] = jnp.mean(k_blk, axis=0, keepdims=True)
            kb_ref[rows, :] = k_blk.astype(BF16)
            vt_ref[:, rows] = v_ref[rows, :].T.astype(BF16)

    blk_i = _iota((nb, BLK), 0)
    blk_f = blk_i.astype(F32)
    past = blk_i < qi
    causal = _iota((BLK, BLK), 0) <= _iota((BLK, BLK), 1)
    own = pl.multiple_of(qi * BLK, BLK)

    sls = [slice(h * LANE, (h + 1) * LANE) for h in range(MOBA_HEADS)]
    each = lambda f: [f(h) for h in range(MOBA_HEADS)]
    q_t = each(lambda h: q_ref[:, sls[h]].T)
    gate = each(lambda h: jnp.where(past, _dot3(_split2(km_ref[:, sls[h]]), _split2(q_t[h])), NEG))
    sel = each(lambda h: jnp.zeros((nb, BLK), jnp.bool_))
    for _ in range(MOBA_TOPK):
        best = each(lambda h: jnp.max(gate[h], axis=0, keepdims=True))
        idx = each(lambda h: jnp.min(jnp.where(gate[h] == best[h], blk_f, float(nb)), axis=0, keepdims=True))
        sel = each(lambda h: sel[h] | (blk_f == idx[h]))
        gate = each(lambda h: jnp.where(blk_f == idx[h], -jnp.inf, gate[h]))
    for h in range(MOBA_HEADS):
        sel_ref[h] = (sel[h] & past).astype(F32)
    qb_t = each(lambda h: q_t[h].astype(BF16))

    def block_scores(start, h):
        return _dot(kb_ref[pl.ds(start, BLK), sls[h]], qb_t[h]) * scale

    s0 = each(lambda h: jnp.where(causal, block_scores(own, h), NEG))
    m0 = each(lambda h: jnp.max(s0[h], axis=0, keepdims=True))
    p0 = each(lambda h: jnp.exp(s0[h] - m0[h]))
    l0 = each(lambda h: jnp.sum(p0[h], axis=0, keepdims=True))
    acc0 = each(lambda h: _dot(vt_ref[sls[h], pl.ds(own, BLK)], p0[h].astype(BF16)))

    def body(j, carry):
        m_i, l_i, acc = carry
        start = pl.multiple_of(j * BLK, BLK)
        s = each(lambda h: jnp.where(sel_ref[h, pl.ds(j, 1), :] > 0.0, block_scores(start, h), NEG))
        m_n = each(lambda h: jnp.maximum(m_i[h], jnp.max(s[h], axis=0, keepdims=True)))
        alpha = each(lambda h: jnp.exp(m_i[h] - m_n[h]))
        p = each(lambda h: jnp.exp(s[h] - m_n[h]))
        l_n = each(lambda h: alpha[h] * l_i[h] + jnp.sum(p[h], axis=0, keepdims=True))
        acc_n = each(lambda h: alpha[h] * acc[h] + _dot(vt_ref[sls[h], pl.ds(start, BLK)], p[h].astype(BF16)))
        return tuple(m_n), tuple(l_n), tuple(acc_n)

    _, l_f, acc_f = lax.fori_loop(0, qi, body, (tuple(m0), tuple(l0), tuple(acc0)))
    for h in range(MOBA_HEADS):
        o_ref[:, sls[h]] = (acc_f[h] / l_f[h]).T

    y_ref[...] = _rms(o_ref[...], ow_ref[...]).astype(y_ref.dtype)


def _moba_attn(q3, k3, cols3, ow):
    B, T, G = q3.shape
    BLK = MOBA_BLOCK
    nb = T // BLK
    return pl.pallas_call(
        functools.partial(_moba_attn_kernel, nb=nb),
        grid=(B, nb),
        in_specs=[
            pl.BlockSpec((None, BLK, G), lambda b, i: (b, i, 0)),
            pl.BlockSpec((None, T, G), lambda b, i: (b, 0, 0)),
            pl.BlockSpec((None, T, G), lambda b, i: (b, 0, C_MV // G)),
            pl.BlockSpec((1, G), lambda b, i: (0, 0)),
        ],
        out_specs=pl.BlockSpec((None, BLK, G), lambda b, i: (b, i, 0)),
        out_shape=jax.ShapeDtypeStruct((B, T, G), BF16),
        scratch_shapes=[pltpu.VMEM((nb, G), F32), pltpu.VMEM((T, G), BF16), pltpu.VMEM((G, T), BF16),
                        pltpu.VMEM((MOBA_HEADS, nb, BLK), F32), pltpu.VMEM((BLK, G), F32)],
        compiler_params=_params(2),
        name="moba_attn",
    )(q3, k3, cols3, ow)


PAGES_PER_BLOCK = MOBA_BLOCK // PAGE_SIZE
KMEAN_BLOCKS = 8


def _kmean_kernel(pt_ref, *refs):
    page_refs, o_ref = refs[:-1], refs[-1]
    for j in range(KMEAN_BLOCKS):
        s = jnp.sum(page_refs[PAGES_PER_BLOCK * j][...], axis=0)
        for o in range(1, PAGES_PER_BLOCK):
            s = s + jnp.sum(page_refs[PAGES_PER_BLOCK * j + o][...], axis=0)
        o_ref[j] = s * (1.0 / MOBA_BLOCK)


def _kmean_pages(cache_k, page_table_flat, layer, batch, n_pages):
    nbk = n_pages // PAGES_PER_BLOCK
    per_step = KMEAN_BLOCKS * PAGES_PER_BLOCK

    def page(o):
        return pl.BlockSpec((None, None, PAGE_SIZE, MOBA_HEADS, MOBA_HEAD_DIM),
                            lambda b, j, pt: (layer, pt[b * n_pages + j * per_step + o], 0, 0, 0))

    return pl.pallas_call(
        _kmean_kernel,
        grid_spec=pltpu.PrefetchScalarGridSpec(
            num_scalar_prefetch=1,
            grid=(batch, nbk // KMEAN_BLOCKS),
            in_specs=[page(o) for o in range(per_step)],
            out_specs=pl.BlockSpec((None, KMEAN_BLOCKS, MOBA_HEADS, MOBA_HEAD_DIM), lambda b, j, pt: (b, j, 0, 0)),
        ),
        out_shape=jax.ShapeDtypeStruct((batch, nbk, MOBA_HEADS, MOBA_HEAD_DIM), F32),
        compiler_params=_params(2),
        name="moba_kmean_pages",
    )(page_table_flat, *([cache_k] * per_step))


def _select_kernel(q_ref, km_ref, idx_ref, *, batch, t_new, nbk):
    rows = batch * t_new
    lane_i = _iota((rows, LANE), 1)
    lane_f = lane_i.astype(F32)
    row_b = _iota((rows, LANE), 0) // t_new
    out = jnp.zeros((rows, LANE), F32)
    pad = jnp.zeros((LANE - nbk, LANE), F32)
    for h in range(MOBA_HEADS):
        sl = slice(h * LANE, (h + 1) * LANE)
        q = q_ref[:, sl]
        gate = jnp.zeros((rows, LANE), F32)
        for b in range(batch):
            km = jnp.concatenate([km_ref[b, :, h, :], pad], axis=0)
            gate = jnp.where(row_b == b, _dot_t(q, km, HI), gate)
        gate = jnp.where(lane_i < nbk, gate, NEG)
        _, idxs = _top3(gate, lane_f)
        for kth, idx in enumerate(idxs):
            out = jnp.where(lane_i == h * MOBA_HEADS + kth, idx, out)
    idx_ref[...] = out.astype(jnp.int32)


def _select(q_rot, kmean, batch, t_new):
    rows = batch * t_new
    nbk = kmean.shape[1]
    G = GROUP_WIDTH
    return pl.pallas_call(
        functools.partial(_select_kernel, batch=batch, t_new=t_new, nbk=nbk),
        grid=(1,),
        in_specs=[
            pl.BlockSpec((rows, G), lambda i: (0, 0)),
            pl.BlockSpec((batch, nbk, MOBA_HEADS, MOBA_HEAD_DIM), lambda i: (0, 0, 0, 0)),
        ],
        out_specs=pl.BlockSpec((rows, LANE), lambda i: (0, 0)),
        out_shape=jax.ShapeDtypeStruct((rows, LANE), jnp.int32),
        compiler_params=_params(1),
        name="moba_select",
    )(q_rot, kmean)


def _sample_attn_kernel(pt_ref, idx_ref, q_ref, kn_ref, vn_ref, ow_ref, ck_ref, cv_ref, y_ref,
                        kbuf, vbuf, sems, o_ref, *, layer, t_new, n_pages):
    b = pl.program_id(0)
    rows = kn_ref.shape[0]
    scale = MOBA_HEAD_DIM ** -0.5
    per_query = PAGES_PER_BLOCK * MOBA_TOPK
    queries = [(t, h) for t in range(t_new) for h in range(MOBA_HEADS)]

    def page_copies(qn):
        t, h = queries[qn]
        out = []
        for kth in range(MOBA_TOPK):
            blk = idx_ref[(b * t_new + t) * LANE + h * MOBA_HEADS + kth]
            for o in range(PAGES_PER_BLOCK):
                page = pt_ref[b * n_pages + PAGES_PER_BLOCK * blk + o]
                slot = qn * per_query + kth * PAGES_PER_BLOCK + o
                out.append(pltpu.make_async_copy(ck_ref.at[layer, page, :, h, :], kbuf.at[slot], sems.at[0, qn]))
                out.append(pltpu.make_async_copy(cv_ref.at[layer, page, :, h, :], vbuf.at[slot], sems.at[1, qn]))
        return out

    copies = [page_copies(qn) for qn in range(len(queries))]
    for group in copies:
        for cp in group:
            cp.start()

    r_i = _iota((8, rows), 1)
    for qn, (t, h) in enumerate(queries):
        sl = slice(h * LANE, (h + 1) * LANE)
        q8 = jnp.broadcast_to(q_ref[t:t + 1, sl], (8, LANE)).astype(BF16)
        s_new = _dot_t(q8, kn_ref[:, sl].astype(BF16)) * scale
        s_new = jnp.where((r_i >= b * t_new) & (r_i <= b * t_new + t), s_new, NEG)
        for cp in copies[qn]:
            cp.wait()
        slots = range(qn * per_query, (qn + 1) * per_query)
        s_old = [_dot_t(q8, kbuf[slot].astype(BF16)) * scale for slot in slots]
        m = jnp.max(s_new, axis=-1, keepdims=True)
        for s in s_old:
            m = jnp.maximum(m, jnp.max(s, axis=-1, keepdims=True))
        p_new = jnp.exp(s_new - m)
        l = jnp.sum(p_new, axis=-1, keepdims=True)
        acc = _dot(p_new.astype(BF16), vn_ref[:, sl].astype(BF16))
        for s, slot in zip(s_old, slots):
            p = jnp.exp(s - m)
            l = l + jnp.sum(p, axis=-1, keepdims=True)
            acc = acc + _dot(p.astype(BF16), vbuf[slot].astype(BF16))
        o_ref[t:t + 1, sl] = (acc / l)[0:1, :]

    y_ref[...] = _rms(o_ref[...], ow_ref[...]).astype(y_ref.dtype)


def _sample_attn(q_rot, k_new, cols, ow, cache_k, cache_v, page_table_flat, idx_flat, layer, batch, t_new, n_pages):
    rows = batch * t_new
    G = GROUP_WIDTH
    n_slots = t_new * MOBA_HEADS * MOBA_TOPK * PAGES_PER_BLOCK
    return pl.pallas_call(
        functools.partial(_sample_attn_kernel, layer=layer, t_new=t_new, n_pages=n_pages),
        grid_spec=pltpu.PrefetchScalarGridSpec(
            num_scalar_prefetch=2,
            grid=(batch,),
            in_specs=[
                pl.BlockSpec((None, t_new, G), lambda b, pt, idx: (b, 0, 0)),
                pl.BlockSpec((rows, G), lambda b, pt, idx: (0, 0)),
                pl.BlockSpec((rows, G), lambda b, pt, idx: (0, C_MV // G)),
                pl.BlockSpec((1, G), lambda b, pt, idx: (0, 0)),
                pl.BlockSpec(memory_space=pl.ANY),
                pl.BlockSpec(memory_space=pl.ANY),
            ],
            out_specs=pl.BlockSpec((None, t_new, G), lambda b, pt, idx: (b, 0, 0)),
            scratch_shapes=[
                pltpu.VMEM((n_slots, PAGE_SIZE, MOBA_HEAD_DIM), F32),
                pltpu.VMEM((n_slots, PAGE_SIZE, MOBA_HEAD_DIM), F32),
                pltpu.SemaphoreType.DMA((2, t_new * MOBA_HEADS)),
                pltpu.VMEM((t_new, G), F32),
            ],
        ),
        out_shape=jax.ShapeDtypeStruct((batch, t_new, G), BF16),
        compiler_params=_params(1),
        name="moba_sample_attn",
    )(page_table_flat, idx_flat, q_rot.reshape(batch, t_new, G), k_new, cols, ow, cache_k, cache_v)


def _rope_tables(pos):
    half = MOBA_HEAD_DIM // 2
    freq = ROPE_THETA ** (-jnp.arange(half, dtype=F32) / half)
    ang = pos.astype(F32)[:, None] * freq[None, :]
    cos, sin = jnp.cos(ang), jnp.sin(ang)
    return jnp.concatenate([cos, cos], -1), jnp.concatenate([-sin, sin], -1)


def _pad_rows_front(x, rows):
    return jnp.pad(x, ((0, 0), (rows - x.shape[1], 0), (0, 0)))


def _pair_block_diag(s):
    B, H, N, _ = s.shape
    s = s.reshape(B, H // 2, 2, N, 1, N) * jnp.eye(2, dtype=s.dtype)[None, None, :, None, :, None]
    return s.reshape(B, H // 2, 2 * N, 2 * N)


def _pair_diag_blocks(s_bd):
    B, n_pairs, P, _ = s_bd.shape
    N = P // 2
    s = s_bd.reshape(B, n_pairs, 2, N, 2, N)
    return jnp.stack([s[:, :, 0, :, 0, :], s[:, :, 1, :, 1, :]], axis=2).reshape(B, 2 * n_pairs, N, N)


def _layer_weights(W, i):
    G = GROUP_WIDTH
    row = lambda v: v.reshape(1, -1)
    lane_pad = lambda v: jnp.pad(v, (0, LANE - v.shape[0])).reshape(1, LANE)
    zeros_r = jnp.zeros((RWKV_HEAD, G), F32)
    return dict(
        cw8=jnp.pad(W['ssd_conv_w'][i].T, ((0, 8 - SSD_CONV), (0, 0))),
        cb=row(W['ssd_conv_b'][i]),
        dtb=lane_pad(W['ssd_dt_bias'][i]),
        alog=lane_pad(W['ssd_a_log'][i]),
        dskip=row(jnp.repeat(W['ssd_d'][i], GROUP_WIDTH // SSD_HEADS)),
        ssd_nw=row(W['ssd_norm_w'][i]),
        qw=row(W['moba_q_norm_w'][i]), kw=row(W['moba_k_norm_w'][i]), ow=row(W['moba_out_norm_w'][i]),
        gvw=row(W['gmlp_v_norm_w'][i]), gow=row(W['gmlp_out_norm_w'][i]),
        ws=W['gmlp_w_s'][i], bs=W['gmlp_b_s'][i][:, :, None],
        mu=row(W['rwkv_mu'][i]), w0=row(W['rwkv_w0'][i]), a0=row(W['rwkv_a0'][i]),
        wup=jnp.concatenate([W['rwkv_w_up'][i], zeros_r], 0),
        aup=jnp.concatenate([zeros_r, W['rwkv_a_up'][i]], 0),
        gup=W['rwkv_g_up'][i],
        kk=row(W['rwkv_k_k'][i]), ka=row(W['rwkv_k_a'][i]), rk=row(W['rwkv_r_k'][i]),
        lnw=row(W['rwkv_ln_w'][i]), lnb=row(W['rwkv_ln_b'][i]),
    )


def _mixers(cols, B, T, Tp, lw, conv_prev, ssd_prev, rwkv_prev, shift_prev, t_valid):
    G = GROUP_WIDTH
    cols3 = cols.reshape(B, T, COLS)
    if Tp != T:
        cols3 = jnp.pad(cols3, ((0, 0), (0, Tp - T), (0, 0)))
    y_ssd, ssd_new = _ssd(cols3, _pad_rows_front(conv_prev, 8), ssd_prev.reshape(B, SSD_HEADS * SSD_HEADDIM, SSD_STATE),
                          lw['cw8'], lw['cb'], lw['dtb'], lw['alog'], lw['dskip'], lw['ssd_nw'], t_valid)
    y_gm, v_gm = _gmlp(cols3, lw['gvw'], lw['ws'], lw['bs'], lw['gow'])
    y_rw, rwkv_new = _rwkv(cols3, _pad_rows_front(shift_prev[:, None, :], 8), _pair_block_diag(rwkv_prev),
                           lw['mu'], lw['w0'], lw['wup'], lw['a0'], lw['aup'], lw['gup'], lw['kk'], lw['ka'],
                           lw['rk'], lw['lnw'], lw['lnb'], t_valid)
    crop = lambda y: y[:, :T].reshape(B * T, G)
    raw = cols.reshape(B, T, COLS)
    conv_new = raw[:, T - (SSD_CONV - 1):, C_X:C_X + SSD_CONV_DIM]
    shift_new = raw[:, T - 1, C_RR:C_RR + RWKV_COLS]
    states = (ssd_new.reshape(B, SSD_HEADS, SSD_HEADDIM, SSD_STATE), conv_new, _pair_diag_blocks(rwkv_new), shift_new)
    return crop(y_ssd), crop(y_gm), crop(y_rw), v_gm[:, :T], states


def kernel(x_prompt, x_sample, c_prompt, c_sample, cache_k, cache_v, page_table, state_ssd, state_ssd_conv, state_rwkv, state_rwkv_shift, norm_w, w_ada, b_ada, ffn_w1, ffn_w3, ffn_w2, w_in, w_out, ssd_conv_w, ssd_conv_b, ssd_dt_bias, ssd_a_log, ssd_d, ssd_norm_w, moba_q_norm_w, moba_k_norm_w, moba_out_norm_w, gmlp_v_norm_w, gmlp_w_s, gmlp_b_s, gmlp_out_norm_w, rwkv_mu, rwkv_w0, rwkv_w_up, rwkv_a0, rwkv_a_up, rwkv_g_up, rwkv_k_k, rwkv_k_a, rwkv_r_k, rwkv_ln_w, rwkv_ln_b):
    W = dict(ssd_conv_w=ssd_conv_w, ssd_conv_b=ssd_conv_b, ssd_dt_bias=ssd_dt_bias, ssd_a_log=ssd_a_log,
             ssd_d=ssd_d, ssd_norm_w=ssd_norm_w, moba_q_norm_w=moba_q_norm_w, moba_k_norm_w=moba_k_norm_w,
             moba_out_norm_w=moba_out_norm_w, gmlp_v_norm_w=gmlp_v_norm_w, gmlp_w_s=gmlp_w_s, gmlp_b_s=gmlp_b_s,
             gmlp_out_norm_w=gmlp_out_norm_w, rwkv_mu=rwkv_mu, rwkv_w0=rwkv_w0, rwkv_w_up=rwkv_w_up,
             rwkv_a0=rwkv_a0, rwkv_a_up=rwkv_a_up, rwkv_g_up=rwkv_g_up, rwkv_k_k=rwkv_k_k, rwkv_k_a=rwkv_k_a,
             rwkv_r_k=rwkv_r_k, rwkv_ln_w=rwkv_ln_w, rwkv_ln_b=rwkv_ln_b)
    Bp, Tq, D = x_prompt.shape
    Bs, Ts, _ = x_sample.shape
    n_pages = page_table.shape[1]
    assert n_pages * PAGE_SIZE == PAST_LEN and PAST_LEN % MOBA_BLOCK == 0
    assert PAST_LEN // MOBA_BLOCK >= MOBA_TOPK and Ts <= MOBA_BLOCK
    assert n_pages % (KMEAN_BLOCKS * PAGES_PER_BLOCK) == 0 and RWKV_CHUNK == RWKV_HEAD
    Rp, Rs = Bp * Tq, Bs * Ts

    assert w_in.shape == (DEPTH, D, IN_COLS)
    w_in_b = _win_prep(jnp.swapaxes(w_in, 1, 2))
    w_out_b = w_out.astype(BF16)
    nw4 = norm_w.reshape(DEPTH, 3, 1, D)
    b_ada3 = b_ada.reshape(DEPTH, 1, N_MOD * D)

    n_c = Bp + Bs
    c_all = jnp.pad(jnp.concatenate([c_prompt, c_sample], 0), ((0, -n_c % 8), (0, 0)))

    pos_p = jnp.arange(Tq, dtype=jnp.int32)
    pos_s = PAST_LEN + jnp.arange(Ts, dtype=jnp.int32)
    cos_p, sin_p = (jnp.tile(t, (Bp, 1)) for t in _rope_tables(pos_p))
    cos_s, sin_s = (jnp.tile(t, (Bs, 1)) for t in _rope_tables(pos_s))

    pt_flat = page_table.reshape(-1)

    zeros = lambda *s: jnp.zeros(s, F32)
    xp = x_prompt.reshape(Rp, D)
    xs = x_sample.reshape(Rs, D)
    TM = 512
    outs_p, outs_s = [], []
    for i in range(DEPTH):
        lw = _layer_weights(W, i)
        mod = _ada(c_all, w_ada, b_ada3, i).reshape(-1, N_MOD, D)
        mod_p = _Mod(mod[:Bp].reshape(Bp, N_MOD, 1, D), False, Tq, TM)
        mod_s = _Mod(jnp.repeat(mod[Bp:n_c], Ts, axis=0).transpose(1, 0, 2), True, Ts, Rs)

        xs, *ffn_a = _ffn(xs, mod_s, 0, nw4, ffn_w1, ffn_w3, ffn_w2, i, 0, Rs, 512)
        cols = _inproj(xs, mod_s, nw4, w_in_b, i, Rs, 1536)
        q_rot, k_rot, k4, v4 = _moba_prep(cols, cos_s, sin_s, lw['qw'], lw['kw'], Rs)
        y_ssd, y_gm, y_rw, v_gm, st = _mixers(cols, Bs, Ts, SAMPLE_PAD, lw, state_ssd_conv[i], state_ssd[i],
                                              state_rwkv[i], state_rwkv_shift[i], Ts)
        kmean = _kmean_pages(cache_k, pt_flat, i, Bs, n_pages)
        idx = _select(q_rot, kmean, Bs, Ts)
        y_att = _sample_attn(q_rot, k_rot, cols, lw['ow'], cache_k, cache_v, pt_flat, idx.reshape(-1), i, Bs, Ts,
                             n_pages).reshape(Rs, GROUP_WIDTH)
        xs = _outproj(xs, mod_s, (y_ssd, y_att, y_gm, y_rw), w_out_b, i, Rs)
        xs, *ffn_b = _ffn(xs, mod_s, 6, nw4, ffn_w1, ffn_w3, ffn_w2, i, 1, Rs, 512)
        shp = (Bs, Ts, MOBA_HEADS, MOBA_HEAD_DIM)
        outs_s.append((k4.reshape(shp), v4.reshape(shp)) + st + (v_gm,))

        xp = _ffn(xp, mod_p, 0, nw4, *ffn_a, i, 0, TM, 512)
        cols = _inproj(xp, mod_p, nw4, w_in_b, i, TM, 3072)
        q_rot, k_rot, k4, v4 = _moba_prep(cols, cos_p, sin_p, lw['qw'], lw['kw'], TM)
        y_ssd, y_gm, y_rw, _, st = _mixers(cols, Bp, Tq, Tq, lw, zeros(Bp, SSD_CONV - 1, SSD_CONV_DIM),
                                           zeros(Bp, SSD_HEADS, SSD_HEADDIM, SSD_STATE),
                                           zeros(Bp, RWKV_HEADS, RWKV_HEAD, RWKV_HEAD), zeros(Bp, RWKV_COLS), None)
        y_att = _moba_attn(q_rot.reshape(Bp, Tq, -1), k_rot.reshape(Bp, Tq, -1), cols.reshape(Bp, Tq, COLS),
                           lw['ow']).reshape(Rp, -1)
        xp = _outproj(xp, mod_p, (y_ssd, y_att, y_gm, y_rw), w_out_b, i, TM)
        xp = _ffn(xp, mod_p, 6, nw4, *ffn_b, i, 1, TM, 512)
        shp = (Bp, Tq, MOBA_HEADS, MOBA_HEAD_DIM)
        outs_p.append((k4.reshape(shp), v4.reshape(shp)) + st)

    k_p, v_p, ssd_p, conv_p, rwkv_p, shift_p = (jnp.stack(s) for s in zip(*outs_p))
    k_s, v_s, ssd_s, conv_s, rwkv_s, shift_s, gmlp_v_s = (jnp.stack(s) for s in zip(*outs_s))
    return (xp.reshape(Bp, Tq, D), xs.reshape(Bs, Ts, D), k_p, v_p, k_s, v_s, ssd_p, ssd_s, conv_p, conv_s,
            rwkv_p, rwkv_s, shift_p, shift_s, gmlp_v_s)
```

```python
import functools
import math

import jax
import jax.numpy as jnp
from jax import lax
from jax.experimental import pallas as pl
from jax.experimental.pallas import tpu as pltpu

F32 = jnp.float32
BF16 = jnp.bfloat16
HI = lax.Precision.HIGHEST

D_MODEL = 2048
DEPTH = 2
PAST_LEN = 16384
PAGE_SIZE = 128
GROUP_WIDTH = 512
SSD_HEADS = 8
SSD_HEADDIM = 64
SSD_STATE = 128
SSD_CONV = 4
SSD_CHUNK = 128
SSD_CONV_DIM = 1024
MOBA_HEADS = 4
MOBA_HEAD_DIM = 128
MOBA_BLOCK = 256
MOBA_TOPK = 3
ROPE_THETA = 10000.0
GMLP_CHUNK = 128
GMLP_HEADS = 4
GMLP_CHUNKS_PER_STEP = 4
RWKV_HEADS = 8
RWKV_HEAD = 64
RWKV_COLS = 1792
RWKV_CHUNK = 64
RWKV_SUB = 16
RWKV_SEQS_PER_STEP = 4
RWKV_DECAY_SCALE = 0.606531
RWKV_LN_EPS = 64e-5
D_FF = 5632
N_MOD = 9
FFN_RES = 0.5
EPS = 1e-6
NEG = -1e30

COLS = 6144
IN_COLS = 5896
C_Z, C_X, C_BC = 0, 512, 1024
C_MQ, C_MK, C_MV = 1536, 2048, 2560
C_GU, C_GV = 3072, 3584
C_RR, C_RK, C_RV, C_RL = 4096, 4608, 5120, 5632
C_DT = 5888

LANE = 128
SAMPLE_PAD = 128
VMEM_LIMIT = 56 * 1024 * 1024
ROW_TILE = 512
FFN_TILE = 512
INPROJ_TILE = 3072


def _params(n_axes, vmem=VMEM_LIMIT):
    return pltpu.CompilerParams(dimension_semantics=("arbitrary",) * n_axes, vmem_limit_bytes=vmem)


def _dot(a, b, precision=None):
    return jnp.dot(a, b, preferred_element_type=F32, precision=precision)


def _dot_t(a, b, precision=None):
    return lax.dot_general(a, b, (((1,), (1,)), ((), ())), preferred_element_type=F32, precision=precision)


def _iota(shape, dim):
    return lax.broadcasted_iota(jnp.int32, shape, dim)


def _rms(x, w):
    return x * lax.rsqrt(jnp.mean(x * x, -1, keepdims=True) + EPS) * w


def _silu(x):
    return x * jax.nn.sigmoid(x)


def _ada_kernel(c_ref, w_ref, b_ref, o_ref):
    s = _silu(c_ref[...]).astype(BF16)
    o_ref[...] = _dot(s, w_ref[...].astype(BF16)) + b_ref[...]


def _ada(c_all, w_ada, b_ada3, layer):
    rows = c_all.shape[0]
    n_out = w_ada.shape[-1]
    tn = 1024
    return pl.pallas_call(
        _ada_kernel,
        grid=(n_out // tn,),
        in_specs=[
            pl.BlockSpec((rows, D_MODEL), lambda n: (0, 0)),
            pl.BlockSpec((None, D_MODEL, tn), lambda n: (layer, 0, n)),
            pl.BlockSpec((None, 1, tn), lambda n: (layer, 0, n)),
        ],
        out_specs=pl.BlockSpec((rows, tn), lambda n: (0, n)),
        out_shape=jax.ShapeDtypeStruct((rows, n_out), F32),
        compiler_params=_params(1),
        name="ada",
    )(c_all, w_ada, b_ada3)


class _Mod:
    def __init__(self, arr, per_row, rows_per_batch, tm):
        self.arr = arr
        self.per_row = per_row
        self.tiles_per_batch = None if per_row else rows_per_batch // tm
        self.tm = tm

    def spec(self, j):
        if self.per_row:
            return pl.BlockSpec((None, self.tm, D_MODEL), lambda r, *_: (j, r, 0))
        tpb = self.tiles_per_batch
        return pl.BlockSpec((None, None, 1, D_MODEL), lambda r, *_: (r // tpb, j, 0, 0))


def _ffn_kernel(x_ref, nw_ref, sh_ref, sc_ref, g_ref, w1_ref, w3_ref, w2_ref, o_ref, *rest, nf, emit_bf16):
    f = pl.program_id(1)
    h_ref, acc_ref = rest[-2:]

    @pl.when(f == 0)
    def _():
        xn = _rms(x_ref[...], nw_ref[...])
        h_ref[...] = (xn * (1 + sc_ref[...]) + sh_ref[...]).astype(BF16)
        acc_ref[...] = jnp.zeros_like(acc_ref)

    w1, w3, w2 = (w[...].astype(BF16) for w in (w1_ref, w3_ref, w2_ref))
    if emit_bf16:
        for dst, w in zip(rest[:3], (w1, w3, w2)):
            dst[...] = w
    h = h_ref[...]
    a = _dot(h, w1)
    b = _dot(h, w3)
    acc_ref[...] += _dot((_silu(a) * b).astype(BF16), w2)

    @pl.when(f == nf - 1)
    def _():
        o_ref[...] = x_ref[...] + FFN_RES * g_ref[...] * acc_ref[...]


def _ffn(x, mod, j0, nw4, w1, w3, w2, layer, slot, tm, tf):
    rows = x.shape[0]
    nf = D_FF // tf
    stacked = w1.ndim == 4
    if stacked:
        up = pl.BlockSpec((None, None, D_MODEL, tf), lambda r, f: (layer, slot, 0, f))
        down = pl.BlockSpec((None, None, tf, D_MODEL), lambda r, f: (layer, slot, f, 0))
    else:
        up = pl.BlockSpec((D_MODEL, tf), lambda r, f: (0, f))
        down = pl.BlockSpec((tf, D_MODEL), lambda r, f: (f, 0))
    out_specs = [pl.BlockSpec((tm, D_MODEL), lambda r, f: (r, 0))]
    out_shape = [jax.ShapeDtypeStruct((rows, D_MODEL), F32)]
    if stacked:
        assert rows == tm
        out_specs += [pl.BlockSpec((D_MODEL, tf), lambda r, f: (0, f))] * 2 + [pl.BlockSpec((tf, D_MODEL), lambda r, f: (f, 0))]
        out_shape += [jax.ShapeDtypeStruct((D_MODEL, D_FF), BF16)] * 2 + [jax.ShapeDtypeStruct((D_FF, D_MODEL), BF16)]
    outs = pl.pallas_call(
        functools.partial(_ffn_kernel, nf=nf, emit_bf16=stacked),
        grid=(rows // tm, nf),
        in_specs=[
            pl.BlockSpec((tm, D_MODEL), lambda r, f: (r, 0)),
            pl.BlockSpec((None, None, 1, D_MODEL), lambda r, f: (layer, 2 * slot, 0, 0)),
            mod.spec(j0), mod.spec(j0 + 1), mod.spec(j0 + 2),
            up, up, down,
        ],
        out_specs=out_specs,
        out_shape=out_shape,
        scratch_shapes=[pltpu.VMEM((tm, D_MODEL), BF16), pltpu.VMEM((tm, D_MODEL), F32)],
        compiler_params=_params(2),
        name="ffn",
    )(x, nw4, mod.arr, mod.arr, mod.arr, w1, w3, w2)
    return outs if stacked else outs[0]


def _inproj_kernel(x_ref, nw_ref, sh_ref, sc_ref, w_ref, o_ref, h_ref):
    @pl.when(pl.program_id(1) == 0)
    def _():
        xn = _rms(x_ref[...], nw_ref[...])
        h_ref[...] = (xn * (1 + sc_ref[...]) + sh_ref[...]).astype(BF16)

    o_ref[...] = _dot_t(h_ref[...], w_ref[...])


def _inproj(x, mod, nw4, w_in_t, layer, tm, tn):
    rows = x.shape[0]
    return pl.pallas_call(
        _inproj_kernel,
        grid=(rows // tm, COLS // tn),
        in_specs=[
            pl.BlockSpec((tm, D_MODEL), lambda r, n: (r, 0)),
            pl.BlockSpec((None, None, 1, D_MODEL), lambda r, n: (layer, 1, 0, 0)),
            mod.spec(3), mod.spec(4),
            pl.BlockSpec((None, tn, D_MODEL), lambda r, n: (layer, n, 0)),
        ],
        out_specs=pl.BlockSpec((tm, tn), lambda r, n: (r, n)),
        out_shape=jax.ShapeDtypeStruct((rows, COLS), F32),
        scratch_shapes=[pltpu.VMEM((tm, D_MODEL), BF16)],
        compiler_params=_params(2),
        name="inproj",
    )(x, nw4, mod.arr, mod.arr, w_in_t)


WIN_TILE = 512


def _win_prep_kernel(w_ref, dt_ref, o_ref, *, n_tiles, tail):
    j = pl.program_id(1)

    @pl.when(j < n_tiles - 1)
    def _():
        o_ref[...] = w_ref[0].astype(BF16)

    @pl.when(j == n_tiles - 1)
    def _():
        o_ref[0:tail, :] = w_ref[0, WIN_TILE - tail:WIN_TILE, :].astype(BF16)
        rest = jnp.concatenate([dt_ref[0], jnp.zeros((WIN_TILE - tail - SSD_HEADS, D_MODEL), F32)], axis=0)
        o_ref[tail:WIN_TILE, :] = rest.astype(BF16)


def _win_prep(w_in_t):
    head = C_BC + GROUP_WIDTH
    n_tiles = COLS // WIN_TILE
    tail = C_DT % WIN_TILE
    assert head % WIN_TILE == 0 and SSD_HEADS == 8 and 0 < tail < WIN_TILE

    def src_row(j):
        shifted = jnp.minimum(j * WIN_TILE + SSD_HEADS, IN_COLS - WIN_TILE)
        return pl.multiple_of(jnp.where(j * WIN_TILE < head, j * WIN_TILE, shifted), SSD_HEADS)

    return pl.pallas_call(
        functools.partial(_win_prep_kernel, n_tiles=n_tiles, tail=tail),
        grid=(DEPTH, n_tiles),
        in_specs=[
            pl.BlockSpec((pl.Element(1), pl.Element(WIN_TILE), pl.Element(D_MODEL)), lambda i, j: (i, src_row(j), 0)),
            pl.BlockSpec((pl.Element(1), pl.Element(SSD_HEADS), pl.Element(D_MODEL)), lambda i, j: (i, head, 0)),
        ],
        out_specs=pl.BlockSpec((None, WIN_TILE, D_MODEL), lambda i, j: (i, j, 0)),
        out_shape=jax.ShapeDtypeStruct((DEPTH, COLS, D_MODEL), BF16),
        compiler_params=_params(2),
        name="w_in_prep",
    )(w_in_t, w_in_t)


def _outproj_kernel(x_ref, g_ref, y0_ref, y1_ref, y2_ref, y3_ref, w_ref, o_ref):
    G = GROUP_WIDTH
    acc = _dot(y0_ref[...], w_ref[0:G, :])
    acc += _dot(y1_ref[...], w_ref[G:2 * G, :])
    acc += _dot(y2_ref[...], w_ref[2 * G:3 * G, :])
    acc += _dot(y3_ref[...], w_ref[3 * G:4 * G, :])
    o_ref[...] = x_ref[...] + g_ref[...] * acc


def _outproj(x, mod, ys, w_out, layer, tm):
    rows = x.shape[0]
    yspec = pl.BlockSpec((tm, GROUP_WIDTH), lambda r: (r, 0))
    return pl.pallas_call(
        _outproj_kernel,
        grid=(rows // tm,),
        in_specs=[
            pl.BlockSpec((tm, D_MODEL), lambda r: (r, 0)),
            mod.spec(5), yspec, yspec, yspec, yspec,
            pl.BlockSpec((None, D_MODEL, D_MODEL), lambda r: (layer, 0, 0)),
        ],
        out_specs=pl.BlockSpec((tm, D_MODEL), lambda r: (r, 0)),
        out_shape=jax.ShapeDtypeStruct((rows, D_MODEL), F32),
        compiler_params=_params(1),
        name="outproj",
    )(x, mod.arr, *ys, w_out)


def _softplus(x):
    return jnp.maximum(x, 0.0) + jnp.log1p(jnp.exp(-jnp.abs(x)))


def _ssd_kernel(z_ref, x_ref, bc_ref, dt_ref, prev_ref, cw_ref, cb_ref, dtb_ref, alog_ref, dskip_ref, nw_ref,
                h0_ref, y_ref, hout_ref, ext_ref, st_ref, *, C, nc, t_valid):
    c = pl.program_id(1)
    G = GROUP_WIDTH

    @pl.when(c == 0)
    def _():
        ext_ref[0:8, :] = prev_ref[...]
        st_ref[...] = h0_ref[...]

    ext_ref[8:8 + C, 0:G] = x_ref[...]
    ext_ref[8:8 + C, G:2 * G] = bc_ref[...]
    conv = cb_ref[...] + ext_ref[5:5 + C, :] * cw_ref[0:1, :]
    for i in range(1, SSD_CONV):
        conv = conv + ext_ref[5 + i:5 + i + C, :] * cw_ref[i:i + 1, :]
    ext_ref[0:8, :] = ext_ref[C:C + 8, :]
    xbc = _silu(conv)
    xs = xbc[:, 0:G]

    dt = _softplus(dt_ref[...] + dtb_ref[...])
    if t_valid is not None:
        dt = jnp.where(c * C + _iota((C, LANE), 0) < t_valid, dt, 0.0)
    a = dt * (-jnp.exp(alog_ref[...]))
    tri = _iota((C, C), 0) >= _iota((C, C), 1)
    tri_b = tri.astype(BF16)
    a_hi, a_mid, a_lo = _split3(a)
    a_cs = _dg(tri_b, a_hi, _NN) + (_dg(tri_b, a_mid, _NN) + _dg(tri_b, a_lo, _NN))
    a_cs_t = a_cs.T
    lane_lo = _iota((C, LANE), 1) < SSD_HEADDIM
    row_lo = _iota((LANE, 1), 0) < SSD_HEADDIM
    col = lambda h: a_cs[:, h:h + 1]
    row = lambda h: a_cs_t[h:h + 1, :]
    last = lambda h: a_cs[C - 1:C, h:h + 1]
    by_head = lambda p, f: jnp.where(lane_lo, f(2 * p), f(2 * p + 1))
    decay_mat = lambda h: jnp.exp(jnp.where(tri, col(h) - row(h), -jnp.inf))

    n_groups = 2
    heads_per_group = SSD_HEADS // n_groups
    bm_s = [_split2(xbc[:, G + g * SSD_STATE:G + (g + 1) * SSD_STATE]) for g in range(n_groups)]
    cm_s = [_split2(xbc[:, G + (n_groups + g) * SSD_STATE:G + (n_groups + g + 1) * SSD_STATE]) for g in range(n_groups)]
    scores = [_dot3(cm_s[g], bm_s[g], _NT) for g in range(n_groups)]
    group = lambda p: 2 * p // heads_per_group
    each = lambda f: [f(p) for p in range(SSD_HEADS // 2)]
    m0_s = each(lambda p: _split2(scores[group(p)] * decay_mat(2 * p)))
    m1_s = each(lambda p: _split2(scores[group(p)] * decay_mat(2 * p + 1)))
    xs_p = each(lambda p: xs[:, p * LANE:(p + 1) * LANE])
    xdt = each(lambda p: xs_p[p] * by_head(p, lambda h: dt[:, h:h + 1]))
    xdt_s = each(lambda p: _split2(xdt[p]))
    y_diag = each(lambda p: jnp.where(lane_lo, _dot3(m0_s[p], xdt_s[p]), _dot3(m1_s[p], xdt_s[p])))
    st = each(lambda p: st_ref[p * LANE:(p + 1) * LANE, :])
    y_off = each(lambda p: _dot3(cm_s[group(p)], _split2(st[p]), _NT) * by_head(p, lambda h: jnp.exp(col(h))))
    decay = each(lambda p: by_head(p, lambda h: jnp.exp(last(h) - col(h))))
    new = each(lambda p: _dot3(_split2(xdt[p] * decay[p]), bm_s[group(p)], _TN))
    for p in range(SSD_HEADS // 2):
        keep = jnp.where(row_lo, jnp.exp(last(2 * p)), jnp.exp(last(2 * p + 1)))
        st_ref[p * LANE:(p + 1) * LANE, :] = st[p] * keep + new[p]
    ys = each(lambda p: y_diag[p] + y_off[p] + xs_p[p] * dskip_ref[:, p * LANE:(p + 1) * LANE])

    y = jnp.concatenate(ys, axis=1) * _silu(z_ref[...])
    y_ref[...] = _rms(y, nw_ref[...]).astype(y_ref.dtype)

    @pl.when(c == nc - 1)
    def _():
        hout_ref[...] = st_ref[...]


def _ssd(cols3, prev8, h0, cw8, cb, dtb, alog, dskip, nw, t_valid):
    B, T, _ = cols3.shape
    C = SSD_CHUNK
    nc = T // C
    G = GROUP_WIDTH
    vec = lambda n: pl.BlockSpec((1, n), lambda b, c: (0, 0))
    return pl.pallas_call(
        functools.partial(_ssd_kernel, C=C, nc=nc, t_valid=t_valid),
        grid=(B, nc),
        in_specs=[
            pl.BlockSpec((None, C, G), lambda b, c: (b, c, C_Z // G)),
            pl.BlockSpec((None, C, G), lambda b, c: (b, c, C_X // G)),
            pl.BlockSpec((None, C, G), lambda b, c: (b, c, C_BC // G)),
            pl.BlockSpec((None, C, LANE), lambda b, c: (b, c, C_DT // LANE)),
            pl.BlockSpec((None, 8, SSD_CONV_DIM), lambda b, c: (b, 0, 0)),
            pl.BlockSpec((8, SSD_CONV_DIM), lambda b, c: (0, 0)),
            vec(SSD_CONV_DIM), vec(LANE), vec(LANE), vec(G), vec(G),
            pl.BlockSpec((None, SSD_HEADS * SSD_HEADDIM, SSD_STATE), lambda b, c: (b, 0, 0)),
        ],
        out_specs=[
            pl.BlockSpec((None, C, G), lambda b, c: (b, c, 0)),
            pl.BlockSpec((None, SSD_HEADS * SSD_HEADDIM, SSD_STATE), lambda b, c: (b, 0, 0)),
        ],
        out_shape=[
            jax.ShapeDtypeStruct((B, T, G), BF16),
            jax.ShapeDtypeStruct((B, SSD_HEADS * SSD_HEADDIM, SSD_STATE), F32),
        ],
        scratch_shapes=[pltpu.VMEM((C + 8, SSD_CONV_DIM), F32), pltpu.VMEM((SSD_HEADS * SSD_HEADDIM, SSD_STATE), F32)],
        compiler_params=_params(2),
        name="ssd",
    )(cols3, cols3, cols3, cols3, prev8, cw8, cb, dtb, alog, dskip, nw, h0)


def _gmlp_kernel(u_ref, v_ref, vw_ref, ws_ref, bs_ref, ow_ref, y_ref, vout_ref, *, C, n_sub):
    u = jax.nn.gelu(u_ref[...])
    v = _rms(jax.nn.gelu(v_ref[...]), vw_ref[...])
    vout_ref[...] = v
    tri = _iota((C, C), 0) >= _iota((C, C), 1)
    mixed = [[None] * GMLP_HEADS for _ in range(n_sub)]
    for h in range(GMLP_HEADS):
        ws = jnp.where(tri, ws_ref[h], 0.0)
        v_h = jnp.concatenate([v[s * C:(s + 1) * C, h * LANE:(h + 1) * LANE] for s in range(n_sub)], axis=1)
        m_h = _dot3(_split2(ws), _split2(v_h)) + bs_ref[h]
        for s in range(n_sub):
            mixed[s][h] = m_h[:, s * LANE:(s + 1) * LANE]
    y = u * jnp.concatenate([jnp.concatenate(row, axis=1) for row in mixed], axis=0)
    y_ref[...] = _rms(y, ow_ref[...]).astype(y_ref.dtype)


def _gmlp(cols3, vw, ws, bs, ow):
    B, T, _ = cols3.shape
    C = GMLP_CHUNK
    G = GROUP_WIDTH
    vec = pl.BlockSpec((1, G), lambda b, c: (0, 0))
    n_sub = math.gcd(T // C, GMLP_CHUNKS_PER_STEP)
    rows = n_sub * C
    return pl.pallas_call(
        functools.partial(_gmlp_kernel, C=C, n_sub=n_sub),
        grid=(B, T // rows),
        in_specs=[
            pl.BlockSpec((None, rows, G), lambda b, c: (b, c, C_GU // G)),
            pl.BlockSpec((None, rows, G), lambda b, c: (b, c, C_GV // G)),
            vec,
            pl.BlockSpec((GMLP_HEADS, C, C), lambda b, c: (0, 0, 0)),
            pl.BlockSpec((GMLP_HEADS, C, 1), lambda b, c: (0, 0, 0)),
            vec,
        ],
        out_specs=[pl.BlockSpec((None, rows, G), lambda b, c: (b, c, 0))] * 2,
        out_shape=[jax.ShapeDtypeStruct((B, T, G), BF16), jax.ShapeDtypeStruct((B, T, G), F32)],
        compiler_params=_params(2),
        name="gmlp",
    )(cols3, cols3, vw, ws, bs, ow)


_NN = ((1,), (0,))
_NT = ((1,), (1,))
_TN = ((0,), (0,))


def _split2(x):
    hi = x.astype(BF16)
    return hi, (x - hi.astype(F32)).astype(BF16)


def _split3(x):
    hi = x.astype(BF16)
    rest = x - hi.astype(F32)
    mid = rest.astype(BF16)
    return hi, mid, (rest - mid.astype(F32)).astype(BF16)


def _dg(a, b, dims):
    return lax.dot_general(a, b, (dims, ((), ())), preferred_element_type=F32)


def _dot3(a, b, dims=_NN):
    (ah, al), (bh, bl) = a, b
    return _dg(ah, bh, dims) + (_dg(ah, bl, dims) + _dg(al, bh, dims))


def _block_diag(pieces, same_head):
    return tuple(jnp.where(same_head, jnp.concatenate([x, x], axis=0), jnp.zeros((), x.dtype)) for x in pieces)


def _rwkv_kernel(r_ref, k_ref, v_ref, lo_ref, prev_ref, mu_ref, w0_ref, wup_ref, a0_ref, aup_ref, gup_ref,
                 kk_ref, ka_ref, rk_ref, lnw_ref, lnb_ref, s0_ref, y_ref, sout_ref, ext_ref, st_ref,
                 *, C, nc, t_valid):
    c = pl.program_id(1)
    G = GROUP_WIDTH
    N = RWKV_HEAD
    P = 2 * N

    n_batch = r_ref.shape[0]
    n_pairs = G // P
    sls = [slice(p * P, (p + 1) * P) for p in range(n_pairs)]
    same_head_n = (_iota((P, P), 0) < N) == (_iota((P, P), 1) < N)
    ones_bd = same_head_n.astype(BF16)
    tri_cc = (_iota((C, C), 0) >= _iota((C, C), 1)).astype(BF16)

    @pl.when(c == 0)
    def _():
        ext_ref[:, 0:8, :] = prev_ref[...]
        st_ref[...] = s0_ref[...]

    def head_sum(x):
        def pair_sum(p):
            hi, lo = _split2(x[:, sls[p]])
            return _dg(hi, ones_bd, _NN) + _dg(lo, ones_bd, _NN)
        return jnp.concatenate([pair_sum(p) for p in range(n_pairs)], axis=1)

    def token_shift(n):
        ext_ref[n, 8:8 + C, 0:G] = r_ref[n]
        ext_ref[n, 8:8 + C, G:2 * G] = k_ref[n]
        ext_ref[n, 8:8 + C, 2 * G:3 * G] = v_ref[n]
        ext_ref[n, 8:8 + C, 3 * G:RWKV_COLS] = lo_ref[n]
        cur = ext_ref[n, 8:8 + C, :]
        prev = ext_ref[n, 7:7 + C, :]
        xs = cur + (prev - cur) * mu_ref[...]
        ext_ref[n, 0:8, :] = ext_ref[n, C:C + 8, :]
        return xs

    shifted = [token_shift(n) for n in range(n_batch)]
    la_all = jnp.concatenate([xs[:, 3 * G:3 * G + P] for xs in shifted], axis=0)
    gl_all = jnp.concatenate([xs[:, 3 * G + P:RWKV_COLS] for xs in shifted], axis=0)
    w_log_all = -RWKV_DECAY_SCALE * jax.nn.sigmoid(
        w0_ref[...] + _dot3(_split2(jnp.tanh(la_all)), _split2(wup_ref[...])))
    a_all = jax.nn.sigmoid(a0_ref[...] + _dot3(_split2(la_all), _split2(aup_ref[...])))
    g_all = _dot3(_split2(jax.nn.sigmoid(gl_all)), _split2(gup_ref[...]))

    def chunk_inputs(n):
        xs = shifted[n]
        r, k, v = xs[:, 0:G], xs[:, G:2 * G], xs[:, 2 * G:3 * G]
        w_log, a, g = (t[n * C:(n + 1) * C] for t in (w_log_all, a_all, g_all))

        kk = k * kk_ref[...]
        kk = kk * lax.rsqrt(jnp.maximum(head_sum(kk * kk), 1e-12))
        k2 = k * (1 + (a - 1) * ka_ref[...])
        if t_valid is not None:
            ok = c * C + _iota((C, G), 0) < t_valid
            w_log = jnp.where(ok, w_log, 0.0)
            kk = jnp.where(ok, kk, 0.0)
            k2 = jnp.where(ok, k2, 0.0)
        b = kk * a

        w_hi, w_mid, w_lo = _split3(w_log)
        cl = _dg(tri_cc, w_hi, _NN) + (_dg(tri_cc, w_mid, _NN) + _dg(tri_cc, w_lo, _NN))
        cl_last = cl[C - 1:C, :]
        einv = jnp.exp(-cl)
        e_c = jnp.exp(cl_last - cl)
        return dict(r=r, v=v, k2=k2, g=g, kkp=kk * jnp.exp(cl - w_log), rp=r * jnp.exp(cl), bi=b * einv,
                    ki=k2 * einv, bt=b * e_c, kt=k2 * e_c, p_c=jnp.exp(cl_last))

    seqs = [chunk_inputs(n) for n in range(n_batch)]
    units = [(n, p) for n in range(n_batch) for p in range(n_pairs)]
    each = lambda f: [f(u) for u in range(len(units))]
    tile = lambda name: each(lambda u: seqs[units[u][0]][name][:, sls[units[u][1]]])
    kkp, rp, bi, ki, bt, kt, v_p = (tile(name) for name in ("kkp", "rp", "bi", "ki", "bt", "kt", "v"))

    t_i = _iota((C, P), 0)
    s_i = _iota((C, P), 1) % C
    strict = s_i < t_i
    incl = s_i <= t_i
    diag_blk = (s_i // RWKV_SUB) == (t_i // RWKV_SUB)
    eye = (s_i == t_i).astype(F32)
    same_head = (_iota((P, P), 0) < C) == (_iota((P, P), 1) < C)

    memo = {}

    def sp(x):
        if id(x) not in memo:
            memo[id(x)] = (x, _split2(x))
        return memo[id(x)][1]

    def bd(x):
        if ("bd", id(x)) not in memo:
            memo["bd", id(x)] = (x, _block_diag(sp(x), same_head))
        return memo["bd", id(x)][1]

    def mm(xs, ys):
        return each(lambda p: _dot3(sp(xs[p]), bd(ys[p])))

    kkp_s = each(lambda p: _split2(kkp[p]))
    rp_s = each(lambda p: _split2(rp[p]))
    lhs_s = each(lambda p: tuple(jnp.concatenate([x, y], axis=0) for x, y in zip(kkp_s[p], rp_s[p])))
    ab = each(lambda p: _dot3(lhs_s[p], _block_diag(_split2(bi[p]), same_head_n), _NT))
    ak = each(lambda p: _dot3(lhs_s[p], _block_diag(_split2(ki[p]), same_head_n), _NT))
    a_m = each(lambda p: jnp.where(strict, ab[p][0:C], 0.0))
    b_k = each(lambda p: jnp.where(strict, ak[p][0:C], 0.0))
    r_b = each(lambda p: jnp.where(incl, ab[p][C:2 * C], 0.0))
    r_k = each(lambda p: jnp.where(incl, ak[p][C:2 * C], 0.0))

    n_pow = each(lambda p: jnp.where(diag_blk, -a_m[p], 0.0))
    a_o = each(lambda p: jnp.where(diag_blk, 0.0, a_m[p]))
    t_d = each(lambda p: eye + n_pow[p])
    for _ in range(int(math.log2(RWKV_SUB)) - 1):
        n_pow = mm(n_pow, n_pow)
        step = mm(t_d, n_pow)
        t_d = each(lambda p: t_d[p] + step[p])
    m1 = mm(t_d, a_o)
    m2 = mm(m1, m1)
    im = each(lambda p: eye - m1[p])
    im_m2 = mm(im, m2)
    t_full = mm(each(lambda p: im[p] + im_m2[p]), t_d)

    st = each(lambda u: st_ref[units[u]])
    st_s = each(lambda p: _split2(st[p]))
    bkv = mm(b_k, v_p)
    rhs = each(lambda p: _dot3(kkp_s[p], st_s[p], _NT) + bkv[p])
    u = mm(t_full, rhs)
    rkv = mm(r_k, v_p)
    rbu = mm(r_b, u)
    ys = each(lambda p: _dot3(rp_s[p], st_s[p], _NT) + rkv[p] - rbu[p])
    upd = each(lambda p: _dot3(_split2(jnp.concatenate([v_p[p], -u[p]], axis=0)),
                               _split2(jnp.concatenate([kt[p], bt[p]], axis=0)), _TN))
    for u_i, (n, p) in enumerate(units):
        st_ref[n, p] = st[u_i] * seqs[n]["p_c"][:, sls[p]] + jnp.where(same_head_n, upd[u_i], 0.0)

    for n, seq in enumerate(seqs):
        y = jnp.concatenate(ys[n * n_pairs:(n + 1) * n_pairs], axis=1)
        mean = head_sum(y) * (1.0 / N)
        d = y - mean
        var = head_sum(d * d) * (1.0 / N)
        yn = d * lax.rsqrt(var + RWKV_LN_EPS) * lnw_ref[...] + lnb_ref[...]
        bonus = head_sum(seq["r"] * seq["k2"] * rk_ref[...]) * seq["v"]
        y_ref[n] = ((yn + bonus) * seq["g"]).astype(y_ref.dtype)

    @pl.when(c == nc - 1)
    def _():
        sout_ref[...] = st_ref[...]


def _rwkv(cols3, prev8, s0_bd, mu, w0, wup, a0, aup, gup, kk, ka, rk, lnw, lnb, t_valid):
    B, T, _ = cols3.shape
    C = RWKV_CHUNK
    nc = T // C if t_valid is None else -(-t_valid // C)
    G = GROUP_WIDTH
    P = 2 * RWKV_HEAD
    n_pairs = RWKV_HEADS // 2
    S = RWKV_SEQS_PER_STEP
    vec = lambda n: pl.BlockSpec((1, n), lambda b, c: (0, 0))
    mat = lambda m, n: pl.BlockSpec((m, n), lambda b, c: (0, 0))
    return pl.pallas_call(
        functools.partial(_rwkv_kernel, C=C, nc=nc, t_valid=t_valid),
        grid=(B // S, nc),
        in_specs=[
            pl.BlockSpec((S, C, G), lambda b, c: (b, c, C_RR // G)),
            pl.BlockSpec((S, C, G), lambda b, c: (b, c, C_RK // G)),
            pl.BlockSpec((S, C, G), lambda b, c: (b, c, C_RV // G)),
            pl.BlockSpec((S, C, 2 * P), lambda b, c: (b, c, C_RL // (2 * P))),
            pl.BlockSpec((S, 8, RWKV_COLS), lambda b, c: (b, 0, 0)),
            vec(RWKV_COLS), vec(G), mat(P, G), vec(G), mat(P, G), mat(P, G),
            vec(G), vec(G), vec(G), vec(G), vec(G),
            pl.BlockSpec((S, n_pairs, P, P), lambda b, c: (b, 0, 0, 0)),
        ],
        out_specs=[
            pl.BlockSpec((S, C, G), lambda b, c: (b, c, 0)),
            pl.BlockSpec((S, n_pairs, P, P), lambda b, c: (b, 0, 0, 0)),
        ],
        out_shape=[
            jax.ShapeDtypeStruct((B, nc * C, G), BF16),
            jax.ShapeDtypeStruct((B, n_pairs, P, P), F32),
        ],
        scratch_shapes=[pltpu.VMEM((S, C + 8, RWKV_COLS), F32), pltpu.VMEM((S, n_pairs, P, P), F32)],
        compiler_params=_params(2),
        name="rwkv",
    )(cols3, cols3, cols3, cols3, prev8, mu, w0, wup, a0, aup, gup, kk, ka, rk, lnw, lnb, s0_bd)


def _moba_prep_kernel(q_ref, k_ref, v_ref, cos_ref, sin_ref, qw_ref, kw_ref, qo_ref, ko_ref, k4_ref, v4_ref):
    cos, sin = cos_ref[...], sin_ref[...]

    def rotate(x, w_ref):
        xn = _rms(x, w_ref[...])
        return xn * cos + pltpu.roll(xn, MOBA_HEAD_DIM // 2, 1) * sin

    for h in range(MOBA_HEADS):
        sl = slice(h * LANE, (h + 1) * LANE)
        qo_ref[:, sl] = rotate(q_ref[:, sl], qw_ref)
        k_h = rotate(k_ref[:, sl], kw_ref)
        ko_ref[:, sl] = k_h
        k4_ref[:, h, :] = k_h
        v4_ref[:, h, :] = v_ref[:, sl]


def _moba_prep(cols, cos, sin, qw, kw, tm):
    rows = cols.shape[0]
    G = GROUP_WIDTH
    tab = pl.BlockSpec((tm, LANE), lambda r: (r, 0))
    vec = pl.BlockSpec((1, LANE), lambda r: (0, 0))
    flat = pl.BlockSpec((tm, G), lambda r: (r, 0))
    heads = pl.BlockSpec((tm, MOBA_HEADS, MOBA_HEAD_DIM), lambda r: (r, 0, 0))
    return pl.pallas_call(
        _moba_prep_kernel,
        grid=(rows // tm,),
        in_specs=[
            pl.BlockSpec((tm, G), lambda r: (r, C_MQ // G)),
            pl.BlockSpec((tm, G), lambda r: (r, C_MK // G)),
            pl.BlockSpec((tm, G), lambda r: (r, C_MV // G)),
            tab, tab, vec, vec,
        ],
        out_specs=[flat, flat, heads, heads],
        out_shape=[jax.ShapeDtypeStruct((rows, G), F32)] * 2
        + [jax.ShapeDtypeStruct((rows, MOBA_HEADS, MOBA_HEAD_DIM), F32)] * 2,
        compiler_params=_params(1),
        name="moba_prep",
    )(cols, cols, cols, cos, sin, qw, kw)


def _top_lanes(gate, lane_f):
    g = gate
    big = float(gate.shape[-1])
    idxs = []
    for _ in range(MOBA_TOPK):
        m = jnp.max(g, axis=-1, keepdims=True)
        idx = jnp.min(jnp.where(g == m, lane_f, big), axis=-1, keepdims=True)
        g = jnp.where(lane_f == idx, -jnp.inf, g)
        idxs.append(idx)
    return idxs


def _moba_attn_kernel(q_ref, k_ref, v_ref, ow_ref, y_ref, km_ref, kb_ref, vt_ref, sel_ref, o_ref, *, nb):
    qi = pl.program_id(1)
    BLK = MOBA_BLOCK
    scale = MOBA_HEAD_DIM ** -0.5

    @pl.when(qi == 0)
    def _():
        for j in range(nb):
            rows = slice(j * BLK, (j + 1) * BLK)
            k_blk = k_ref[rows, :]
            km_ref[j:j + 1, :] = jnp.mean(k_blk, axis=0, keepdims=True)
            kb_ref[rows, :] = k_blk.astype(BF16)
            vt_ref[:, rows] = v_ref[rows, :].T.astype(BF16)

    blk_i = _iota((nb, BLK), 0)
    blk_f = blk_i.astype(F32)
    past = blk_i < qi
    causal = _iota((BLK, BLK), 0) <= _iota((BLK, BLK), 1)
    own = pl.multiple_of(qi * BLK, BLK)

    sls = [slice(h * LANE, (h + 1) * LANE) for h in range(MOBA_HEADS)]
    each = lambda f: [f(h) for h in range(MOBA_HEADS)]
    q_t = each(lambda h: q_ref[:, sls[h]].T)
    gate = each(lambda h: jnp.where(past, _dot3(_split2(km_ref[:, sls[h]]), _split2(q_t[h])), NEG))
    sel = each(lambda h: jnp.zeros((nb, BLK), jnp.bool_))
    for _ in range(MOBA_TOPK):
        best = each(lambda h: jnp.max(gate[h], axis=0, keepdims=True))
        idx = each(lambda h: jnp.min(jnp.where(gate[h] == best[h], blk_f, float(nb)), axis=0, keepdims=True))
        sel = each(lambda h: sel[h] | (blk_f == idx[h]))
        gate = each(lambda h: jnp.where(blk_f == idx[h], -jnp.inf, gate[h]))
    for h in range(MOBA_HEADS):
        sel_ref[h] = (sel[h] & past).astype(F32)
    qb_t = each(lambda h: q_t[h].astype(BF16))

    def block_scores(start, h):
        return _dot(kb_ref[pl.ds(start, BLK), sls[h]], qb_t[h]) * scale

    s0 = each(lambda h: jnp.where(causal, block_scores(own, h), NEG))
    m0 = each(lambda h: jnp.max(s0[h], axis=0, keepdims=True))
    p0 = each(lambda h: jnp.exp(s0[h] - m0[h]))
    l0 = each(lambda h: jnp.sum(p0[h], axis=0, keepdims=True))
    acc0 = each(lambda h: _dot(vt_ref[sls[h], pl.ds(own, BLK)], p0[h].astype(BF16)))

    def body(j, carry):
        m_i, l_i, acc = carry
        start = pl.multiple_of(j * BLK, BLK)
        s = each(lambda h: jnp.where(sel_ref[h, pl.ds(j, 1), :] > 0.0, block_scores(start, h), NEG))
        m_n = each(lambda h: jnp.maximum(m_i[h], jnp.max(s[h], axis=0, keepdims=True)))
        alpha = each(lambda h: jnp.exp(m_i[h] - m_n[h]))
        p = each(lambda h: jnp.exp(s[h] - m_n[h]))
        l_n = each(lambda h: alpha[h] * l_i[h] + jnp.sum(p[h], axis=0, keepdims=True))
        acc_n = each(lambda h: alpha[h] * acc[h] + _dot(vt_ref[sls[h], pl.ds(start, BLK)], p[h].astype(BF16)))
        return tuple(m_n), tuple(l_n), tuple(acc_n)

    _, l_f, acc_f = lax.fori_loop(0, qi, body, (tuple(m0), tuple(l0), tuple(acc0)))
    for h in range(MOBA_HEADS):
        o_ref[:, sls[h]] = (acc_f[h] / l_f[h]).T

    y_ref[...] = _rms(o_ref[...], ow_ref[...]).astype(y_ref.dtype)


def _moba_attn(q3, k3, cols3, ow):
    B, T, G = q3.shape
    BLK = MOBA_BLOCK
    nb = T // BLK
    return pl.pallas_call(
        functools.partial(_moba_attn_kernel, nb=nb),
        grid=(B, nb),
        in_specs=[
            pl.BlockSpec((None, BLK, G), lambda b, i: (b, i, 0)),
            pl.BlockSpec((None, T, G), lambda b, i: (b, 0, 0)),
            pl.BlockSpec((None, T, G), lambda b, i: (b, 0, C_MV // G)),
            pl.BlockSpec((1, G), lambda b, i: (0, 0)),
        ],
        out_specs=pl.BlockSpec((None, BLK, G), lambda b, i: (b, i, 0)),
        out_shape=jax.ShapeDtypeStruct((B, T, G), BF16),
        scratch_shapes=[pltpu.VMEM((nb, G), F32), pltpu.VMEM((T, G), BF16), pltpu.VMEM((G, T), BF16),
                        pltpu.VMEM((MOBA_HEADS, nb, BLK), F32), pltpu.VMEM((BLK, G), F32)],
        compiler_params=_params(2),
        name="moba_attn",
    )(q3, k3, cols3, ow)


PAGES_PER_BLOCK = MOBA_BLOCK // PAGE_SIZE
KMEAN_BLOCKS = 16


def _kmean_kernel(pt_ref, *refs):
    page_refs, o_ref = refs[:-1], refs[-1]
    for j in range(KMEAN_BLOCKS):
        s = jnp.sum(page_refs[PAGES_PER_BLOCK * j][...], axis=0)
        for o in range(1, PAGES_PER_BLOCK):
            s = s + jnp.sum(page_refs[PAGES_PER_BLOCK * j + o][...], axis=0)
        o_ref[j] = s * (1.0 / MOBA_BLOCK)


def _kmean_pages(cache_k, page_table_flat, layer, batch, n_pages):
    nbk = n_pages // PAGES_PER_BLOCK
    per_step = KMEAN_BLOCKS * PAGES_PER_BLOCK

    def page(o):
        return pl.BlockSpec((None, None, PAGE_SIZE, MOBA_HEADS, MOBA_HEAD_DIM),
                            lambda b, j, pt: (layer, pt[b * n_pages + j * per_step + o], 0, 0, 0))

    return pl.pallas_call(
        _kmean_kernel,
        grid_spec=pltpu.PrefetchScalarGridSpec(
            num_scalar_prefetch=1,
            grid=(batch, nbk // KMEAN_BLOCKS),
            in_specs=[page(o) for o in range(per_step)],
            out_specs=pl.BlockSpec((None, KMEAN_BLOCKS, MOBA_HEADS, MOBA_HEAD_DIM), lambda b, j, pt: (b, j, 0, 0)),
        ),
        out_shape=jax.ShapeDtypeStruct((batch, nbk, MOBA_HEADS, MOBA_HEAD_DIM), F32),
        compiler_params=_params(2),
        name="moba_kmean_pages",
    )(page_table_flat, *([cache_k] * per_step))


def _select_kernel(q_ref, km_ref, idx_ref, *, batch, t_new, nbk):
    rows = batch * t_new
    lane_i = _iota((rows, LANE), 1)
    lane_f = lane_i.astype(F32)
    row_b = _iota((rows, LANE), 0) // t_new
    out = jnp.zeros((rows, LANE), F32)
    pad = jnp.zeros((LANE - nbk, LANE), F32)
    for h in range(MOBA_HEADS):
        sl = slice(h * LANE, (h + 1) * LANE)
        q = q_ref[:, sl]
        gate = jnp.zeros((rows, LANE), F32)
        for b in range(batch):
            km = jnp.concatenate([km_ref[b, :, h, :], pad], axis=0)
            gate = jnp.where(row_b == b, _dot_t(q, km, HI), gate)
        gate = jnp.where(lane_i < nbk, gate, NEG)
        for kth, idx in enumerate(_top_lanes(gate, lane_f)):
            out = jnp.where(lane_i == h * MOBA_HEADS + kth, idx, out)
    idx_ref[...] = out.astype(jnp.int32)


def _select(q_rot, kmean, batch, t_new):
    rows = batch * t_new
    nbk = kmean.shape[1]
    G = GROUP_WIDTH
    return pl.pallas_call(
        functools.partial(_select_kernel, batch=batch, t_new=t_new, nbk=nbk),
        grid=(1,),
        in_specs=[
            pl.BlockSpec((rows, G), lambda i: (0, 0)),
            pl.BlockSpec((batch, nbk, MOBA_HEADS, MOBA_HEAD_DIM), lambda i: (0, 0, 0, 0)),
        ],
        out_specs=pl.BlockSpec((rows, LANE), lambda i: (0, 0)),
        out_shape=jax.ShapeDtypeStruct((rows, LANE), jnp.int32),
        compiler_params=_params(1),
        name="moba_select",
    )(q_rot, kmean)


def _sample_attn_kernel(pt_ref, idx_ref, q_ref, kn_ref, vn_ref, ow_ref, ck_ref, cv_ref, y_ref,
                        kbuf, vbuf, sems, o_ref, *, layer, t_new, n_pages):
    b = pl.program_id(0)
    rows = kn_ref.shape[0]
    scale = MOBA_HEAD_DIM ** -0.5
    per_query = PAGES_PER_BLOCK * MOBA_TOPK
    queries = [(t, h) for t in range(t_new) for h in range(MOBA_HEADS)]

    def page_copies(qn):
        t, h = queries[qn]
        out = []
        for kth in range(MOBA_TOPK):
            blk = idx_ref[(b * t_new + t) * LANE + h * MOBA_HEADS + kth]
            for o in range(PAGES_PER_BLOCK):
                page = pt_ref[b * n_pages + PAGES_PER_BLOCK * blk + o]
                slot = qn * per_query + kth * PAGES_PER_BLOCK + o
                out.append(pltpu.make_async_copy(ck_ref.at[layer, page, :, h, :], kbuf.at[slot], sems.at[0, qn]))
                out.append(pltpu.make_async_copy(cv_ref.at[layer, page, :, h, :], vbuf.at[slot], sems.at[1, qn]))
        return out

    copies = [page_copies(qn) for qn in range(len(queries))]
    for group in copies:
        for cp in group:
            cp.start()

    r_i = _iota((8, rows), 1)
    for qn, (t, h) in enumerate(queries):
        sl = slice(h * LANE, (h + 1) * LANE)
        q8 = jnp.broadcast_to(q_ref[t:t + 1, sl], (8, LANE)).astype(BF16)
        s_new = _dot_t(q8, kn_ref[:, sl].astype(BF16)) * scale
        s_new = jnp.where((r_i >= b * t_new) & (r_i <= b * t_new + t), s_new, NEG)
        for cp in copies[qn]:
            cp.wait()
        slots = range(qn * per_query, (qn + 1) * per_query)
        s_old = [_dot_t(q8, kbuf[slot].astype(BF16)) * scale for slot in slots]
        m = jnp.max(s_new, axis=-1, keepdims=True)
        for s in s_old:
            m = jnp.maximum(m, jnp.max(s, axis=-1, keepdims=True))
        p_new = jnp.exp(s_new - m)
        l = jnp.sum(p_new, axis=-1, keepdims=True)
        acc = _dot(p_new.astype(BF16), vn_ref[:, sl].astype(BF16))
        for s, slot in zip(s_old, slots):
            p = jnp.exp(s - m)
            l = l + jnp.sum(p, axis=-1, keepdims=True)
            acc = acc + _dot(p.astype(BF16), vbuf[slot].astype(BF16))
        o_ref[t:t + 1, sl] = (acc / l)[0:1, :]

    y_ref[...] = _rms(o_ref[...], ow_ref[...]).astype(y_ref.dtype)


def _sample_attn(q_rot, k_new, cols, ow, cache_k, cache_v, page_table_flat, idx_flat, layer, batch, t_new, n_pages):
    rows = batch * t_new
    G = GROUP_WIDTH
    n_slots = t_new * MOBA_HEADS * MOBA_TOPK * PAGES_PER_BLOCK
    return pl.pallas_call(
        functools.partial(_sample_attn_kernel, layer=layer, t_new=t_new, n_pages=n_pages),
        grid_spec=pltpu.PrefetchScalarGridSpec(
            num_scalar_prefetch=2,
            grid=(batch,),
            in_specs=[
                pl.BlockSpec((None, t_new, G), lambda b, pt, idx: (b, 0, 0)),
                pl.BlockSpec((rows, G), lambda b, pt, idx: (0, 0)),
                pl.BlockSpec((rows, G), lambda b, pt, idx: (0, C_MV // G)),
                pl.BlockSpec((1, G), lambda b, pt, idx: (0, 0)),
                pl.BlockSpec(memory_space=pl.ANY),
                pl.BlockSpec(memory_space=pl.ANY),
            ],
            out_specs=pl.BlockSpec((None, t_new, G), lambda b, pt, idx: (b, 0, 0)),
            scratch_shapes=[
                pltpu.VMEM((n_slots, PAGE_SIZE, MOBA_HEAD_DIM), F32),
                pltpu.VMEM((n_slots, PAGE_SIZE, MOBA_HEAD_DIM), F32),
                pltpu.SemaphoreType.DMA((2, t_new * MOBA_HEADS)),
                pltpu.VMEM((t_new, G), F32),
            ],
        ),
        out_shape=jax.ShapeDtypeStruct((batch, t_new, G), BF16),
        compiler_params=_params(1),
        name="moba_sample_attn",
    )(page_table_flat, idx_flat, q_rot.reshape(batch, t_new, G), k_new, cols, ow, cache_k, cache_v)


def _rope_tables(pos):
    half = MOBA_HEAD_DIM // 2
    freq = ROPE_THETA ** (-jnp.arange(half, dtype=F32) / half)
    ang = pos.astype(F32)[:, None] * freq[None, :]
    cos, sin = jnp.cos(ang), jnp.sin(ang)
    return jnp.concatenate([cos, cos], -1), jnp.concatenate([-sin, sin], -1)


def _pad_rows_front(x, rows):
    return jnp.pad(x, ((0, 0), (rows - x.shape[1], 0), (0, 0)))


def _pair_block_diag(s):
    B, H, N, _ = s.shape
    s = s.reshape(B, H // 2, 2, N, 1, N) * jnp.eye(2, dtype=s.dtype)[None, None, :, None, :, None]
    return s.reshape(B, H // 2, 2 * N, 2 * N)


def _pair_diag_blocks(s_bd):
    B, n_pairs, P, _ = s_bd.shape
    N = P // 2
    s = s_bd.reshape(B, n_pairs, 2, N, 2, N)
    return jnp.stack([s[:, :, 0, :, 0, :], s[:, :, 1, :, 1, :]], axis=2).reshape(B, 2 * n_pairs, N, N)


def _layer_weights(W, i):
    G = GROUP_WIDTH
    row = lambda v: v.reshape(1, -1)
    lane_pad = lambda v: jnp.pad(v, (0, LANE - v.shape[0])).reshape(1, LANE)
    zeros_r = jnp.zeros((RWKV_HEAD, G), F32)
    return dict(
        cw8=jnp.pad(W['ssd_conv_w'][i].T, ((0, 8 - SSD_CONV), (0, 0))),
        cb=row(W['ssd_conv_b'][i]),
        dtb=lane_pad(W['ssd_dt_bias'][i]),
        alog=lane_pad(W['ssd_a_log'][i]),
        dskip=row(jnp.repeat(W['ssd_d'][i], GROUP_WIDTH // SSD_HEADS)),
        ssd_nw=row(W['ssd_norm_w'][i]),
        qw=row(W['moba_q_norm_w'][i]), kw=row(W['moba_k_norm_w'][i]), ow=row(W['moba_out_norm_w'][i]),
        gvw=row(W['gmlp_v_norm_w'][i]), gow=row(W['gmlp_out_norm_w'][i]),
        ws=W['gmlp_w_s'][i], bs=W['gmlp_b_s'][i][:, :, None],
        mu=row(W['rwkv_mu'][i]), w0=row(W['rwkv_w0'][i]), a0=row(W['rwkv_a0'][i]),
        wup=jnp.concatenate([W['rwkv_w_up'][i], zeros_r], 0),
        aup=jnp.concatenate([zeros_r, W['rwkv_a_up'][i]], 0),
        gup=W['rwkv_g_up'][i],
        kk=row(W['rwkv_k_k'][i]), ka=row(W['rwkv_k_a'][i]), rk=row(W['rwkv_r_k'][i]),
        lnw=row(W['rwkv_ln_w'][i]), lnb=row(W['rwkv_ln_b'][i]),
    )


def _mixers(cols, B, T, Tp, lw, conv_prev, ssd_prev, rwkv_prev, shift_prev, t_valid):
    G = GROUP_WIDTH
    cols3 = cols.reshape(B, T, COLS)
    if Tp != T:
        cols3 = jnp.pad(cols3, ((0, 0), (0, Tp - T), (0, 0)))
    y_ssd, ssd_new = _ssd(cols3, _pad_rows_front(conv_prev, 8), ssd_prev.reshape(B, SSD_HEADS * SSD_HEADDIM, SSD_STATE),
                          lw['cw8'], lw['cb'], lw['dtb'], lw['alog'], lw['dskip'], lw['ssd_nw'], t_valid)
    y_gm, v_gm = _gmlp(cols3, lw['gvw'], lw['ws'], lw['bs'], lw['gow'])
    y_rw, rwkv_new = _rwkv(cols3, _pad_rows_front(shift_prev[:, None, :], 8), _pair_block_diag(rwkv_prev),
                           lw['mu'], lw['w0'], lw['wup'], lw['a0'], lw['aup'], lw['gup'], lw['kk'], lw['ka'],
                           lw['rk'], lw['lnw'], lw['lnb'], t_valid)
    crop = lambda y: y[:, :T].reshape(B * T, G)
    raw = cols.reshape(B, T, COLS)
    conv_new = raw[:, T - (SSD_CONV - 1):, C_X:C_X + SSD_CONV_DIM]
    shift_new = raw[:, T - 1, C_RR:C_RR + RWKV_COLS]
    states = (ssd_new.reshape(B, SSD_HEADS, SSD_HEADDIM, SSD_STATE), conv_new, _pair_diag_blocks(rwkv_new), shift_new)
    return crop(y_ssd), crop(y_gm), crop(y_rw), v_gm[:, :T], states


def kernel(x_prompt, x_sample, c_prompt, c_sample, cache_k, cache_v, page_table, state_ssd, state_ssd_conv, state_rwkv, state_rwkv_shift, norm_w, w_ada, b_ada, ffn_w1, ffn_w3, ffn_w2, w_in, w_out, ssd_conv_w, ssd_conv_b, ssd_dt_bias, ssd_a_log, ssd_d, ssd_norm_w, moba_q_norm_w, moba_k_norm_w, moba_out_norm_w, gmlp_v_norm_w, gmlp_w_s, gmlp_b_s, gmlp_out_norm_w, rwkv_mu, rwkv_w0, rwkv_w_up, rwkv_a0, rwkv_a_up, rwkv_g_up, rwkv_k_k, rwkv_k_a, rwkv_r_k, rwkv_ln_w, rwkv_ln_b):
    W = dict(ssd_conv_w=ssd_conv_w, ssd_conv_b=ssd_conv_b, ssd_dt_bias=ssd_dt_bias, ssd_a_log=ssd_a_log,
             ssd_d=ssd_d, ssd_norm_w=ssd_norm_w, moba_q_norm_w=moba_q_norm_w, moba_k_norm_w=moba_k_norm_w,
             moba_out_norm_w=moba_out_norm_w, gmlp_v_norm_w=gmlp_v_norm_w, gmlp_w_s=gmlp_w_s, gmlp_b_s=gmlp_b_s,
             gmlp_out_norm_w=gmlp_out_norm_w, rwkv_mu=rwkv_mu, rwkv_w0=rwkv_w0, rwkv_w_up=rwkv_w_up,
             rwkv_a0=rwkv_a0, rwkv_a_up=rwkv_a_up, rwkv_g_up=rwkv_g_up, rwkv_k_k=rwkv_k_k, rwkv_k_a=rwkv_k_a,
             rwkv_r_k=rwkv_r_k, rwkv_ln_w=rwkv_ln_w, rwkv_ln_b=rwkv_ln_b)
    Bp, Tq, D = x_prompt.shape
    Bs, Ts, _ = x_sample.shape
    n_pages = page_table.shape[1]
    assert n_pages * PAGE_SIZE == PAST_LEN and PAST_LEN % MOBA_BLOCK == 0
    assert PAST_LEN // MOBA_BLOCK >= MOBA_TOPK and Ts <= MOBA_BLOCK
    assert n_pages % (KMEAN_BLOCKS * PAGES_PER_BLOCK) == 0 and RWKV_CHUNK == RWKV_HEAD
    Rp, Rs = Bp * Tq, Bs * Ts

    assert w_in.shape == (DEPTH, D, IN_COLS)
    w_in_b = _win_prep(jnp.swapaxes(w_in, 1, 2))
    w_out_b = w_out.astype(BF16)
    nw4 = norm_w.reshape(DEPTH, 3, 1, D)
    b_ada3 = b_ada.reshape(DEPTH, 1, N_MOD * D)

    n_c = Bp + Bs
    c_all = jnp.pad(jnp.concatenate([c_prompt, c_sample], 0), ((0, -n_c % 8), (0, 0)))

    pos_p = jnp.arange(Tq, dtype=jnp.int32)
    pos_s = PAST_LEN + jnp.arange(Ts, dtype=jnp.int32)
    cos_p, sin_p = (jnp.tile(t, (Bp, 1)) for t in _rope_tables(pos_p))
    cos_s, sin_s = (jnp.tile(t, (Bs, 1)) for t in _rope_tables(pos_s))

    pt_flat = page_table.reshape(-1)

    zeros = lambda *s: jnp.zeros(s, F32)
    xp = x_prompt.reshape(Rp, D)
    xs = x_sample.reshape(Rs, D)
    TM = ROW_TILE
    outs_p, outs_s = [], []
    for i in range(DEPTH):
        lw = _layer_weights(W, i)
        mod = _ada(c_all, w_ada, b_ada3, i).reshape(-1, N_MOD, D)
        mod_p = _Mod(mod[:Bp].reshape(Bp, N_MOD, 1, D), False, Tq, TM)
        mod_s = _Mod(jnp.repeat(mod[Bp:n_c], Ts, axis=0).transpose(1, 0, 2), True, Ts, Rs)

        xs, *ffn_a = _ffn(xs, mod_s, 0, nw4, ffn_w1, ffn_w3, ffn_w2, i, 0, Rs, FFN_TILE)
        cols = _inproj(xs, mod_s, nw4, w_in_b, i, Rs, INPROJ_TILE)
        q_rot, k_rot, k4, v4 = _moba_prep(cols, cos_s, sin_s, lw['qw'], lw['kw'], Rs)
        y_ssd, y_gm, y_rw, v_gm, st = _mixers(cols, Bs, Ts, SAMPLE_PAD, lw, state_ssd_conv[i], state_ssd[i],
                                              state_rwkv[i], state_rwkv_shift[i], Ts)
        kmean = _kmean_pages(cache_k, pt_flat, i, Bs, n_pages)
        idx = _select(q_rot, kmean, Bs, Ts)
        y_att = _sample_attn(q_rot, k_rot, cols, lw['ow'], cache_k, cache_v, pt_flat, idx.reshape(-1), i, Bs, Ts,
                             n_pages).reshape(Rs, GROUP_WIDTH)
        xs = _outproj(xs, mod_s, (y_ssd, y_att, y_gm, y_rw), w_out_b, i, Rs)
        xs, *ffn_b = _ffn(xs, mod_s, 6, nw4, ffn_w1, ffn_w3, ffn_w2, i, 1, Rs, FFN_TILE)
        shp = (Bs, Ts, MOBA_HEADS, MOBA_HEAD_DIM)
        outs_s.append((k4.reshape(shp), v4.reshape(shp)) + st + (v_gm,))

        xp = _ffn(xp, mod_p, 0, nw4, *ffn_a, i, 0, TM, FFN_TILE)
        cols = _inproj(xp, mod_p, nw4, w_in_b, i, TM, INPROJ_TILE)
        q_rot, k_rot, k4, v4 = _moba_prep(cols, cos_p, sin_p, lw['qw'], lw['kw'], TM)
        y_ssd, y_gm, y_rw, _, st = _mixers(cols, Bp, Tq, Tq, lw, zeros(Bp, SSD_CONV - 1, SSD_CONV_DIM),
                                           zeros(Bp, SSD_HEADS, SSD_HEADDIM, SSD_STATE),
                                           zeros(Bp, RWKV_HEADS, RWKV_HEAD, RWKV_HEAD), zeros(Bp, RWKV_COLS), None)
        y_att = _moba_attn(q_rot.reshape(Bp, Tq, -1), k_rot.reshape(Bp, Tq, -1), cols.reshape(Bp, Tq, COLS),
                           lw['ow']).reshape(Rp, -1)
        xp = _outproj(xp, mod_p, (y_ssd, y_att, y_gm, y_rw), w_out_b, i, TM)
        xp = _ffn(xp, mod_p, 6, nw4, *ffn_b, i, 1, TM, FFN_TILE)
        shp = (Bp, Tq, MOBA_HEADS, MOBA_HEAD_DIM)
        outs_p.append((k4.reshape(shp), v4.reshape(shp)) + st)

    k_p, v_p, ssd_p, conv_p, rwkv_p, shift_p = (jnp.stack(s) for s in zip(*outs_p))
    k_s, v_s, ssd_s, conv_s, rwkv_s, shift_s, gmlp_v_s = (jnp.stack(s) for s in zip(*outs_s))
    return (xp.reshape(Bp, Tq, D), xs.reshape(Bs, Ts, D), k_p, v_p, k_s, v_s, ssd_p, ssd_s, conv_p, conv_s,
            rwkv_p, rwkv_s, shift_p, shift_s, gmlp_v_s)
```

```python
import functools
import math

import jax
import jax.numpy as jnp
from jax import lax
from jax.experimental import pallas as pl
from jax.experimental.pallas import tpu as pltpu

F32 = jnp.float32
BF16 = jnp.bfloat16
HI = lax.Precision.HIGHEST

D_MODEL = 2048
DEPTH = 2
PAST_LEN = 16384
PAGE_SIZE = 128
GROUP_WIDTH = 512
SSD_HEADS = 8
SSD_HEADDIM = 64
SSD_STATE = 128
SSD_CONV = 4
SSD_CHUNK = 128
SSD_CONV_DIM = 1024
MOBA_HEADS = 4
MOBA_HEAD_DIM = 128
MOBA_BLOCK = 256
MOBA_TOPK = 3
ROPE_THETA = 10000.0
GMLP_CHUNK = 128
GMLP_HEADS = 4
GMLP_CHUNKS_PER_STEP = 8
RWKV_HEADS = 8
RWKV_HEAD = 64
RWKV_COLS = 1792
RWKV_CHUNK = 64
RWKV_SUB = 16
RWKV_SEQS_PER_STEP = 4
RWKV_DECAY_SCALE = 0.606531
RWKV_LN_EPS = 64e-5
D_FF = 5632
N_MOD = 9
FFN_RES = 0.5
EPS = 1e-6
NEG = -1e30

COLS = 6144
IN_COLS = 5896
C_Z, C_X, C_BC = 0, 512, 1024
C_MQ, C_MK, C_MV = 1536, 2048, 2560
C_GU, C_GV = 3072, 3584
C_RR, C_RK, C_RV, C_RL = 4096, 4608, 5120, 5632
C_DT = 5888

LANE = 128
SAMPLE_PAD = 128
VMEM_LIMIT = 56 * 1024 * 1024
ROW_TILE = 512
FFN_TILE = 512
INPROJ_TILE = 3072


def _params(n_axes, vmem=VMEM_LIMIT):
    return pltpu.CompilerParams(dimension_semantics=("arbitrary",) * n_axes, vmem_limit_bytes=vmem)


def _dot(a, b, precision=None):
    return jnp.dot(a, b, preferred_element_type=F32, precision=precision)


def _dot_t(a, b, precision=None):
    return lax.dot_general(a, b, (((1,), (1,)), ((), ())), preferred_element_type=F32, precision=precision)


def _iota(shape, dim):
    return lax.broadcasted_iota(jnp.int32, shape, dim)


def _rms(x, w):
    return x * lax.rsqrt(jnp.mean(x * x, -1, keepdims=True) + EPS) * w


def _silu(x):
    return x * jax.nn.sigmoid(x)


def _ada_kernel(c_ref, w_ref, b_ref, o_ref):
    s = _silu(c_ref[...]).astype(BF16)
    o_ref[...] = _dot(s, w_ref[...].astype(BF16)) + b_ref[...]


def _ada(c_all, w_ada, b_ada3, layer):
    rows = c_all.shape[0]
    n_out = w_ada.shape[-1]
    tn = 1024
    return pl.pallas_call(
        _ada_kernel,
        grid=(n_out // tn,),
        in_specs=[
            pl.BlockSpec((rows, D_MODEL), lambda n: (0, 0)),
            pl.BlockSpec((None, D_MODEL, tn), lambda n: (layer, 0, n)),
            pl.BlockSpec((None, 1, tn), lambda n: (layer, 0, n)),
        ],
        out_specs=pl.BlockSpec((rows, tn), lambda n: (0, n)),
        out_shape=jax.ShapeDtypeStruct((rows, n_out), F32),
        compiler_params=_params(1),
        name="ada",
    )(c_all, w_ada, b_ada3)


class _Mod:
    def __init__(self, arr, per_row, rows_per_batch, tm):
        self.arr = arr
        self.per_row = per_row
        self.tiles_per_batch = None if per_row else rows_per_batch // tm
        self.tm = tm

    def spec(self, j):
        if self.per_row:
            return pl.BlockSpec((None, self.tm, D_MODEL), lambda r, *_: (j, r, 0))
        tpb = self.tiles_per_batch
        return pl.BlockSpec((None, None, 1, D_MODEL), lambda r, *_: (r // tpb, j, 0, 0))


def _ffn_kernel(x_ref, nw_ref, sh_ref, sc_ref, g_ref, w1_ref, w3_ref, w2_ref, o_ref, *rest, nf, emit_bf16):
    f = pl.program_id(1)
    h_ref, acc_ref = rest[-2:]

    @pl.when(f == 0)
    def _():
        xn = _rms(x_ref[...], nw_ref[...])
        h_ref[...] = (xn * (1 + sc_ref[...]) + sh_ref[...]).astype(BF16)
        acc_ref[...] = jnp.zeros_like(acc_ref)

    w1, w3, w2 = (w[...].astype(BF16) for w in (w1_ref, w3_ref, w2_ref))
    if emit_bf16:
        for dst, w in zip(rest[:3], (w1, w3, w2)):
            dst[...] = w
    h = h_ref[...]
    a = _dot(h, w1)
    b = _dot(h, w3)
    acc_ref[...] += _dot((_silu(a) * b).astype(BF16), w2)

    @pl.when(f == nf - 1)
    def _():
        o_ref[...] = x_ref[...] + FFN_RES * g_ref[...] * acc_ref[...]


def _ffn(x, mod, j0, nw4, w1, w3, w2, layer, slot, tm, tf):
    rows = x.shape[0]
    nf = D_FF // tf
    stacked = w1.ndim == 4
    if stacked:
        up = pl.BlockSpec((None, None, D_MODEL, tf), lambda r, f: (layer, slot, 0, f))
        down = pl.BlockSpec((None, None, tf, D_MODEL), lambda r, f: (layer, slot, f, 0))
    else:
        up = pl.BlockSpec((D_MODEL, tf), lambda r, f: (0, f))
        down = pl.BlockSpec((tf, D_MODEL), lambda r, f: (f, 0))
    out_specs = [pl.BlockSpec((tm, D_MODEL), lambda r, f: (r, 0))]
    out_shape = [jax.ShapeDtypeStruct((rows, D_MODEL), F32)]
    if stacked:
        assert rows == tm
        out_specs += [pl.BlockSpec((D_MODEL, tf), lambda r, f: (0, f))] * 2 + [pl.BlockSpec((tf, D_MODEL), lambda r, f: (f, 0))]
        out_shape += [jax.ShapeDtypeStruct((D_MODEL, D_FF), BF16)] * 2 + [jax.ShapeDtypeStruct((D_FF, D_MODEL), BF16)]
    outs = pl.pallas_call(
        functools.partial(_ffn_kernel, nf=nf, emit_bf16=stacked),
        grid=(rows // tm, nf),
        in_specs=[
            pl.BlockSpec((tm, D_MODEL), lambda r, f: (r, 0)),
            pl.BlockSpec((None, None, 1, D_MODEL), lambda r, f: (layer, 2 * slot, 0, 0)),
            mod.spec(j0), mod.spec(j0 + 1), mod.spec(j0 + 2),
            up, up, down,
        ],
        out_specs=out_specs,
        out_shape=out_shape,
        scratch_shapes=[pltpu.VMEM((tm, D_MODEL), BF16), pltpu.VMEM((tm, D_MODEL), F32)],
        compiler_params=_params(2),
        name="ffn",
    )(x, nw4, mod.arr, mod.arr, mod.arr, w1, w3, w2)
    return outs if stacked else outs[0]


def _inproj_kernel(x_ref, nw_ref, sh_ref, sc_ref, w_ref, o_ref, h_ref):
    @pl.when(pl.program_id(1) == 0)
    def _():
        xn = _rms(x_ref[...], nw_ref[...])
        h_ref[...] = (xn * (1 + sc_ref[...]) + sh_ref[...]).astype(BF16)

    o_ref[...] = _dot_t(h_ref[...], w_ref[...])


def _inproj(x, mod, nw4, w_in_t, layer, tm, tn):
    rows = x.shape[0]
    return pl.pallas_call(
        _inproj_kernel,
        grid=(rows // tm, COLS // tn),
        in_specs=[
            pl.BlockSpec((tm, D_MODEL), lambda r, n: (r, 0)),
            pl.BlockSpec((None, None, 1, D_MODEL), lambda r, n: (layer, 1, 0, 0)),
            mod.spec(3), mod.spec(4),
            pl.BlockSpec((None, tn, D_MODEL), lambda r, n: (layer, n, 0)),
        ],
        out_specs=pl.BlockSpec((tm, tn), lambda r, n: (r, n)),
        out_shape=jax.ShapeDtypeStruct((rows, COLS), F32),
        scratch_shapes=[pltpu.VMEM((tm, D_MODEL), BF16)],
        compiler_params=_params(2),
        name="inproj",
    )(x, nw4, mod.arr, mod.arr, w_in_t)


WIN_TILE = 512


def _win_prep_kernel(w_ref, dt_ref, o_ref, *, n_tiles, tail):
    j = pl.program_id(1)

    @pl.when(j < n_tiles - 1)
    def _():
        o_ref[...] = w_ref[0].astype(BF16)

    @pl.when(j == n_tiles - 1)
    def _():
        o_ref[0:tail, :] = w_ref[0, WIN_TILE - tail:WIN_TILE, :].astype(BF16)
        rest = jnp.concatenate([dt_ref[0], jnp.zeros((WIN_TILE - tail - SSD_HEADS, D_MODEL), F32)], axis=0)
        o_ref[tail:WIN_TILE, :] = rest.astype(BF16)


def _win_prep(w_in_t):
    head = C_BC + GROUP_WIDTH
    n_tiles = COLS // WIN_TILE
    tail = C_DT % WIN_TILE
    assert head % WIN_TILE == 0 and SSD_HEADS == 8 and 0 < tail < WIN_TILE

    def src_row(j):
        shifted = jnp.minimum(j * WIN_TILE + SSD_HEADS, IN_COLS - WIN_TILE)
        return pl.multiple_of(jnp.where(j * WIN_TILE < head, j * WIN_TILE, shifted), SSD_HEADS)

    return pl.pallas_call(
        functools.partial(_win_prep_kernel, n_tiles=n_tiles, tail=tail),
        grid=(DEPTH, n_tiles),
        in_specs=[
            pl.BlockSpec((pl.Element(1), pl.Element(WIN_TILE), pl.Element(D_MODEL)), lambda i, j: (i, src_row(j), 0)),
            pl.BlockSpec((pl.Element(1), pl.Element(SSD_HEADS), pl.Element(D_MODEL)), lambda i, j: (i, head, 0)),
        ],
        out_specs=pl.BlockSpec((None, WIN_TILE, D_MODEL), lambda i, j: (i, j, 0)),
        out_shape=jax.ShapeDtypeStruct((DEPTH, COLS, D_MODEL), BF16),
        compiler_params=_params(2),
        name="w_in_prep",
    )(w_in_t, w_in_t)


def _outproj_kernel(x_ref, g_ref, y0_ref, y1_ref, y2_ref, y3_ref, w_ref, o_ref):
    G = GROUP_WIDTH
    acc = _dot(y0_ref[...], w_ref[0:G, :])
    acc += _dot(y1_ref[...], w_ref[G:2 * G, :])
    acc += _dot(y2_ref[...], w_ref[2 * G:3 * G, :])
    acc += _dot(y3_ref[...], w_ref[3 * G:4 * G, :])
    o_ref[...] = x_ref[...] + g_ref[...] * acc


def _outproj(x, mod, ys, w_out, layer, tm):
    rows = x.shape[0]
    yspec = pl.BlockSpec((tm, GROUP_WIDTH), lambda r: (r, 0))
    return pl.pallas_call(
        _outproj_kernel,
        grid=(rows // tm,),
        in_specs=[
            pl.BlockSpec((tm, D_MODEL), lambda r: (r, 0)),
            mod.spec(5), yspec, yspec, yspec, yspec,
            pl.BlockSpec((None, D_MODEL, D_MODEL), lambda r: (layer, 0, 0)),
        ],
        out_specs=pl.BlockSpec((tm, D_MODEL), lambda r: (r, 0)),
        out_shape=jax.ShapeDtypeStruct((rows, D_MODEL), F32),
        compiler_params=_params(1),
        name="outproj",
    )(x, mod.arr, *ys, w_out)


def _softplus(x):
    return jnp.maximum(x, 0.0) + jnp.log1p(jnp.exp(-jnp.abs(x)))


def _ssd_kernel(z_ref, x_ref, bc_ref, dt_ref, prev_ref, cw_ref, cb_ref, dtb_ref, alog_ref, dskip_ref, nw_ref,
                h0_ref, y_ref, hout_ref, ext_ref, st_ref, *, C, nc, t_valid):
    c = pl.program_id(1)
    G = GROUP_WIDTH

    @pl.when(c == 0)
    def _():
        ext_ref[0:8, :] = prev_ref[...]
        st_ref[...] = h0_ref[...]

    ext_ref[8:8 + C, 0:G] = x_ref[...]
    ext_ref[8:8 + C, G:2 * G] = bc_ref[...]
    conv = cb_ref[...] + ext_ref[5:5 + C, :] * cw_ref[0:1, :]
    for i in range(1, SSD_CONV):
        conv = conv + ext_ref[5 + i:5 + i + C, :] * cw_ref[i:i + 1, :]
    ext_ref[0:8, :] = ext_ref[C:C + 8, :]
    xbc = _silu(conv)
    xs = xbc[:, 0:G]

    dt = _softplus(dt_ref[...] + dtb_ref[...])
    if t_valid is not None:
        dt = jnp.where(c * C + _iota((C, LANE), 0) < t_valid, dt, 0.0)
    a = dt * (-jnp.exp(alog_ref[...]))
    tri = _iota((C, C), 0) >= _iota((C, C), 1)
    tri_b = tri.astype(BF16)
    a_hi, a_mid, a_lo = _split3(a)
    a_cs = _dg(tri_b, a_hi, _NN) + (_dg(tri_b, a_mid, _NN) + _dg(tri_b, a_lo, _NN))
    a_cs_t = a_cs.T
    lane_lo = _iota((C, LANE), 1) < SSD_HEADDIM
    row_lo = _iota((LANE, 1), 0) < SSD_HEADDIM
    col = lambda h: a_cs[:, h:h + 1]
    row = lambda h: a_cs_t[h:h + 1, :]
    last = lambda h: a_cs[C - 1:C, h:h + 1]
    by_head = lambda p, f: jnp.where(lane_lo, f(2 * p), f(2 * p + 1))
    decay_mat = lambda h: jnp.exp(jnp.where(tri, col(h) - row(h), -jnp.inf))

    n_groups = 2
    heads_per_group = SSD_HEADS // n_groups
    bm_s = [_split2(xbc[:, G + g * SSD_STATE:G + (g + 1) * SSD_STATE]) for g in range(n_groups)]
    cm_s = [_split2(xbc[:, G + (n_groups + g) * SSD_STATE:G + (n_groups + g + 1) * SSD_STATE]) for g in range(n_groups)]
    scores = [_dot3(cm_s[g], bm_s[g], _NT) for g in range(n_groups)]
    group = lambda p: 2 * p // heads_per_group
    each = lambda f: [f(p) for p in range(SSD_HEADS // 2)]
    m0_s = each(lambda p: _split2(scores[group(p)] * decay_mat(2 * p)))
    m1_s = each(lambda p: _split2(scores[group(p)] * decay_mat(2 * p + 1)))
    xs_p = each(lambda p: xs[:, p * LANE:(p + 1) * LANE])
    xdt = each(lambda p: xs_p[p] * by_head(p, lambda h: dt[:, h:h + 1]))
    xdt_s = each(lambda p: _split2(xdt[p]))
    y_diag = each(lambda p: jnp.where(lane_lo, _dot3(m0_s[p], xdt_s[p]), _dot3(m1_s[p], xdt_s[p])))
    st = each(lambda p: st_ref[p * LANE:(p + 1) * LANE, :])
    y_off = each(lambda p: _dot3(cm_s[group(p)], _split2(st[p]), _NT) * by_head(p, lambda h: jnp.exp(col(h))))
    decay = each(lambda p: by_head(p, lambda h: jnp.exp(last(h) - col(h))))
    new = each(lambda p: _dot3(_split2(xdt[p] * decay[p]), bm_s[group(p)], _TN))
    for p in range(SSD_HEADS // 2):
        keep = jnp.where(row_lo, jnp.exp(last(2 * p)), jnp.exp(last(2 * p + 1)))
        st_ref[p * LANE:(p + 1) * LANE, :] = st[p] * keep + new[p]
    ys = each(lambda p: y_diag[p] + y_off[p] + xs_p[p] * dskip_ref[:, p * LANE:(p + 1) * LANE])

    y = jnp.concatenate(ys, axis=1) * _silu(z_ref[...])
    y_ref[...] = _rms(y, nw_ref[...]).astype(y_ref.dtype)

    @pl.when(c == nc - 1)
    def _():
        hout_ref[...] = st_ref[...]


def _ssd(cols3, prev8, h0, cw8, cb, dtb, alog, dskip, nw, t_valid):
    B, T, _ = cols3.shape
    C = SSD_CHUNK
    nc = T // C
    G = GROUP_WIDTH
    vec = lambda n: pl.BlockSpec((1, n), lambda b, c: (0, 0))
    return pl.pallas_call(
        functools.partial(_ssd_kernel, C=C, nc=nc, t_valid=t_valid),
        grid=(B, nc),
        in_specs=[
            pl.BlockSpec((None, C, G), lambda b, c: (b, c, C_Z // G)),
            pl.BlockSpec((None, C, G), lambda b, c: (b, c, C_X // G)),
            pl.BlockSpec((None, C, G), lambda b, c: (b, c, C_BC // G)),
            pl.BlockSpec((None, C, LANE), lambda b, c: (b, c, C_DT // LANE)),
            pl.BlockSpec((None, 8, SSD_CONV_DIM), lambda b, c: (b, 0, 0)),
            pl.BlockSpec((8, SSD_CONV_DIM), lambda b, c: (0, 0)),
            vec(SSD_CONV_DIM), vec(LANE), vec(LANE), vec(G), vec(G),
            pl.BlockSpec((None, SSD_HEADS * SSD_HEADDIM, SSD_STATE), lambda b, c: (b, 0, 0)),
        ],
        out_specs=[
            pl.BlockSpec((None, C, G), lambda b, c: (b, c, 0)),
            pl.BlockSpec((None, SSD_HEADS * SSD_HEADDIM, SSD_STATE), lambda b, c: (b, 0, 0)),
        ],
        out_shape=[
            jax.ShapeDtypeStruct((B, T, G), BF16),
            jax.ShapeDtypeStruct((B, SSD_HEADS * SSD_HEADDIM, SSD_STATE), F32),
        ],
        scratch_shapes=[pltpu.VMEM((C + 8, SSD_CONV_DIM), F32), pltpu.VMEM((SSD_HEADS * SSD_HEADDIM, SSD_STATE), F32)],
        compiler_params=_params(2),
        name="ssd",
    )(cols3, cols3, cols3, cols3, prev8, cw8, cb, dtb, alog, dskip, nw, h0)


def _gmlp_kernel(u_ref, v_ref, vw_ref, ws_ref, bs_ref, ow_ref, y_ref, vout_ref, *, C, n_sub):
    u = jax.nn.gelu(u_ref[...])
    v = _rms(jax.nn.gelu(v_ref[...]), vw_ref[...])
    vout_ref[...] = v
    tri = _iota((C, C), 0) >= _iota((C, C), 1)
    mixed = [[None] * GMLP_HEADS for _ in range(n_sub)]
    for h in range(GMLP_HEADS):
        ws = jnp.where(tri, ws_ref[h], 0.0)
        v_h = jnp.concatenate([v[s * C:(s + 1) * C, h * LANE:(h + 1) * LANE] for s in range(n_sub)], axis=1)
        m_h = _dot3(_split2(ws), _split2(v_h)) + bs_ref[h]
        for s in range(n_sub):
            mixed[s][h] = m_h[:, s * LANE:(s + 1) * LANE]
    y = u * jnp.concatenate([jnp.concatenate(row, axis=1) for row in mixed], axis=0)
    y_ref[...] = _rms(y, ow_ref[...]).astype(y_ref.dtype)


def _gmlp(cols3, vw, ws, bs, ow):
    B, T, _ = cols3.shape
    C = GMLP_CHUNK
    G = GROUP_WIDTH
    vec = pl.BlockSpec((1, G), lambda b, c: (0, 0))
    n_sub = math.gcd(T // C, GMLP_CHUNKS_PER_STEP)
    rows = n_sub * C
    return pl.pallas_call(
        functools.partial(_gmlp_kernel, C=C, n_sub=n_sub),
        grid=(B, T // rows),
        in_specs=[
            pl.BlockSpec((None, rows, G), lambda b, c: (b, c, C_GU // G)),
            pl.BlockSpec((None, rows, G), lambda b, c: (b, c, C_GV // G)),
            vec,
            pl.BlockSpec((GMLP_HEADS, C, C), lambda b, c: (0, 0, 0)),
            pl.BlockSpec((GMLP_HEADS, C, 1), lambda b, c: (0, 0, 0)),
            vec,
        ],
        out_specs=[pl.BlockSpec((None, rows, G), lambda b, c: (b, c, 0))] * 2,
        out_shape=[jax.ShapeDtypeStruct((B, T, G), BF16), jax.ShapeDtypeStruct((B, T, G), F32)],
        compiler_params=_params(2),
        name="gmlp",
    )(cols3, cols3, vw, ws, bs, ow)


_NN = ((1,), (0,))
_NT = ((1,), (1,))
_TN = ((0,), (0,))


def _split2(x):
    hi = x.astype(BF16)
    return hi, (x - hi.astype(F32)).astype(BF16)


def _split3(x):
    hi = x.astype(BF16)
    rest = x - hi.astype(F32)
    mid = rest.astype(BF16)
    return hi, mid, (rest - mid.astype(F32)).astype(BF16)


def _dg(a, b, dims):
    return lax.dot_general(a, b, (dims, ((), ())), preferred_element_type=F32)


def _dot3(a, b, dims=_NN):
    (ah, al), (bh, bl) = a, b
    return _dg(ah, bh, dims) + (_dg(ah, bl, dims) + _dg(al, bh, dims))


def _block_diag(pieces, same_head):
    return tuple(jnp.where(same_head, jnp.concatenate([x, x], axis=0), jnp.zeros((), x.dtype)) for x in pieces)


def _rwkv_kernel(r_ref, k_ref, v_ref, lo_ref, prev_ref, mu_ref, w0_ref, wup_ref, a0_ref, aup_ref, gup_ref,
                 kk_ref, ka_ref, rk_ref, lnw_ref, lnb_ref, s0_ref, y_ref, sout_ref, ext_ref, st_ref,
                 *, C, nc, t_valid):
    c = pl.program_id(1)
    G = GROUP_WIDTH
    N = RWKV_HEAD
    P = 2 * N

    n_batch = r_ref.shape[0]
    n_pairs = G // P
    sls = [slice(p * P, (p + 1) * P) for p in range(n_pairs)]
    same_head_n = (_iota((P, P), 0) < N) == (_iota((P, P), 1) < N)
    ones_bd = same_head_n.astype(BF16)
    tri_cc = (_iota((C, C), 0) >= _iota((C, C), 1)).astype(BF16)

    @pl.when(c == 0)
    def _():
        ext_ref[:, 0:8, :] = prev_ref[...]
        st_ref[...] = s0_ref[...]

    def head_sum(x):
        def pair_sum(p):
            hi, lo = _split2(x[:, sls[p]])
            return _dg(hi, ones_bd, _NN) + _dg(lo, ones_bd, _NN)
        return jnp.concatenate([pair_sum(p) for p in range(n_pairs)], axis=1)

    def token_shift(n):
        ext_ref[n, 8:8 + C, 0:G] = r_ref[n]
        ext_ref[n, 8:8 + C, G:2 * G] = k_ref[n]
        ext_ref[n, 8:8 + C, 2 * G:3 * G] = v_ref[n]
        ext_ref[n, 8:8 + C, 3 * G:RWKV_COLS] = lo_ref[n]
        cur = ext_ref[n, 8:8 + C, :]
        prev = ext_ref[n, 7:7 + C, :]
        xs = cur + (prev - cur) * mu_ref[...]
        ext_ref[n, 0:8, :] = ext_ref[n, C:C + 8, :]
        return xs

    shifted = [token_shift(n) for n in range(n_batch)]
    la_all = jnp.concatenate([xs[:, 3 * G:3 * G + P] for xs in shifted], axis=0)
    gl_all = jnp.concatenate([xs[:, 3 * G + P:RWKV_COLS] for xs in shifted], axis=0)
    w_log_all = -RWKV_DECAY_SCALE * jax.nn.sigmoid(
        w0_ref[...] + _dot3(_split2(jnp.tanh(la_all)), _split2(wup_ref[...])))
    a_all = jax.nn.sigmoid(a0_ref[...] + _dot3(_split2(la_all), _split2(aup_ref[...])))
    g_all = _dot3(_split2(jax.nn.sigmoid(gl_all)), _split2(gup_ref[...]))

    def chunk_inputs(n):
        xs = shifted[n]
        r, k, v = xs[:, 0:G], xs[:, G:2 * G], xs[:, 2 * G:3 * G]
        w_log, a, g = (t[n * C:(n + 1) * C] for t in (w_log_all, a_all, g_all))

        kk = k * kk_ref[...]
        kk = kk * lax.rsqrt(jnp.maximum(head_sum(kk * kk), 1e-12))
        k2 = k * (1 + (a - 1) * ka_ref[...])
        if t_valid is not None:
            ok = c * C + _iota((C, G), 0) < t_valid
            w_log = jnp.where(ok, w_log, 0.0)
            kk = jnp.where(ok, kk, 0.0)
            k2 = jnp.where(ok, k2, 0.0)
        b = kk * a

        w_hi, w_mid, w_lo = _split3(w_log)
        cl = _dg(tri_cc, w_hi, _NN) + (_dg(tri_cc, w_mid, _NN) + _dg(tri_cc, w_lo, _NN))
        cl_last = cl[C - 1:C, :]
        einv = jnp.exp(-cl)
        e_c = jnp.exp(cl_last - cl)
        return dict(r=r, v=v, k2=k2, g=g, kkp=kk * jnp.exp(cl - w_log), rp=r * jnp.exp(cl), bi=b * einv,
                    ki=k2 * einv, bt=b * e_c, kt=k2 * e_c, p_c=jnp.exp(cl_last))

    seqs = [chunk_inputs(n) for n in range(n_batch)]
    units = [(n, p) for n in range(n_batch) for p in range(n_pairs)]
    each = lambda f: [f(u) for u in range(len(units))]
    tile = lambda name: each(lambda u: seqs[units[u][0]][name][:, sls[units[u][1]]])
    kkp, rp, bi, ki, bt, kt, v_p = (tile(name) for name in ("kkp", "rp", "bi", "ki", "bt", "kt", "v"))

    t_i = _iota((C, P), 0)
    s_i = _iota((C, P), 1) % C
    strict = s_i < t_i
    incl = s_i <= t_i
    diag_blk = (s_i // RWKV_SUB) == (t_i // RWKV_SUB)
    eye = (s_i == t_i).astype(F32)
    same_head = (_iota((P, P), 0) < C) == (_iota((P, P), 1) < C)

    memo = {}

    def sp(x):
        if id(x) not in memo:
            memo[id(x)] = (x, _split2(x))
        return memo[id(x)][1]

    def bd(x):
        if ("bd", id(x)) not in memo:
            memo["bd", id(x)] = (x, _block_diag(sp(x), same_head))
        return memo["bd", id(x)][1]

    def mm(xs, ys):
        return each(lambda p: _dot3(sp(xs[p]), bd(ys[p])))

    kkp_s = each(lambda p: _split2(kkp[p]))
    rp_s = each(lambda p: _split2(rp[p]))
    lhs_s = each(lambda p: tuple(jnp.concatenate([x, y], axis=0) for x, y in zip(kkp_s[p], rp_s[p])))
    ab = each(lambda p: _dot3(lhs_s[p], _block_diag(_split2(bi[p]), same_head_n), _NT))
    ak = each(lambda p: _dot3(lhs_s[p], _block_diag(_split2(ki[p]), same_head_n), _NT))
    a_m = each(lambda p: jnp.where(strict, ab[p][0:C], 0.0))
    b_k = each(lambda p: jnp.where(strict, ak[p][0:C], 0.0))
    r_b = each(lambda p: jnp.where(incl, ab[p][C:2 * C], 0.0))
    r_k = each(lambda p: jnp.where(incl, ak[p][C:2 * C], 0.0))

    n_pow = each(lambda p: jnp.where(diag_blk, -a_m[p], 0.0))
    a_o = each(lambda p: jnp.where(diag_blk, 0.0, a_m[p]))
    t_d = each(lambda p: eye + n_pow[p])
    for _ in range(int(math.log2(RWKV_SUB)) - 1):
        n_pow = mm(n_pow, n_pow)
        step = mm(t_d, n_pow)
        t_d = each(lambda p: t_d[p] + step[p])
    m1 = mm(t_d, a_o)
    m2 = mm(m1, m1)
    im = each(lambda p: eye - m1[p])
    im_m2 = mm(im, m2)
    t_full = mm(each(lambda p: im[p] + im_m2[p]), t_d)

    st = each(lambda u: st_ref[units[u]])
    st_s = each(lambda p: _split2(st[p]))
    bkv = mm(b_k, v_p)
    rhs = each(lambda p: _dot3(kkp_s[p], st_s[p], _NT) + bkv[p])
    u = mm(t_full, rhs)
    rkv = mm(r_k, v_p)
    rbu = mm(r_b, u)
    ys = each(lambda p: _dot3(rp_s[p], st_s[p], _NT) + rkv[p] - rbu[p])
    upd = each(lambda p: _dot3(_split2(jnp.concatenate([v_p[p], -u[p]], axis=0)),
                               _split2(jnp.concatenate([kt[p], bt[p]], axis=0)), _TN))
    for u_i, (n, p) in enumerate(units):
        st_ref[n, p] = st[u_i] * seqs[n]["p_c"][:, sls[p]] + jnp.where(same_head_n, upd[u_i], 0.0)

    for n, seq in enumerate(seqs):
        y = jnp.concatenate(ys[n * n_pairs:(n + 1) * n_pairs], axis=1)
        mean = head_sum(y) * (1.0 / N)
        d = y - mean
        var = head_sum(d * d) * (1.0 / N)
        yn = d * lax.rsqrt(var + RWKV_LN_EPS) * lnw_ref[...] + lnb_ref[...]
        bonus = head_sum(seq["r"] * seq["k2"] * rk_ref[...]) * seq["v"]
        y_ref[n] = ((yn + bonus) * seq["g"]).astype(y_ref.dtype)

    @pl.when(c == nc - 1)
    def _():
        sout_ref[...] = st_ref[...]


def _rwkv(cols3, prev8, s0_bd, mu, w0, wup, a0, aup, gup, kk, ka, rk, lnw, lnb, t_valid):
    B, T, _ = cols3.shape
    C = RWKV_CHUNK
    nc = T // C if t_valid is None else -(-t_valid // C)
    G = GROUP_WIDTH
    P = 2 * RWKV_HEAD
    n_pairs = RWKV_HEADS // 2
    S = RWKV_SEQS_PER_STEP
    vec = lambda n: pl.BlockSpec((1, n), lambda b, c: (0, 0))
    mat = lambda m, n: pl.BlockSpec((m, n), lambda b, c: (0, 0))
    return pl.pallas_call(
        functools.partial(_rwkv_kernel, C=C, nc=nc, t_valid=t_valid),
        grid=(B // S, nc),
        in_specs=[
            pl.BlockSpec((S, C, G), lambda b, c: (b, c, C_RR // G)),
            pl.BlockSpec((S, C, G), lambda b, c: (b, c, C_RK // G)),
            pl.BlockSpec((S, C, G), lambda b, c: (b, c, C_RV // G)),
            pl.BlockSpec((S, C, 2 * P), lambda b, c: (b, c, C_RL // (2 * P))),
            pl.BlockSpec((S, 8, RWKV_COLS), lambda b, c: (b, 0, 0)),
            vec(RWKV_COLS), vec(G), mat(P, G), vec(G), mat(P, G), mat(P, G),
            vec(G), vec(G), vec(G), vec(G), vec(G),
            pl.BlockSpec((S, n_pairs, P, P), lambda b, c: (b, 0, 0, 0)),
        ],
        out_specs=[
            pl.BlockSpec((S, C, G), lambda b, c: (b, c, 0)),
            pl.BlockSpec((S, n_pairs, P, P), lambda b, c: (b, 0, 0, 0)),
        ],
        out_shape=[
            jax.ShapeDtypeStruct((B, nc * C, G), BF16),
            jax.ShapeDtypeStruct((B, n_pairs, P, P), F32),
        ],
        scratch_shapes=[pltpu.VMEM((S, C + 8, RWKV_COLS), F32), pltpu.VMEM((S, n_pairs, P, P), F32)],
        compiler_params=_params(2),
        name="rwkv",
    )(cols3, cols3, cols3, cols3, prev8, mu, w0, wup, a0, aup, gup, kk, ka, rk, lnw, lnb, s0_bd)


def _moba_prep_kernel(q_ref, k_ref, v_ref, cos_ref, sin_ref, qw_ref, kw_ref, qo_ref, ko_ref, k4_ref, v4_ref):
    cos, sin = cos_ref[...], sin_ref[...]

    def rotate(x, w_ref):
        xn = _rms(x, w_ref[...])
        return xn * cos + pltpu.roll(xn, MOBA_HEAD_DIM // 2, 1) * sin

    for h in range(MOBA_HEADS):
        sl = slice(h * LANE, (h + 1) * LANE)
        qo_ref[:, sl] = rotate(q_ref[:, sl], qw_ref)
        k_h = rotate(k_ref[:, sl], kw_ref)
        ko_ref[:, sl] = k_h
        k4_ref[:, h, :] = k_h
        v4_ref[:, h, :] = v_ref[:, sl]


def _moba_prep(cols, cos, sin, qw, kw, tm):
    rows = cols.shape[0]
    G = GROUP_WIDTH
    tab = pl.BlockSpec((tm, LANE), lambda r: (r, 0))
    vec = pl.BlockSpec((1, LANE), lambda r: (0, 0))
    flat = pl.BlockSpec((tm, G), lambda r: (r, 0))
    heads = pl.BlockSpec((tm, MOBA_HEADS, MOBA_HEAD_DIM), lambda r: (r, 0, 0))
    return pl.pallas_call(
        _moba_prep_kernel,
        grid=(rows // tm,),
        in_specs=[
            pl.BlockSpec((tm, G), lambda r: (r, C_MQ // G)),
            pl.BlockSpec((tm, G), lambda r: (r, C_MK // G)),
            pl.BlockSpec((tm, G), lambda r: (r, C_MV // G)),
            tab, tab, vec, vec,
        ],
        out_specs=[flat, flat, heads, heads],
        out_shape=[jax.ShapeDtypeStruct((rows, G), F32)] * 2
        + [jax.ShapeDtypeStruct((rows, MOBA_HEADS, MOBA_HEAD_DIM), F32)] * 2,
        compiler_params=_params(1),
        name="moba_prep",
    )(cols, cols, cols, cos, sin, qw, kw)


def _top_lanes(gate, lane_f):
    g = gate
    big = float(gate.shape[-1])
    idxs = []
    for _ in range(MOBA_TOPK):
        m = jnp.max(g, axis=-1, keepdims=True)
        idx = jnp.min(jnp.where(g == m, lane_f, big), axis=-1, keepdims=True)
        g = jnp.where(lane_f == idx, -jnp.inf, g)
        idxs.append(idx)
    return idxs


def _moba_attn_kernel(q_ref, k_ref, v_ref, ow_ref, y_ref, km_ref, kb_ref, vt_ref, sel_ref, o_ref, *, nb):
    qi = pl.program_id(1)
    BLK = MOBA_BLOCK
    scale = MOBA_HEAD_DIM ** -0.5

    @pl.when(qi == 0)
    def _():
        for j in range(nb):
            rows = slice(j * BLK, (j + 1) * BLK)
            k_blk = k_ref[rows, :]
            km_ref[j:j + 1, :] = jnp.mean(k_blk, axis=0, keepdims=True)
            kb_ref[rows, :] = k_blk.astype(BF16)
            vt_ref[:, rows] = v_ref[rows, :].T.astype(BF16)

    blk_i = _iota((nb, BLK), 0)
    blk_f = blk_i.astype(F32)
    past = blk_i < qi
    causal = _iota((BLK, BLK), 0) <= _iota((BLK, BLK), 1)
    own = pl.multiple_of(qi * BLK, BLK)

    sls = [slice(h * LANE, (h + 1) * LANE) for h in range(MOBA_HEADS)]
    each = lambda f: [f(h) for h in range(MOBA_HEADS)]
    q_t = each(lambda h: q_ref[:, sls[h]].T)
    gate = each(lambda h: jnp.where(past, _dot3(_split2(km_ref[:, sls[h]]), _split2(q_t[h])), NEG))
    sel = each(lambda h: jnp.zeros((nb, BLK), jnp.bool_))
    for _ in range(MOBA_TOPK):
        best = each(lambda h: jnp.max(gate[h], axis=0, keepdims=True))
        idx = each(lambda h: jnp.min(jnp.where(gate[h] == best[h], blk_f, float(nb)), axis=0, keepdims=True))
        sel = each(lambda h: sel[h] | (blk_f == idx[h]))
        gate = each(lambda h: jnp.where(blk_f == idx[h], -jnp.inf, gate[h]))
    for h in range(MOBA_HEADS):
        sel_ref[h] = (sel[h] & past).astype(F32)
    qb_t = each(lambda h: q_t[h].astype(BF16))

    def block_scores(start, h):
        return _dot(kb_ref[pl.ds(start, BLK), sls[h]], qb_t[h]) * scale

    s0 = each(lambda h: jnp.where(causal, block_scores(own, h), NEG))
    m0 = each(lambda h: jnp.max(s0[h], axis=0, keepdims=True))
    p0 = each(lambda h: jnp.exp(s0[h] - m0[h]))
    l0 = each(lambda h: jnp.sum(p0[h], axis=0, keepdims=True))
    acc0 = each(lambda h: _dot(vt_ref[sls[h], pl.ds(own, BLK)], p0[h].astype(BF16)))

    def body(j, carry):
        m_i, l_i, acc = carry
        start = pl.multiple_of(j * BLK, BLK)
        s = each(lambda h: jnp.where(sel_ref[h, pl.ds(j, 1), :] > 0.0, block_scores(start, h), NEG))
        m_n = each(lambda h: jnp.maximum(m_i[h], jnp.max(s[h], axis=0, keepdims=True)))
        alpha = each(lambda h: jnp.exp(m_i[h] - m_n[h]))
        p = each(lambda h: jnp.exp(s[h] - m_n[h]))
        l_n = each(lambda h: alpha[h] * l_i[h] + jnp.sum(p[h], axis=0, keepdims=True))
        acc_n = each(lambda h: alpha[h] * acc[h] + _dot(vt_ref[sls[h], pl.ds(start, BLK)], p[h].astype(BF16)))
        return tuple(m_n), tuple(l_n), tuple(acc_n)

    _, l_f, acc_f = lax.fori_loop(0, qi, body, (tuple(m0), tuple(l0), tuple(acc0)))
    for h in range(MOBA_HEADS):
        o_ref[:, sls[h]] = (acc_f[h] / l_f[h]).T

    y_ref[...] = _rms(o_ref[...], ow_ref[...]).astype(y_ref.dtype)


def _moba_attn(q3, k3, cols3, ow):
    B, T, G = q3.shape
    BLK = MOBA_BLOCK
    nb = T // BLK
    return pl.pallas_call(
        functools.partial(_moba_attn_kernel, nb=nb),
        grid=(B, nb),
        in_specs=[
            pl.BlockSpec((None, BLK, G), lambda b, i: (b, i, 0)),
            pl.BlockSpec((None, T, G), lambda b, i: (b, 0, 0)),
            pl.BlockSpec((None, T, G), lambda b, i: (b, 0, C_MV // G)),
            pl.BlockSpec((1, G), lambda b, i: (0, 0)),
        ],
        out_specs=pl.BlockSpec((None, BLK, G), lambda b, i: (b, i, 0)),
        out_shape=jax.ShapeDtypeStruct((B, T, G), BF16),
        scratch_shapes=[pltpu.VMEM((nb, G), F32), pltpu.VMEM((T, G), BF16), pltpu.VMEM((G, T), BF16),
                        pltpu.VMEM((MOBA_HEADS, nb, BLK), F32), pltpu.VMEM((BLK, G), F32)],
        compiler_params=_params(2),
        name="moba_attn",
    )(q3, k3, cols3, ow)


PAGES_PER_BLOCK = MOBA_BLOCK // PAGE_SIZE
KMEAN_BLOCKS = 32


def _kmean_kernel(pt_ref, *refs):
    page_refs, o_ref = refs[:-1], refs[-1]
    for j in range(KMEAN_BLOCKS):
        s = jnp.sum(page_refs[PAGES_PER_BLOCK * j][...], axis=0)
        for o in range(1, PAGES_PER_BLOCK):
            s = s + jnp.sum(page_refs[PAGES_PER_BLOCK * j + o][...], axis=0)
        o_ref[j] = s * (1.0 / MOBA_BLOCK)


def _kmean_pages(cache_k, page_table_flat, layer, batch, n_pages):
    nbk = n_pages // PAGES_PER_BLOCK
    per_step = KMEAN_BLOCKS * PAGES_PER_BLOCK

    def page(o):
        return pl.BlockSpec((None, None, PAGE_SIZE, MOBA_HEADS, MOBA_HEAD_DIM),
                            lambda b, j, pt: (layer, pt[b * n_pages + j * per_step + o], 0, 0, 0))

    return pl.pallas_call(
        _kmean_kernel,
        grid_spec=pltpu.PrefetchScalarGridSpec(
            num_scalar_prefetch=1,
            grid=(batch, nbk // KMEAN_BLOCKS),
            in_specs=[page(o) for o in range(per_step)],
            out_specs=pl.BlockSpec((None, KMEAN_BLOCKS, MOBA_HEADS, MOBA_HEAD_DIM), lambda b, j, pt: (b, j, 0, 0)),
        ),
        out_shape=jax.ShapeDtypeStruct((batch, nbk, MOBA_HEADS, MOBA_HEAD_DIM), F32),
        compiler_params=_params(2),
        name="moba_kmean_pages",
    )(page_table_flat, *([cache_k] * per_step))


def _select_kernel(q_ref, km_ref, idx_ref, *, batch, t_new, nbk):
    rows = batch * t_new
    lane_i = _iota((rows, LANE), 1)
    lane_f = lane_i.astype(F32)
    row_b = _iota((rows, LANE), 0) // t_new
    out = jnp.zeros((rows, LANE), F32)
    pad = jnp.zeros((LANE - nbk, LANE), F32)
    for h in range(MOBA_HEADS):
        sl = slice(h * LANE, (h + 1) * LANE)
        q = q_ref[:, sl]
        gate = jnp.zeros((rows, LANE), F32)
        for b in range(batch):
            km = jnp.concatenate([km_ref[b, :, h, :], pad], axis=0)
            gate = jnp.where(row_b == b, _dot_t(q, km, HI), gate)
        gate = jnp.where(lane_i < nbk, gate, NEG)
        for kth, idx in enumerate(_top_lanes(gate, lane_f)):
            out = jnp.where(lane_i == h * MOBA_HEADS + kth, idx, out)
    idx_ref[...] = out.astype(jnp.int32)


def _select(q_rot, kmean, batch, t_new):
    rows = batch * t_new
    nbk = kmean.shape[1]
    G = GROUP_WIDTH
    return pl.pallas_call(
        functools.partial(_select_kernel, batch=batch, t_new=t_new, nbk=nbk),
        grid=(1,),
        in_specs=[
            pl.BlockSpec((rows, G), lambda i: (0, 0)),
            pl.BlockSpec((batch, nbk, MOBA_HEADS, MOBA_HEAD_DIM), lambda i: (0, 0, 0, 0)),
        ],
        out_specs=pl.BlockSpec((rows, LANE), lambda i: (0, 0)),
        out_shape=jax.ShapeDtypeStruct((rows, LANE), jnp.int32),
        compiler_params=_params(1),
        name="moba_select",
    )(q_rot, kmean)


def _sample_attn_kernel(pt_ref, idx_ref, q_ref, kn_ref, vn_ref, ow_ref, ck_ref, cv_ref, y_ref,
                        kbuf, vbuf, sems, o_ref, *, layer, t_new, n_pages):
    b = pl.program_id(0)
    rows = kn_ref.shape[0]
    scale = MOBA_HEAD_DIM ** -0.5
    per_query = PAGES_PER_BLOCK * MOBA_TOPK
    queries = [(t, h) for t in range(t_new) for h in range(MOBA_HEADS)]

    def page_copies(qn):
        t, h = queries[qn]
        out = []
        for kth in range(MOBA_TOPK):
            blk = idx_ref[(b * t_new + t) * LANE + h * MOBA_HEADS + kth]
            for o in range(PAGES_PER_BLOCK):
                page = pt_ref[b * n_pages + PAGES_PER_BLOCK * blk + o]
                slot = qn * per_query + kth * PAGES_PER_BLOCK + o
                out.append(pltpu.make_async_copy(ck_ref.at[layer, page, :, h, :], kbuf.at[slot], sems.at[0, qn]))
                out.append(pltpu.make_async_copy(cv_ref.at[layer, page, :, h, :], vbuf.at[slot], sems.at[1, qn]))
        return out

    copies = [page_copies(qn) for qn in range(len(queries))]
    for group in copies:
        for cp in group:
            cp.start()

    r_i = _iota((8, rows), 1)
    for qn, (t, h) in enumerate(queries):
        sl = slice(h * LANE, (h + 1) * LANE)
        q8 = jnp.broadcast_to(q_ref[t:t + 1, sl], (8, LANE)).astype(BF16)
        s_new = _dot_t(q8, kn_ref[:, sl].astype(BF16)) * scale
        s_new = jnp.where((r_i >= b * t_new) & (r_i <= b * t_new + t), s_new, NEG)
        for cp in copies[qn]:
            cp.wait()
        slots = range(qn * per_query, (qn + 1) * per_query)
        s_old = [_dot_t(q8, kbuf[slot].astype(BF16)) * scale for slot in slots]
        m = jnp.max(s_new, axis=-1, keepdims=True)
        for s in s_old:
            m = jnp.maximum(m, jnp.max(s, axis=-1, keepdims=True))
        p_new = jnp.exp(s_new - m)
        l = jnp.sum(p_new, axis=-1, keepdims=True)
        acc = _dot(p_new.astype(BF16), vn_ref[:, sl].astype(BF16))
        for s, slot in zip(s_old, slots):
            p = jnp.exp(s - m)
            l = l + jnp.sum(p, axis=-1, keepdims=True)
            acc = acc + _dot(p.astype(BF16), vbuf[slot].astype(BF16))
        o_ref[t:t + 1, sl] = (acc / l)[0:1, :]

    y_ref[...] = _rms(o_ref[...], ow_ref[...]).astype(y_ref.dtype)


def _sample_attn(q_rot, k_new, cols, ow, cache_k, cache_v, page_table_flat, idx_flat, layer, batch, t_new, n_pages):
    rows = batch * t_new
    G = GROUP_WIDTH
    n_slots = t_new * MOBA_HEADS * MOBA_TOPK * PAGES_PER_BLOCK
    return pl.pallas_call(
        functools.partial(_sample_attn_kernel, layer=layer, t_new=t_new, n_pages=n_pages),
        grid_spec=pltpu.PrefetchScalarGridSpec(
            num_scalar_prefetch=2,
            grid=(batch,),
            in_specs=[
                pl.BlockSpec((None, t_new, G), lambda b, pt, idx: (b, 0, 0)),
                pl.BlockSpec((rows, G), lambda b, pt, idx: (0, 0)),
                pl.BlockSpec((rows, G), lambda b, pt, idx: (0, C_MV // G)),
                pl.BlockSpec((1, G), lambda b, pt, idx: (0, 0)),
                pl.BlockSpec(memory_space=pl.ANY),
                pl.BlockSpec(memory_space=pl.ANY),
            ],
            out_specs=pl.BlockSpec((None, t_new, G), lambda b, pt, idx: (b, 0, 0)),
            scratch_shapes=[
                pltpu.VMEM((n_slots, PAGE_SIZE, MOBA_HEAD_DIM), F32),
                pltpu.VMEM((n_slots, PAGE_SIZE, MOBA_HEAD_DIM), F32),
                pltpu.SemaphoreType.DMA((2, t_new * MOBA_HEADS)),
                pltpu.VMEM((t_new, G), F32),
            ],
        ),
        out_shape=jax.ShapeDtypeStruct((batch, t_new, G), BF16),
        compiler_params=_params(1),
        name="moba_sample_attn",
    )(page_table_flat, idx_flat, q_rot.reshape(batch, t_new, G), k_new, cols, ow, cache_k, cache_v)


def _rope_tables(pos):
    half = MOBA_HEAD_DIM // 2
    freq = ROPE_THETA ** (-jnp.arange(half, dtype=F32) / half)
    ang = pos.astype(F32)[:, None] * freq[None, :]
    cos, sin = jnp.cos(ang), jnp.sin(ang)
    return jnp.concatenate([cos, cos], -1), jnp.concatenate([-sin, sin], -1)


def _pad_rows_front(x, rows):
    return jnp.pad(x, ((0, 0), (rows - x.shape[1], 0), (0, 0)))


def _pair_block_diag(s):
    B, H, N, _ = s.shape
    s = s.reshape(B, H // 2, 2, N, 1, N) * jnp.eye(2, dtype=s.dtype)[None, None, :, None, :, None]
    return s.reshape(B, H // 2, 2 * N, 2 * N)


def _pair_diag_blocks(s_bd):
    B, n_pairs, P, _ = s_bd.shape
    N = P // 2
    s = s_bd.reshape(B, n_pairs, 2, N, 2, N)
    return jnp.stack([s[:, :, 0, :, 0, :], s[:, :, 1, :, 1, :]], axis=2).reshape(B, 2 * n_pairs, N, N)


def _layer_weights(W, i):
    G = GROUP_WIDTH
    row = lambda v: v.reshape(1, -1)
    lane_pad = lambda v: jnp.pad(v, (0, LANE - v.shape[0])).reshape(1, LANE)
    zeros_r = jnp.zeros((RWKV_HEAD, G), F32)
    return dict(
        cw8=jnp.pad(W['ssd_conv_w'][i].T, ((0, 8 - SSD_CONV), (0, 0))),
        cb=row(W['ssd_conv_b'][i]),
        dtb=lane_pad(W['ssd_dt_bias'][i]),
        alog=lane_pad(W['ssd_a_log'][i]),
        dskip=row(jnp.repeat(W['ssd_d'][i], GROUP_WIDTH // SSD_HEADS)),
        ssd_nw=row(W['ssd_norm_w'][i]),
        qw=row(W['moba_q_norm_w'][i]), kw=row(W['moba_k_norm_w'][i]), ow=row(W['moba_out_norm_w'][i]),
        gvw=row(W['gmlp_v_norm_w'][i]), gow=row(W['gmlp_out_norm_w'][i]),
        ws=W['gmlp_w_s'][i], bs=W['gmlp_b_s'][i][:, :, None],
        mu=row(W['rwkv_mu'][i]), w0=row(W['rwkv_w0'][i]), a0=row(W['rwkv_a0'][i]),
        wup=jnp.concatenate([W['rwkv_w_up'][i], zeros_r], 0),
        aup=jnp.concatenate([zeros_r, W['rwkv_a_up'][i]], 0),
        gup=W['rwkv_g_up'][i],
        kk=row(W['rwkv_k_k'][i]), ka=row(W['rwkv_k_a'][i]), rk=row(W['rwkv_r_k'][i]),
        lnw=row(W['rwkv_ln_w'][i]), lnb=row(W['rwkv_ln_b'][i]),
    )


def _mixers(cols, B, T, Tp, lw, conv_prev, ssd_prev, rwkv_prev, shift_prev, t_valid):
    G = GROUP_WIDTH
    cols3 = cols.reshape(B, T, COLS)
    if Tp != T:
        cols3 = jnp.pad(cols3, ((0, 0), (0, Tp - T), (0, 0)))
    y_ssd, ssd_new = _ssd(cols3, _pad_rows_front(conv_prev, 8), ssd_prev.reshape(B, SSD_HEADS * SSD_HEADDIM, SSD_STATE),
                          lw['cw8'], lw['cb'], lw['dtb'], lw['alog'], lw['dskip'], lw['ssd_nw'], t_valid)
    y_gm, v_gm = _gmlp(cols3, lw['gvw'], lw['ws'], lw['bs'], lw['gow'])
    y_rw, rwkv_new = _rwkv(cols3, _pad_rows_front(shift_prev[:, None, :], 8), _pair_block_diag(rwkv_prev),
                           lw['mu'], lw['w0'], lw['wup'], lw['a0'], lw['aup'], lw['gup'], lw['kk'], lw['ka'],
                           lw['rk'], lw['lnw'], lw['lnb'], t_valid)
    crop = lambda y: y[:, :T].reshape(B * T, G)
    raw = cols.reshape(B, T, COLS)
    conv_new = raw[:, T - (SSD_CONV - 1):, C_X:C_X + SSD_CONV_DIM]
    shift_new = raw[:, T - 1, C_RR:C_RR + RWKV_COLS]
    states = (ssd_new.reshape(B, SSD_HEADS, SSD_HEADDIM, SSD_STATE), conv_new, _pair_diag_blocks(rwkv_new), shift_new)
    return crop(y_ssd), crop(y_gm), crop(y_rw), v_gm[:, :T], states


def kernel(x_prompt, x_sample, c_prompt, c_sample, cache_k, cache_v, page_table, state_ssd, state_ssd_conv, state_rwkv, state_rwkv_shift, norm_w, w_ada, b_ada, ffn_w1, ffn_w3, ffn_w2, w_in, w_out, ssd_conv_w, ssd_conv_b, ssd_dt_bias, ssd_a_log, ssd_d, ssd_norm_w, moba_q_norm_w, moba_k_norm_w, moba_out_norm_w, gmlp_v_norm_w, gmlp_w_s, gmlp_b_s, gmlp_out_norm_w, rwkv_mu, rwkv_w0, rwkv_w_up, rwkv_a0, rwkv_a_up, rwkv_g_up, rwkv_k_k, rwkv_k_a, rwkv_r_k, rwkv_ln_w, rwkv_ln_b):
    W = dict(ssd_conv_w=ssd_conv_w, ssd_conv_b=ssd_conv_b, ssd_dt_bias=ssd_dt_bias, ssd_a_log=ssd_a_log,
             ssd_d=ssd_d, ssd_norm_w=ssd_norm_w, moba_q_norm_w=moba_q_norm_w, moba_k_norm_w=moba_k_norm_w,
             moba_out_norm_w=moba_out_norm_w, gmlp_v_norm_w=gmlp_v_norm_w, gmlp_w_s=gmlp_w_s, gmlp_b_s=gmlp_b_s,
             gmlp_out_norm_w=gmlp_out_norm_w, rwkv_mu=rwkv_mu, rwkv_w0=rwkv_w0, rwkv_w_up=rwkv_w_up,
             rwkv_a0=rwkv_a0, rwkv_a_up=rwkv_a_up, rwkv_g_up=rwkv_g_up, rwkv_k_k=rwkv_k_k, rwkv_k_a=rwkv_k_a,
             rwkv_r_k=rwkv_r_k, rwkv_ln_w=rwkv_ln_w, rwkv_ln_b=rwkv_ln_b)
    Bp, Tq, D = x_prompt.shape
    Bs, Ts, _ = x_sample.shape
    n_pages = page_table.shape[1]
    assert n_pages * PAGE_SIZE == PAST_LEN and PAST_LEN % MOBA_BLOCK == 0
    assert PAST_LEN // MOBA_BLOCK >= MOBA_TOPK and Ts <= MOBA_BLOCK
    assert n_pages % (KMEAN_BLOCKS * PAGES_PER_BLOCK) == 0 and RWKV_CHUNK == RWKV_HEAD
    Rp, Rs = Bp * Tq, Bs * Ts

    assert w_in.shape == (DEPTH, D, IN_COLS)
    w_in_b = _win_prep(jnp.swapaxes(w_in, 1, 2))
    w_out_b = w_out.astype(BF16)
    nw4 = norm_w.reshape(DEPTH, 3, 1, D)
    b_ada3 = b_ada.reshape(DEPTH, 1, N_MOD * D)

    n_c = Bp + Bs
    c_all = jnp.pad(jnp.concatenate([c_prompt, c_sample], 0), ((0, -n_c % 8), (0, 0)))

    pos_p = jnp.arange(Tq, dtype=jnp.int32)
    pos_s = PAST_LEN + jnp.arange(Ts, dtype=jnp.int32)
    cos_p, sin_p = (jnp.tile(t, (Bp, 1)) for t in _rope_tables(pos_p))
    cos_s, sin_s = (jnp.tile(t, (Bs, 1)) for t in _rope_tables(pos_s))

    pt_flat = page_table.reshape(-1)

    zeros = lambda *s: jnp.zeros(s, F32)
    xp = x_prompt.reshape(Rp, D)
    xs = x_sample.reshape(Rs, D)
    TM = ROW_TILE
    outs_p, outs_s = [], []
    for i in range(DEPTH):
        lw = _layer_weights(W, i)
        mod = _ada(c_all, w_ada, b_ada3, i).reshape(-1, N_MOD, D)
        mod_p = _Mod(mod[:Bp].reshape(Bp, N_MOD, 1, D), False, Tq, TM)
        mod_s = _Mod(jnp.repeat(mod[Bp:n_c], Ts, axis=0).transpose(1, 0, 2), True, Ts, Rs)

        xs, *ffn_a = _ffn(xs, mod_s, 0, nw4, ffn_w1, ffn_w3, ffn_w2, i, 0, Rs, FFN_TILE)
        cols = _inproj(xs, mod_s, nw4, w_in_b, i, Rs, INPROJ_TILE)
        q_rot, k_rot, k4, v4 = _moba_prep(cols, cos_s, sin_s, lw['qw'], lw['kw'], Rs)
        y_ssd, y_gm, y_rw, v_gm, st = _mixers(cols, Bs, Ts, SAMPLE_PAD, lw, state_ssd_conv[i], state_ssd[i],
                                              state_rwkv[i], state_rwkv_shift[i], Ts)
        kmean = _kmean_pages(cache_k, pt_flat, i, Bs, n_pages)
        idx = _select(q_rot, kmean, Bs, Ts)
        y_att = _sample_attn(q_rot, k_rot, cols, lw['ow'], cache_k, cache_v, pt_flat, idx.reshape(-1), i, Bs, Ts,
                             n_pages).reshape(Rs, GROUP_WIDTH)
        xs = _outproj(xs, mod_s, (y_ssd, y_att, y_gm, y_rw), w_out_b, i, Rs)
        xs, *ffn_b = _ffn(xs, mod_s, 6, nw4, ffn_w1, ffn_w3, ffn_w2, i, 1, Rs, FFN_TILE)
        shp = (Bs, Ts, MOBA_HEADS, MOBA_HEAD_DIM)
        outs_s.append((k4.reshape(shp), v4.reshape(shp)) + st + (v_gm,))

        xp = _ffn(xp, mod_p, 0, nw4, *ffn_a, i, 0, TM, FFN_TILE)
        cols = _inproj(xp, mod_p, nw4, w_in_b, i, TM, INPROJ_TILE)
        q_rot, k_rot, k4, v4 = _moba_prep(cols, cos_p, sin_p, lw['qw'], lw['kw'], TM)
        y_ssd, y_gm, y_rw, _, st = _mixers(cols, Bp, Tq, Tq, lw, zeros(Bp, SSD_CONV - 1, SSD_CONV_DIM),
                                           zeros(Bp, SSD_HEADS, SSD_HEADDIM, SSD_STATE),
                                           zeros(Bp, RWKV_HEADS, RWKV_HEAD, RWKV_HEAD), zeros(Bp, RWKV_COLS), None)
        y_att = _moba_attn(q_rot.reshape(Bp, Tq, -1), k_rot.reshape(Bp, Tq, -1), cols.reshape(Bp, Tq, COLS),
                           lw['ow']).reshape(Rp, -1)
        xp = _outproj(xp, mod_p, (y_ssd, y_att, y_gm, y_rw), w_out_b, i, TM)
        xp = _ffn(xp, mod_p, 6, nw4, *ffn_b, i, 1, TM, FFN_TILE)
        shp = (Bp, Tq, MOBA_HEADS, MOBA_HEAD_DIM)
        outs_p.append((k4.reshape(shp), v4.reshape(shp)) + st)

    k_p, v_p, ssd_p, conv_p, rwkv_p, shift_p = (jnp.stack(s) for s in zip(*outs_p))
    k_s, v_s, ssd_s, conv_s, rwkv_s, shift_s, gmlp_v_s = (jnp.stack(s) for s in zip(*outs_s))
    return (xp.reshape(Bp, Tq, D), xs.reshape(Bs, Ts, D), k_p, v_p, k_s, v_s, ssd_p, ssd_s, conv_p, conv_s,
            rwkv_p, rwkv_s, shift_p, shift_s, gmlp_v_s)
```
